```python
import jax, jax.numpy as jnp
from jax import lax
import numpy as np

D_MODEL = 2048
BATCH = 1
SEQ = 8192
DEPTH = 2
DEC_BATCH = 2
DEC_SEQ = 16384
PAST_LEN = 128

N_MIXERS = 2
N_MLA_LAYERS = (DEPTH + 1) // 2
N_RWKV_LAYERS = DEPTH // 2
NORM_EPS = 1e-6

MLA_HEADS = 16
Q_LORA = 512
KV_LORA = 512
NOPE_DIM = 128
ROPE_DIM = 64
V_DIM = 128
ROPE_THETA = 10000.0
Q_BLOCK = 128
MLA_IN = Q_LORA + KV_LORA + ROPE_DIM

RWKV_HEAD = 64
RWKV_HEADS = D_MODEL // RWKV_HEAD
DECAY_LORA = 96
AAA_LORA = 96
GATE_LORA = 256
LNX_EPS = 64e-5
N_LERP = 6

D_FF = ((8 * D_MODEL + 3 * 256 - 1) // (3 * 256)) * 256

kernel_name = 'hybrid_mla_rwkv7_bidir_encoder'


def rms_norm(x, g):
    xf = x.astype(jnp.float32)
    y = xf * lax.rsqrt(jnp.mean(xf * xf, axis=-1, keepdims=True) + NORM_EPS)
    return (y * g.astype(jnp.float32)).astype(x.dtype)


def rope_tables(S):
    half = ROPE_DIM // 2
    inv = 1.0 / (ROPE_THETA ** (jnp.arange(half, dtype=jnp.float32) * (2.0 / ROPE_DIM)))
    ang = jnp.arange(S, dtype=jnp.float32)[:, None] * inv[None, :]
    return jnp.cos(ang), jnp.sin(ang)


def apply_rope(x, cos, sin):
    half = ROPE_DIM // 2
    xf = x.astype(jnp.float32)
    x1, x2 = xf[..., :half], xf[..., half:]
    return jnp.concatenate([x1 * cos - x2 * sin, x1 * sin + x2 * cos], axis=-1).astype(x.dtype)


def mla_mixer(x, w_in, g_q, g_kv, w_uq, w_ukv, w_o):
    B, S, _ = x.shape
    H = MLA_HEADS
    h = jnp.einsum('bsd,de->bse', x, w_in)
    c_q = rms_norm(h[..., :Q_LORA], g_q)
    c_kv = rms_norm(h[..., Q_LORA:Q_LORA + KV_LORA], g_kv)
    k_pe = h[..., Q_LORA + KV_LORA:]
    q = jnp.einsum('bsc,ce->bse', c_q, w_uq).reshape(B, S, H, NOPE_DIM + ROPE_DIM)
    kv = jnp.einsum('bsc,ce->bse', c_kv, w_ukv).reshape(B, S, H, NOPE_DIM + V_DIM)
    q_nope, q_pe = q[..., :NOPE_DIM], q[..., NOPE_DIM:]
    k_nope, v = kv[..., :NOPE_DIM], kv[..., NOPE_DIM:]
    cos, sin = rope_tables(S)
    q_pe = apply_rope(q_pe, cos[:, None, :], sin[:, None, :])
    k_pe = apply_rope(k_pe, cos, sin)
    scale = (NOPE_DIM + ROPE_DIM) ** -0.5
    n_blk = S // Q_BLOCK
    qn_blk = q_nope.reshape(B, n_blk, Q_BLOCK, H, NOPE_DIM).transpose(1, 0, 2, 3, 4)
    qp_blk = q_pe.reshape(B, n_blk, Q_BLOCK, H, ROPE_DIM).transpose(1, 0, 2, 3, 4)

    def attend(blk):
        qn, qp = blk
        s = (jnp.einsum('bqhd,bkhd->bhqk', qn, k_nope, preferred_element_type=jnp.float32)
             + jnp.einsum('bqhr,bkr->bhqk', qp, k_pe, preferred_element_type=jnp.float32))
        p = jax.nn.softmax(s * scale, axis=-1).astype(v.dtype)
        return jnp.einsum('bhqk,bkhd->bqhd', p, v)

    o = lax.map(attend, (qn_blk, qp_blk))
    o = o.transpose(1, 0, 2, 3, 4).reshape(B, S, H * V_DIM)
    return jnp.einsum('bse,ed->bsd', o, w_o)


def centred_shift(x):
    x_prev = jnp.pad(x[:, :-1], ((0, 0), (1, 0), (0, 0)))
    x_next = jnp.pad(x[:, 1:], ((0, 0), (0, 1), (0, 0)))
    return 0.5 * (x_prev + x_next)


def wkv7_scan(r, decay, k, v, kk, a, reverse):
    B, S, H, N = r.shape

    def step(state, inp):
        r_t, w_t, k_t, v_t, kk_t, a_t = inp
        sa = -jnp.einsum('bhvk,bhk->bhv', state, kk_t)
        state = (state * w_t[:, :, None, :]
                 + sa[..., None] * (kk_t * a_t)[:, :, None, :]
                 + v_t[..., None] * k_t[:, :, None, :])
        return state, jnp.einsum('bhvk,bhk->bhv', state, r_t)

    seq = tuple(t.transpose(1, 0, 2, 3) for t in (r, decay, k, v, kk, a))
    init = jnp.zeros((B, H, N, N), jnp.float32)
    _, ys = lax.scan(step, init, seq, reverse=reverse)
    return ys.transpose(1, 0, 2, 3)


def rwkv7_mixer(x, mu, w_r, w_k, w_v, w_o, w0, w1, w2, a0, a1, a2, g1, g2,
                k_k, k_a, r_k, lnx_g, lnx_b):
    B, S, D = x.shape
    H, N = RWKV_HEADS, RWKV_HEAD
    f32 = jnp.float32
    xx = centred_shift(x) - x
    xr, xw, xk, xv, xa, xg = [x + xx * mu[i] for i in range(N_LERP)]
    r = (xr @ w_r).astype(f32).reshape(B, S, H, N)
    k = (xk @ w_k).astype(f32)
    v = (xv @ w_v).astype(f32).reshape(B, S, H, N)
    g = jax.nn.sigmoid(xg @ g1) @ g2
    kk = (k * k_k).reshape(B, S, H, N)
    kk = kk / jnp.maximum(jnp.sqrt(jnp.sum(kk * kk, axis=-1, keepdims=True)), 1e-12)
    ys, bonuses = [], []
    for d, reverse in ((0, False), (1, True)):
        w_log = -jax.nn.softplus(-(w0[d] + jnp.tanh(xw @ w1[d]) @ w2[d]).astype(f32)) - 0.5
        decay = jnp.exp(-jnp.exp(w_log)).reshape(B, S, H, N)
        a = jax.nn.sigmoid((a0[d] + (xa @ a1[d]) @ a2[d]).astype(f32))
        k_d = (k * (1.0 + (a - 1.0) * k_a)).reshape(B, S, H, N)
        a = a.reshape(B, S, H, N)
        ys.append(wkv7_scan(r, decay, k_d, v, kk, a, reverse))
        bonuses.append(jnp.sum(r * k_d * r_k, axis=-1, keepdims=True) * v)
    y = ys[0] + ys[1]
    mean = jnp.mean(y, axis=-1, keepdims=True)
    var = jnp.mean(jnp.square(y - mean), axis=-1, keepdims=True)
    y = (y - mean) * lax.rsqrt(var + LNX_EPS) * lnx_g.reshape(H, N) + lnx_b.reshape(H, N)
    y = y + bonuses[0] + bonuses[1]
    return (y.reshape(B, S, D).astype(x.dtype) * g) @ w_o


def swiglu(x, w_gu, w_down):
    h = jnp.einsum('bsd,df->bsf', x, w_gu)
    gate, up = h[..., :D_FF], h[..., D_FF:]
    return jnp.einsum('bsf,fd->bsd', jax.nn.silu(gate) * up, w_down)


def trunk(x, norm_g, mla_w_in, mla_g_q, mla_g_kv, mla_w_uq, mla_w_ukv, mla_w_o,
          rwkv_mu, rwkv_w_r, rwkv_w_k, rwkv_w_v, rwkv_w_o, rwkv_w0, rwkv_w1, rwkv_w2,
          rwkv_a0, rwkv_a1, rwkv_a2, rwkv_g1, rwkv_g2, rwkv_k_k, rwkv_k_a, rwkv_r_k,
          rwkv_lnx_g, rwkv_lnx_b, ffn_w_gu, ffn_w_down):
    for i in range(DEPTH):
        j = i // N_MIXERS
        h = rms_norm(x, norm_g[i, 0])
        if i % N_MIXERS == 0:
            h = mla_mixer(h, mla_w_in[j], mla_g_q[j], mla_g_kv[j], mla_w_uq[j],
                          mla_w_ukv[j], mla_w_o[j])
        else:
            h = rwkv7_mixer(h, rwkv_mu[j], rwkv_w_r[j], rwkv_w_k[j], rwkv_w_v[j], rwkv_w_o[j],
                            rwkv_w0[j], rwkv_w1[j], rwkv_w2[j], rwkv_a0[j], rwkv_a1[j],
                            rwkv_a2[j], rwkv_g1[j], rwkv_g2[j], rwkv_k_k[j], rwkv_k_a[j],
                            rwkv_r_k[j], rwkv_lnx_g[j], rwkv_lnx_b[j])
        x = x + rms_norm(h, norm_g[i, 1])
        h = swiglu(rms_norm(x, norm_g[i, 2]), ffn_w_gu[i], ffn_w_down[i])
        x = x + rms_norm(h, norm_g[i, 3])
    return x


def setup_inputs(seed: int = 0) -> dict:
    key = jax.random.key(seed)
    ks = jax.random.split(key, 32)

    def nrm(k, shape, scale):
        return jax.random.normal(k, shape, jnp.float32) * scale

    def gain(k, shape):
        return 1.0 + nrm(k, shape, 0.05)

    L, NM, NR = DEPTH, N_MLA_LAYERS, N_RWKV_LAYERS
    D = D_MODEL
    return {
        'x_prompt': nrm(ks[0], (BATCH, SEQ, D), 1.0),
        'x_sample': nrm(ks[1], (DEC_BATCH, DEC_SEQ, D), 1.0),
        'norm_g': gain(ks[2], (L, 4, D)),
        'mla_w_in': nrm(ks[3], (NM, D, MLA_IN), D ** -0.5),
        'mla_g_q': gain(ks[4], (NM, Q_LORA)),
        'mla_g_kv': gain(ks[5], (NM, KV_LORA)),
        'mla_w_uq': nrm(ks[6], (NM, Q_LORA, MLA_HEADS * (NOPE_DIM + ROPE_DIM)), Q_LORA ** -0.5),
        'mla_w_ukv': nrm(ks[7], (NM, KV_LORA, MLA_HEADS * (NOPE_DIM + V_DIM)), KV_LORA ** -0.5),
        'mla_w_o': nrm(ks[8], (NM, MLA_HEADS * V_DIM, D), (MLA_HEADS * V_DIM) ** -0.5),
        'rwkv_mu': jax.random.uniform(ks[9], (NR, N_LERP, D), jnp.float32),
        'rwkv_w_r': nrm(ks[10], (NR, D, D), D ** -0.5),
        'rwkv_w_k': nrm(ks[11], (NR, D, D), D ** -0.5),
        'rwkv_w_v': nrm(ks[12], (NR, D, D), D ** -0.5),
        'rwkv_w_o': nrm(ks[13], (NR, D, D), D ** -0.5),
        'rwkv_w0': -0.5 + nrm(ks[14], (NR, 2, D), 0.5),
        'rwkv_w1': nrm(ks[15], (NR, 2, D, DECAY_LORA), D ** -0.5),
        'rwkv_w2': nrm(ks[16], (NR, 2, DECAY_LORA, D), 0.1 * DECAY_LORA ** -0.5),
        'rwkv_a0': nrm(ks[17], (NR, 2, D), 0.5),
        'rwkv_a1': nrm(ks[18], (NR, 2, D, AAA_LORA), D ** -0.5),
        'rwkv_a2': nrm(ks[19], (NR, 2, AAA_LORA, D), 0.1 * AAA_LORA ** -0.5),
        'rwkv_g1': nrm(ks[20], (NR, D, GATE_LORA), D ** -0.5),
        'rwkv_g2': nrm(ks[21], (NR, GATE_LORA, D), GATE_LORA ** -0.5),
        'rwkv_k_k': 0.85 + nrm(ks[22], (NR, D), 0.05),
        'rwkv_k_a': gain(ks[23], (NR, D)),
        'rwkv_r_k': nrm(ks[24], (NR, RWKV_HEADS, RWKV_HEAD), 0.1),
        'rwkv_lnx_g': gain(ks[25], (NR, D)),
        'rwkv_lnx_b': nrm(ks[26], (NR, D), 0.01),
        'ffn_w_gu': nrm(ks[27], (L, D, 2 * D_FF), D ** -0.5),
        'ffn_w_down': nrm(ks[28], (L, D_FF, D), D_FF ** -0.5),
    }


def reference(x_prompt, x_sample, norm_g, mla_w_in, mla_g_q, mla_g_kv, mla_w_uq, mla_w_ukv,
              mla_w_o, rwkv_mu, rwkv_w_r, rwkv_w_k, rwkv_w_v, rwkv_w_o, rwkv_w0, rwkv_w1,
              rwkv_w2, rwkv_a0, rwkv_a1, rwkv_a2, rwkv_g1, rwkv_g2, rwkv_k_k, rwkv_k_a,
              rwkv_r_k, rwkv_lnx_g, rwkv_lnx_b, ffn_w_gu, ffn_w_down):
    params = (norm_g, mla_w_in, mla_g_q, mla_g_kv, mla_w_uq, mla_w_ukv, mla_w_o,
              rwkv_mu, rwkv_w_r, rwkv_w_k, rwkv_w_v, rwkv_w_o, rwkv_w0, rwkv_w1, rwkv_w2,
              rwkv_a0, rwkv_a1, rwkv_a2, rwkv_g1, rwkv_g2, rwkv_k_k, rwkv_k_a, rwkv_r_k,
              rwkv_lnx_g, rwkv_lnx_b, ffn_w_gu, ffn_w_down)
    y_prompt = trunk(x_prompt, *params)
    y_sample = trunk(x_sample, *params)
    return (y_prompt, y_sample)
```

```python
import functools
import math

import jax
import jax.numpy as jnp
from jax import lax
from jax.experimental import pallas as pl
from jax.experimental.pallas import tpu as pltpu

F32 = jnp.float32
BF16 = jnp.bfloat16

NORM_EPS = 1e-6
LNX_EPS = 64e-5
ROPE_THETA = 10000.0

MLA_HEADS = 16
Q_LORA = 512
KV_LORA = 512
NOPE_DIM = 128
ROPE_DIM = 64
V_DIM = 128
QK_PAD = 256
RWKV_HEAD = 64

LANES = 128
VMEM_LIMIT = 56 * 1024 * 1024

WKV_CHUNK = 64
WKV_GROUP = 4
WKV_BLOCK = 512

NT = (((1,), (1,)), ((), ()))
TN = (((0,), (0,)), ((), ()))


def _params(*sem):
    return pltpu.CompilerParams(dimension_semantics=sem, vmem_limit_bytes=VMEM_LIMIT)


def _mm(a, b):
    return jnp.dot(a.astype(BF16), b.astype(BF16), preferred_element_type=F32)


def _rms(x, g):
    return x * lax.rsqrt(jnp.mean(x * x, axis=-1, keepdims=True) + NORM_EPS) * g


def _sigmoid(z):
    return 1.0 / (1.0 + jnp.exp(-z))


def _rope_upper(up, cosw, sinw):
    lane = lax.broadcasted_iota(jnp.int32, up.shape, 1)
    swapped = jnp.where(lane < ROPE_DIM // 2, pltpu.roll(up, LANES - ROPE_DIM // 2, 1),
                        pltpu.roll(up, ROPE_DIM // 2, 1))
    return up * cosw + swapped * sinw


def _head_ones(n):
    r = lax.broadcasted_iota(jnp.int32, (n, n), 0) // RWKV_HEAD
    c = lax.broadcasted_iota(jnp.int32, (n, n), 1) // RWKV_HEAD
    return jnp.where(r == c, 1.0, 0.0).astype(BF16)


def _head_sum(z, ones_bd):
    hi = z.astype(BF16)
    lo = (z - hi.astype(F32)).astype(BF16)
    return (jnp.dot(hi, ones_bd, preferred_element_type=F32)
            + jnp.dot(lo, ones_bd, preferred_element_type=F32))


def _mla_in_kernel(x_ref, g_ref, w_ref, gq_ref, gkv_ref, cos_ref, sin_ref, cq_ref, ckv_ref, kpe_ref):
    xn = _rms(x_ref[...], g_ref[...]).astype(BF16)
    h = jnp.dot(xn, w_ref[...], preferred_element_type=F32)
    cq_ref[...] = _rms(h[:, :Q_LORA], gq_ref[...]).astype(BF16)
    ckv_ref[...] = _rms(h[:, Q_LORA:Q_LORA + KV_LORA], gkv_ref[...]).astype(BF16)
    kpe_ref[...] = _rope_upper(h[:, Q_LORA + KV_LORA:], cos_ref[...], sin_ref[...]).astype(BF16)


def _mla_in(x, g, w_in_pad, g_q, g_kv, cosw, sinw, seq):
    m, d = x.shape
    tm = min(512, seq)
    nseq = seq // tm
    n = w_in_pad.shape[1]
    row = lambda i: (i, 0)
    fix = lambda i: (0, 0)
    tab = lambda i: (i % nseq, 0)
    return pl.pallas_call(
        _mla_in_kernel,
        grid=(m // tm,),
        in_specs=[pl.BlockSpec((tm, d), row), pl.BlockSpec((1, d), fix), pl.BlockSpec((d, n), fix),
                  pl.BlockSpec((1, Q_LORA), fix), pl.BlockSpec((1, KV_LORA), fix),
                  pl.BlockSpec((tm, LANES), tab), pl.BlockSpec((tm, LANES), tab)],
        out_specs=[pl.BlockSpec((tm, Q_LORA), row), pl.BlockSpec((tm, KV_LORA), row),
                   pl.BlockSpec((tm, LANES), row)],
        out_shape=[jax.ShapeDtypeStruct((m, Q_LORA), BF16), jax.ShapeDtypeStruct((m, KV_LORA), BF16),
                   jax.ShapeDtypeStruct((m, LANES), BF16)],
        compiler_params=_params("parallel"),
        name="mla_in",
    )(x, g, w_in_pad, g_q, g_kv, cosw, sinw)


def _mla_q_kernel(cq_ref, w_ref, cos_ref, sin_ref, q_ref, *, scale):
    res = jnp.dot(cq_ref[...], w_ref[...], preferred_element_type=F32)
    cosw = cos_ref[...]
    sinw = sin_ref[...]
    for h in range(MLA_HEADS):
        lo = res[:, h * QK_PAD:h * QK_PAD + NOPE_DIM]
        up = _rope_upper(res[:, h * QK_PAD + NOPE_DIM:(h + 1) * QK_PAD], cosw, sinw)
        q_ref[:, h * QK_PAD:h * QK_PAD + NOPE_DIM] = (lo * scale).astype(BF16)
        q_ref[:, h * QK_PAD + NOPE_DIM:(h + 1) * QK_PAD] = (up * scale).astype(BF16)


def _mla_q(cq, w_uq_pad, cosw, sinw, seq):
    m, c = cq.shape
    n = w_uq_pad.shape[1]
    tm = min(256, seq)
    nseq = seq // tm
    scale = (NOPE_DIM + ROPE_DIM) ** -0.5
    return pl.pallas_call(
        functools.partial(_mla_q_kernel, scale=scale),
        grid=(m // tm,),
        in_specs=[pl.BlockSpec((tm, c), lambda i: (i, 0)), pl.BlockSpec((c, n), lambda i: (0, 0)),
                  pl.BlockSpec((tm, LANES), lambda i: (i % nseq, 0)),
                  pl.BlockSpec((tm, LANES), lambda i: (i % nseq, 0))],
        out_specs=pl.BlockSpec((tm, n), lambda i: (i, 0)),
        out_shape=jax.ShapeDtypeStruct((m, n), BF16),
        compiler_params=_params("parallel"),
        name="mla_q",
    )(cq, w_uq_pad, cosw, sinw)


def _matmul_kernel(a_ref, w_ref, o_ref):
    o_ref[...] = jnp.dot(a_ref[...], w_ref[...], preferred_element_type=F32).astype(o_ref.dtype)


def _matmul(a, w, out_dtype, name):
    m, k = a.shape
    n = w.shape[1]
    tm = min(512, m)
    tn = min(1024, n)
    return pl.pallas_call(
        _matmul_kernel,
        grid=(m // tm, n // tn),
        in_specs=[pl.BlockSpec((tm, k), lambda i, j: (i, 0)), pl.BlockSpec((k, tn), lambda i, j: (0, j))],
        out_specs=pl.BlockSpec((tm, tn), lambda i, j: (i, j)),
        out_shape=jax.ShapeDtypeStruct((m, n), out_dtype),
        compiler_params=_params("parallel", "parallel"),
        name=name,
    )(a, w)


def _attn_kernel(q_ref, kn_ref, kpe_ref, v_ref, o_ref, m_sc, l_sc, acc_sc):
    j = pl.program_id(3)

    @pl.when(j == 0)
    def _():
        m_sc[...] = jnp.full(m_sc.shape, -jnp.inf, F32)
        l_sc[...] = jnp.zeros(l_sc.shape, F32)
        acc_sc[...] = jnp.zeros(acc_sc.shape, F32)

    k = jnp.concatenate([kn_ref[...], kpe_ref[...]], axis=1)
    s = lax.dot_general(q_ref[...], k, NT, preferred_element_type=F32)
    m_prev = m_sc[...]
    m_new = jnp.maximum(m_prev, jnp.max(s, axis=1, keepdims=True))
    alpha = jnp.exp(m_prev - m_new)
    p = jnp.exp(s - m_new)
    l_sc[...] = alpha * l_sc[...] + jnp.sum(p, axis=1, keepdims=True)
    acc_sc[...] = alpha * acc_sc[...] + jnp.dot(p.astype(BF16), v_ref[...], preferred_element_type=F32)
    m_sc[...] = m_new

    @pl.when(j == pl.num_programs(3) - 1)
    def _():
        o_ref[...] = (acc_sc[...] / l_sc[...]).astype(o_ref.dtype)


def _attention(q, kv, kpe, batch, seq):
    m = q.shape[0]
    tq = min(512, seq)
    tk = min(1024, seq)
    nq = seq // tq
    nk = seq // tk
    return pl.pallas_call(
        _attn_kernel,
        grid=(batch, MLA_HEADS, nq, nk),
        in_specs=[pl.BlockSpec((tq, QK_PAD), lambda b, h, i, j: (b * nq + i, h)),
                  pl.BlockSpec((tk, NOPE_DIM), lambda b, h, i, j: (b * nk + j, 2 * h)),
                  pl.BlockSpec((tk, LANES), lambda b, h, i, j: (b * nk + j, 0)),
                  pl.BlockSpec((tk, V_DIM), lambda b, h, i, j: (b * nk + j, 2 * h + 1))],
        out_specs=pl.BlockSpec((tq, V_DIM), lambda b, h, i, j: (b * nq + i, h)),
        out_shape=jax.ShapeDtypeStruct((m, MLA_HEADS * V_DIM), BF16),
        scratch_shapes=[pltpu.VMEM((tq, 1), F32), pltpu.VMEM((tq, 1), F32), pltpu.VMEM((tq, V_DIM), F32)],
        compiler_params=_params("parallel", "parallel", "parallel", "arbitrary"),
        name="mla_attention",
    )(q, kv, kpe, kv)


def _proj_res_kernel(a_ref, w_ref, x_ref, g_ref, o_ref, acc_sc):
    k = pl.program_id(1)

    @pl.when(k == 0)
    def _():
        acc_sc[...] = jnp.zeros(acc_sc.shape, F32)

    acc_sc[...] += jnp.dot(a_ref[...], w_ref[...], preferred_element_type=F32)

    @pl.when(k == pl.num_programs(1) - 1)
    def _():
        o_ref[...] = x_ref[...] + _rms(acc_sc[...], g_ref[...])


def _proj_res(a, w, x, g, name):
    m, kdim = a.shape
    n = w.shape[1]
    tm = min(512, m)
    tk = min(512, kdim)
    return pl.pallas_call(
        _proj_res_kernel,
        grid=(m // tm, kdim // tk),
        in_specs=[pl.BlockSpec((tm, tk), lambda i, k: (i, k)), pl.BlockSpec((tk, n), lambda i, k: (k, 0)),
                  pl.BlockSpec((tm, n), lambda i, k: (i, 0)), pl.BlockSpec((1, n), lambda i, k: (0, 0))],
        out_specs=pl.BlockSpec((tm, n), lambda i, k: (i, 0)),
        out_shape=jax.ShapeDtypeStruct((m, n), F32),
        scratch_shapes=[pltpu.VMEM((tm, n), F32)],
        compiler_params=_params("parallel", "arbitrary"),
        name=name,
    )(a, w, x, g)


def _ffn_kernel(x_ref, g_pre_ref, wg_ref, wu_ref, wd_ref, g_post_ref, o_ref, xn_sc, acc_sc):
    f = pl.program_id(1)

    @pl.when(f == 0)
    def _():
        xn_sc[...] = _rms(x_ref[...], g_pre_ref[...]).astype(BF16)
        acc_sc[...] = jnp.zeros(acc_sc.shape, F32)

    xn = xn_sc[...]
    gate = jnp.dot(xn, wg_ref[...], preferred_element_type=F32)
    up = jnp.dot(xn, wu_ref[...], preferred_element_type=F32)
    act = (gate * _sigmoid(gate) * up).astype(BF16)
    acc_sc[...] += jnp.dot(act, wd_ref[...], preferred_element_type=F32)

    @pl.when(f == pl.num_programs(1) - 1)
    def _():
        o_ref[...] = x_ref[...] + _rms(acc_sc[...], g_post_ref[...])


def _ffn(x, g_pre, w_gu, w_down, g_post):
    m, d = x.shape
    d_ff = w_down.shape[0]
    tm = min(512, m)
    tf = 512
    nf = d_ff // tf
    return pl.pallas_call(
        _ffn_kernel,
        grid=(m // tm, nf),
        in_specs=[pl.BlockSpec((tm, d), lambda i, f: (i, 0)), pl.BlockSpec((1, d), lambda i, f: (0, 0)),
                  pl.BlockSpec((d, tf), lambda i, f: (0, f)), pl.BlockSpec((d, tf), lambda i, f: (0, f + nf)),
                  pl.BlockSpec((tf, d), lambda i, f: (f, 0)), pl.BlockSpec((1, d), lambda i, f: (0, 0))],
        out_specs=pl.BlockSpec((tm, d), lambda i, f: (i, 0)),
        out_shape=jax.ShapeDtypeStruct((m, d), F32),
        scratch_shapes=[pltpu.VMEM((tm, d), BF16), pltpu.VMEM((tm, d), F32)],
        compiler_params=_params("parallel", "arbitrary"),
        name="ffn",
    )(x, g_pre, w_gu, w_gu, w_down, g_post)


def _rwkv_mix_kernel(x_ref, xp_ref, xn_ref, g_ref, mu_ref, xr_ref, xw_ref, xk_ref, xv_ref, xa_ref, xg_ref,
                     *, tiles_per_seq):
    i = pl.program_id(0)
    g = g_ref[...]
    h = _rms(x_ref[...], g)
    tm = h.shape[0]
    first = (i % tiles_per_seq) == 0
    last = (i % tiles_per_seq) == tiles_per_seq - 1
    hp = jnp.where(first, 0.0, _rms(xp_ref[7:8, :], g))
    hn = jnp.where(last, 0.0, _rms(xn_ref[0:1, :], g))
    row = lax.broadcasted_iota(jnp.int32, h.shape, 0)
    h_prev = jnp.where(row == 0, hp, pltpu.roll(h, 1, 0))
    h_next = jnp.where(row == tm - 1, hn, pltpu.roll(h, tm - 1, 0))
    xx = 0.5 * (h_prev + h_next) - h
    for idx, ref in enumerate((xr_ref, xw_ref, xk_ref, xv_ref, xa_ref, xg_ref)):
        ref[...] = (h + xx * mu_ref[idx:idx + 1, :]).astype(BF16)


def _rwkv_mix(x, g, mu, seq):
    m, d = x.shape
    tm = min(256, seq)
    tps = seq // tm
    nb8 = m // 8
    r8 = tm // 8
    out = jax.ShapeDtypeStruct((m, d), BF16)
    row = lambda i: (i, 0)
    return pl.pallas_call(
        functools.partial(_rwkv_mix_kernel, tiles_per_seq=tps),
        grid=(m // tm,),
        in_specs=[pl.BlockSpec((tm, d), row),
                  pl.BlockSpec((8, d), lambda i: (jnp.maximum(i * r8 - 1, 0), 0)),
                  pl.BlockSpec((8, d), lambda i: (jnp.minimum((i + 1) * r8, nb8 - 1), 0)),
                  pl.BlockSpec((1, d), lambda i: (0, 0)), pl.BlockSpec((8, d), lambda i: (0, 0))],
        out_specs=[pl.BlockSpec((tm, d), row)] * 6,
        out_shape=[out] * 6,
        compiler_params=_params("parallel"),
        name="rwkv_mix",
    )(x, x, x, g, mu)


def _lora_kernel(x_ref, a_ref, b_ref, bias_ref, o_ref, *, mid, post):
    t = jnp.dot(x_ref[...], a_ref[...], preferred_element_type=F32)
    if mid == "tanh":
        t = jnp.tanh(t)
    elif mid == "sigmoid":
        t = _sigmoid(t)
    z = jnp.dot(t.astype(BF16), b_ref[...], preferred_element_type=F32) + bias_ref[...]
    if post == "sigmoid":
        z = _sigmoid(z)
    elif post == "logdecay":
        z = -math.exp(-0.5) * _sigmoid(z)
    o_ref[...] = z.astype(o_ref.dtype)


def _lora(x, a, b, bias, mid, post, out_dtype, name):
    m, d = x.shape
    r = a.shape[1]
    n = b.shape[1]
    tm = min(512, m)
    return pl.pallas_call(
        functools.partial(_lora_kernel, mid=mid, post=post),
        grid=(m // tm,),
        in_specs=[pl.BlockSpec((tm, d), lambda i: (i, 0)), pl.BlockSpec((d, r), lambda i: (0, 0)),
                  pl.BlockSpec((r, n), lambda i: (0, 0)), pl.BlockSpec((1, n), lambda i: (0, 0))],
        out_specs=pl.BlockSpec((tm, n), lambda i: (i, 0)),
        out_shape=jax.ShapeDtypeStruct((m, n), out_dtype),
        compiler_params=_params("parallel"),
        name=name,
    )(x, a, b, bias)


def _wkv_prep_kernel(r_ref, k_ref, v_ref, a0_ref, a1_ref, kk_par, ka_par, rk_par,
                     kk_ref, b0_ref, b1_ref, kd0_ref, kd1_ref, bonus_ref):
    ones_bd = _head_ones(LANES)
    d = r_ref.shape[1]
    for c in range(d // LANES):
        sl = slice(c * LANES, (c + 1) * LANES)
        k = k_ref[:, sl]
        a0 = a0_ref[:, sl]
        a1 = a1_ref[:, sl]
        k_a = ka_par[:, sl]
        kkr = k * kk_par[:, sl]
        nrm = jnp.maximum(jnp.sqrt(_head_sum(kkr * kkr, ones_bd)), 1e-12)
        kk = kkr / nrm
        kd0 = k * (1.0 + (a0 - 1.0) * k_a)
        kd1 = k * (1.0 + (a1 - 1.0) * k_a)
        kk_ref[:, sl] = kk
        b0_ref[:, sl] = kk * a0
        b1_ref[:, sl] = kk * a1
        kd0_ref[:, sl] = kd0
        kd1_ref[:, sl] = kd1
        bonus_ref[:, sl] = _head_sum(r_ref[:, sl] * (kd0 + kd1) * rk_par[:, sl], ones_bd) * v_ref[:, sl]


def _wkv_prep(r, k, v, a0, a1, k_k, k_a, r_k):
    m, d = r.shape
    tm = min(256, m)
    row = pl.BlockSpec((tm, d), lambda i: (i, 0))
    par = pl.BlockSpec((1, d), lambda i: (0, 0))
    out = jax.ShapeDtypeStruct((m, d), F32)
    return pl.pallas_call(
        _wkv_prep_kernel,
        grid=(m // tm,),
        in_specs=[row] * 5 + [par] * 3,
        out_specs=[row] * 6,
        out_shape=[out] * 6,
        compiler_params=_params("parallel"),
        name="wkv_prep",
    )(r, k, v, a0, a1, k_k, k_a, r_k)


def _wkv_kernel(r_ref, kd_ref, v_ref, kk_ref, b_ref, lw_ref, y_ref, st_ref, *, reverse):
    t_len = WKV_CHUNK
    gl = st_ref.shape[0]
    ng = gl // RWKV_HEAD
    gt = ng * t_len
    n_chunks = r_ref.shape[0] // t_len

    @pl.when(pl.program_id(2) == 0)
    def _():
        st_ref[...] = jnp.zeros(st_ref.shape, F32)

    row = lax.broadcasted_iota(jnp.int32, (gt, gt), 0)
    col = lax.broadcasted_iota(jnp.int32, (gt, gt), 1)
    same = (row // t_len) == (col // t_len)
    if reverse:
        strict = same & (col > row)
        incl = same & (col >= row)
    else:
        strict = same & (col < row)
        incl = same & (col <= row)
    eye = jnp.where(row == col, 1.0, 0.0)
    trow = lax.broadcasted_iota(jnp.int32, (t_len, t_len), 0)
    tcol = lax.broadcasted_iota(jnp.int32, (t_len, t_len), 1)
    tri = jnp.where((tcol >= trow) if reverse else (tcol <= trow), 1.0, 0.0).astype(BF16)
    blk_mask = (lax.broadcasted_iota(jnp.int32, (gt, gl), 0) // t_len
                == lax.broadcasted_iota(jnp.int32, (gt, gl), 1) // RWKV_HEAD)
    head_mask = (lax.broadcasted_iota(jnp.int32, (gl, gl), 0) // RWKV_HEAD
                 == lax.broadcasted_iota(jnp.int32, (gl, gl), 1) // RWKV_HEAD)

    def tile(x):
        return jnp.concatenate([x] * ng, axis=0)

    def blk(x):
        return jnp.where(blk_mask, tile(x), 0.0)

    def unblk(x):
        out = x[0:t_len]
        for h in range(1, ng):
            out = out + x[h * t_len:(h + 1) * t_len]
        return out

    def chunk(ci, carry):
        c = (n_chunks - 1 - ci) if reverse else ci
        rows = pl.ds(pl.multiple_of(c * t_len, t_len), t_len)
        r = r_ref[rows, :]
        kd = kd_ref[rows, :]
        v = v_ref[rows, :]
        kk = kk_ref[rows, :]
        b = b_ref[rows, :]
        lw = lw_ref[rows, :]

        lw_hi = lw.astype(BF16)
        lw_lo = (lw - lw_hi.astype(F32)).astype(BF16)
        cum = (jnp.dot(tri, lw_hi, preferred_element_type=F32)
               + jnp.dot(tri, lw_lo, preferred_element_type=F32))
        tot = cum[0:1] if reverse else cum[t_len - 1:t_len]
        inv_p = jnp.exp(-cum)
        rt = r * jnp.exp(cum)
        kkt = kk * jnp.exp(cum - lw)
        bh = b * inv_p
        kh = kd * inv_p
        to_end = jnp.exp(tot - cum)
        b_end = b * to_end
        k_end = kd * to_end

        lhs = jnp.concatenate([blk(kkt), blk(rt)], axis=0).astype(BF16)
        rhs = jnp.concatenate([tile(bh), tile(kh)], axis=0).astype(BF16)
        aa = lax.dot_general(lhs, rhs, NT, preferred_element_type=F32)
        a_ab = jnp.where(strict, aa[:gt, :gt], 0.0)
        a_ak = jnp.where(strict, aa[:gt, gt:], 0.0)
        a_rb = jnp.where(incl, aa[gt:, :gt], 0.0)
        a_rk = jnp.where(incl, aa[gt:, gt:], 0.0)

        x = eye - a_ab
        p = _mm(a_ab, a_ab)
        rounds = int(math.log2(t_len)) - 1
        for it in range(rounds):
            x = x + _mm(x, p)
            if it < rounds - 1:
                p = _mm(p, p)

        st = st_ref[...]
        st_b = st.astype(BF16)
        ks = lax.dot_general(kkt.astype(BF16), st_b, NT, preferred_element_type=F32)
        rs = lax.dot_general(rt.astype(BF16), st_b, NT, preferred_element_type=F32)
        v_blk = blk(v)
        sa_blk = -_mm(x, blk(ks) + _mm(a_ak, v_blk))
        y_blk = _mm(a_rb, sa_blk) + _mm(a_rk, v_blk)
        y_ref[rows, :] = rs + unblk(y_blk)
        sa = unblk(sa_blk)
        upd = lax.dot_general(jnp.concatenate([sa, v], axis=0).astype(BF16),
                              jnp.concatenate([b_end, k_end], axis=0).astype(BF16),
                              TN, preferred_element_type=F32)
        st_ref[...] = st * jnp.exp(tot) + jnp.where(head_mask, upd, 0.0)
        return carry

    lax.fori_loop(0, n_chunks, chunk, 0)


def _wkv(r, kd, v, kk, b, lw, batch, seq, reverse):
    m, d = r.shape
    gl = WKV_GROUP * RWKV_HEAD
    tb = min(WKV_BLOCK, seq)
    nb = seq // tb
    if reverse:
        idx = lambda bi, g, j: (bi * nb + nb - 1 - j, g)
    else:
        idx = lambda bi, g, j: (bi * nb + j, g)
    spec = pl.BlockSpec((tb, gl), idx)
    return pl.pallas_call(
        functools.partial(_wkv_kernel, reverse=reverse),
        grid=(batch, d // gl, nb),
        in_specs=[spec] * 6,
        out_specs=spec,
        out_shape=jax.ShapeDtypeStruct((m, d), F32),
        scratch_shapes=[pltpu.VMEM((gl, gl), F32)],
        compiler_params=_params("parallel", "parallel", "arbitrary"),
        name="wkv_bwd" if reverse else "wkv_fwd",
    )(r, kd, v, kk, b, lw)


def _wkv_post_kernel(y0_ref, y1_ref, bonus_ref, g_ref, lg_ref, lb_ref, o_ref):
    ones_bd = _head_ones(LANES)
    d = y0_ref.shape[1]
    inv_n = 1.0 / RWKV_HEAD
    for c in range(d // LANES):
        sl = slice(c * LANES, (c + 1) * LANES)
        y = y0_ref[:, sl] + y1_ref[:, sl]
        mean = _head_sum(y, ones_bd) * inv_n
        yc = y - mean
        var = _head_sum(yc * yc, ones_bd) * inv_n
        yn = yc * lax.rsqrt(var + LNX_EPS) * lg_ref[:, sl] + lb_ref[:, sl]
        o_ref[:, sl] = ((yn + bonus_ref[:, sl]) * g_ref[:, sl]).astype(o_ref.dtype)


def _wkv_post(y0, y1, bonus, g, lnx_g, lnx_b):
    m, d = y0.shape
    tm = min(256, m)
    row = pl.BlockSpec((tm, d), lambda i: (i, 0))
    par = pl.BlockSpec((1, d), lambda i: (0, 0))
    return pl.pallas_call(
        _wkv_post_kernel,
        grid=(m // tm,),
        in_specs=[row] * 4 + [par] * 2,
        out_specs=row,
        out_shape=jax.ShapeDtypeStruct((m, d), BF16),
        compiler_params=_params("parallel"),
        name="wkv_post",
    )(y0, y1, bonus, g, lnx_g, lnx_b)


def _rope_tables(seq):
    half = ROPE_DIM // 2
    inv = 1.0 / (ROPE_THETA ** (jnp.arange(half, dtype=F32) * (2.0 / ROPE_DIM)))
    ang = jnp.arange(seq, dtype=F32)[:, None] * inv[None, :]
    cos, sin = jnp.cos(ang), jnp.sin(ang)
    zero = jnp.zeros((seq, LANES - ROPE_DIM), F32)
    return (jnp.concatenate([cos, cos, zero], axis=1), jnp.concatenate([-sin, sin, zero], axis=1))


def _row(v):
    return v.reshape(1, -1).astype(F32)


def _pad_cols(w, n):
    return jnp.pad(w, ((0, 0), (0, n - w.shape[1])))


def _pad_rows(w, n):
    return jnp.pad(w, ((0, n - w.shape[0]), (0, 0)))


def _mla_layer(x, batch, seq, g_pre, g_post, w_in, g_q, g_kv, w_uq, w_ukv, w_o):
    cosw, sinw = _rope_tables(seq)
    w_in_pad = _pad_cols(w_in, Q_LORA + KV_LORA + LANES).astype(BF16)
    w_uq_pad = jnp.pad(w_uq.reshape(Q_LORA, MLA_HEADS, NOPE_DIM + ROPE_DIM),
                       ((0, 0), (0, 0), (0, QK_PAD - NOPE_DIM - ROPE_DIM))).reshape(Q_LORA, MLA_HEADS * QK_PAD)
    cq, ckv, kpe = _mla_in(x, _row(g_pre), w_in_pad, _row(g_q), _row(g_kv), cosw, sinw, seq)
    q = _mla_q(cq, w_uq_pad.astype(BF16), cosw, sinw, seq)
    kv = _matmul(ckv, w_ukv.astype(BF16), BF16, "mla_kv")
    o = _attention(q, kv, kpe, batch, seq)
    return _proj_res(o, w_o.astype(BF16), x, _row(g_post), "mla_out")


def _rwkv_layer(x, batch, seq, g_pre, g_post, mu, w_r, w_k, w_v, w_o, w0, w1, w2, a0, a1, a2, g1, g2,
                k_k, k_a, r_k, lnx_g, lnx_b):
    d = x.shape[1]
    mu8 = jnp.pad(mu, ((0, 8 - mu.shape[0]), (0, 0)))
    xr, xw, xk, xv, xa, xg = _rwkv_mix(x, _row(g_pre), mu8, seq)
    r = _matmul(xr, w_r.astype(BF16), F32, "rwkv_r")
    k = _matmul(xk, w_k.astype(BF16), F32, "rwkv_k")
    v = _matmul(xv, w_v.astype(BF16), F32, "rwkv_v")
    zero = jnp.zeros((1, d), F32)
    g = _lora(xg, g1.astype(BF16), g2.astype(BF16), zero, "sigmoid", "none", BF16, "rwkv_gate")
    lws, avs = [], []
    for di in range(2):
        w1p = _pad_cols(w1[di], LANES).astype(BF16)
        w2p = _pad_rows(w2[di], LANES).astype(BF16)
        a1p = _pad_cols(a1[di], LANES).astype(BF16)
        a2p = _pad_rows(a2[di], LANES).astype(BF16)
        lws.append(_lora(xw, w1p, w2p, _row(w0[di]), "tanh", "logdecay", F32, "rwkv_decay%d" % di))
        avs.append(_lora(xa, a1p, a2p, _row(a0[di]), "none", "sigmoid", F32, "rwkv_a%d" % di))
    kk, b0, b1, kd0, kd1, bonus = _wkv_prep(r, k, v, avs[0], avs[1], _row(k_k), _row(k_a), _row(r_k))
    y0 = _wkv(r, kd0, v, kk, b0, lws[0], batch, seq, False)
    y1 = _wkv(r, kd1, v, kk, b1, lws[1], batch, seq, True)
    yg = _wkv_post(y0, y1, bonus, g, _row(lnx_g), _row(lnx_b))
    return _proj_res(yg, w_o.astype(BF16), x, _row(g_post), "rwkv_out")


def _trunk(x3, norm_g, mla_w_in, mla_g_q, mla_g_kv, mla_w_uq, mla_w_ukv, mla_w_o,
           rwkv_mu, rwkv_w_r, rwkv_w_k, rwkv_w_v, rwkv_w_o, rwkv_w0, rwkv_w1, rwkv_w2,
           rwkv_a0, rwkv_a1, rwkv_a2, rwkv_g1, rwkv_g2, rwkv_k_k, rwkv_k_a, rwkv_r_k,
           rwkv_lnx_g, rwkv_lnx_b, ffn_w_gu, ffn_w_down):
    batch, seq, d = x3.shape
    x = x3.reshape(batch * seq, d)
    depth = norm_g.shape[0]
    for i in range(depth):
        j = i // 2
        if i % 2 == 0:
            x = _mla_layer(x, batch, seq, norm_g[i, 0], norm_g[i, 1], mla_w_in[j], mla_g_q[j], mla_g_kv[j],
                           mla_w_uq[j], mla_w_ukv[j], mla_w_o[j])
        else:
            x = _rwkv_layer(x, batch, seq, norm_g[i, 0], norm_g[i, 1], rwkv_mu[j], rwkv_w_r[j], rwkv_w_k[j],
                            rwkv_w_v[j], rwkv_w_o[j], rwkv_w0[j], rwkv_w1[j], rwkv_w2[j], rwkv_a0[j],
                            rwkv_a1[j], rwkv_a2[j], rwkv_g1[j], rwkv_g2[j], rwkv_k_k[j], rwkv_k_a[j],
                            rwkv_r_k[j].reshape(-1), rwkv_lnx_g[j], rwkv_lnx_b[j])
        x = _ffn(x, _row(norm_g[i, 2]), ffn_w_gu[i].astype(BF16), ffn_w_down[i].astype(BF16), _row(norm_g[i, 3]))
    return x.reshape(batch, seq, d)


def kernel(x_prompt, x_sample, norm_g, mla_w_in, mla_g_q, mla_g_kv, mla_w_uq, mla_w_ukv, mla_w_o, rwkv_mu, rwkv_w_r, rwkv_w_k, rwkv_w_v, rwkv_w_o, rwkv_w0, rwkv_w1, rwkv_w2, rwkv_a0, rwkv_a1, rwkv_a2, rwkv_g1, rwkv_g2, rwkv_k_k, rwkv_k_a, rwkv_r_k, rwkv_lnx_g, rwkv_lnx_b, ffn_w_gu, ffn_w_down):
    params = (norm_g, mla_w_in, mla_g_q, mla_g_kv, mla_w_uq, mla_w_ukv, mla_w_o,
              rwkv_mu, rwkv_w_r, rwkv_w_k, rwkv_w_v, rwkv_w_o, rwkv_w0, rwkv_w1, rwkv_w2,
              rwkv_a0, rwkv_a1, rwkv_a2, rwkv_g1, rwkv_g2, rwkv_k_k, rwkv_k_a, rwkv_r_k,
              rwkv_lnx_g, rwkv_lnx_b, ffn_w_gu, ffn_w_down)
    return (_trunk(x_prompt, *params), _trunk(x_sample, *params))
```

```python
import functools
import math

import jax
import jax.numpy as jnp
from jax import lax
from jax.experimental import pallas as pl
from jax.experimental.pallas import tpu as pltpu

F32 = jnp.float32
BF16 = jnp.bfloat16

NORM_EPS = 1e-6
LNX_EPS = 64e-5
ROPE_THETA = 10000.0

MLA_HEADS = 16
Q_LORA = 512
KV_LORA = 512
NOPE_DIM = 128
ROPE_DIM = 64
V_DIM = 128
QK_PAD = 256
RWKV_HEAD = 64

LANES = 128
VMEM_LIMIT = 56 * 1024 * 1024

WKV_CHUNK = 64
WKV_GROUP = 4
WKV_BLOCK = 512

NT = (((1,), (1,)), ((), ()))
TN = (((0,), (0,)), ((), ()))


def _params(*sem):
    return pltpu.CompilerParams(dimension_semantics=sem, vmem_limit_bytes=VMEM_LIMIT)


def _mm(a, b):
    return jnp.dot(a.astype(BF16), b.astype(BF16), preferred_element_type=F32)


def _rms(x, g):
    return x * lax.rsqrt(jnp.mean(x * x, axis=-1, keepdims=True) + NORM_EPS) * g


def _sigmoid(z):
    return 1.0 / (1.0 + jnp.exp(-z))


def _rope_upper(up, cosw, sinw):
    lane = lax.broadcasted_iota(jnp.int32, up.shape, 1)
    swapped = jnp.where(lane < ROPE_DIM // 2, pltpu.roll(up, LANES - ROPE_DIM // 2, 1),
                        pltpu.roll(up, ROPE_DIM // 2, 1))
    return up * cosw + swapped * sinw


def _head_ones(n):
    r = lax.broadcasted_iota(jnp.int32, (n, n), 0) // RWKV_HEAD
    c = lax.broadcasted_iota(jnp.int32, (n, n), 1) // RWKV_HEAD
    return jnp.where(r == c, 1.0, 0.0).astype(BF16)


def _head_sum(z, ones_bd):
    hi = z.astype(BF16)
    lo = (z - hi.astype(F32)).astype(BF16)
    return (jnp.dot(hi, ones_bd, preferred_element_type=F32)
            + jnp.dot(lo, ones_bd, preferred_element_type=F32))


def _mla_in_kernel(x_ref, g_ref, w_ref, gq_ref, gkv_ref, cos_ref, sin_ref, cq_ref, ckv_ref, kpe_ref):
    xn = _rms(x_ref[...], g_ref[...]).astype(BF16)
    h = jnp.dot(xn, w_ref[...], preferred_element_type=F32)
    cq_ref[...] = _rms(h[:, :Q_LORA], gq_ref[...]).astype(BF16)
    ckv_ref[...] = _rms(h[:, Q_LORA:Q_LORA + KV_LORA], gkv_ref[...]).astype(BF16)
    kpe_ref[...] = _rope_upper(h[:, Q_LORA + KV_LORA:], cos_ref[...], sin_ref[...]).astype(BF16)


def _mla_in(x, g, w_in_pad, g_q, g_kv, cosw, sinw, seq):
    m, d = x.shape
    tm = min(512, seq)
    nseq = seq // tm
    n = w_in_pad.shape[1]
    row = lambda i: (i, 0)
    fix = lambda i: (0, 0)
    tab = lambda i: (i % nseq, 0)
    return pl.pallas_call(
        _mla_in_kernel,
        grid=(m // tm,),
        in_specs=[pl.BlockSpec((tm, d), row), pl.BlockSpec((1, d), fix), pl.BlockSpec((d, n), fix),
                  pl.BlockSpec((1, Q_LORA), fix), pl.BlockSpec((1, KV_LORA), fix),
                  pl.BlockSpec((tm, LANES), tab), pl.BlockSpec((tm, LANES), tab)],
        out_specs=[pl.BlockSpec((tm, Q_LORA), row), pl.BlockSpec((tm, KV_LORA), row),
                   pl.BlockSpec((tm, LANES), row)],
        out_shape=[jax.ShapeDtypeStruct((m, Q_LORA), BF16), jax.ShapeDtypeStruct((m, KV_LORA), BF16),
                   jax.ShapeDtypeStruct((m, LANES), BF16)],
        compiler_params=_params("parallel"),
        name="mla_in",
    )(x, g, w_in_pad, g_q, g_kv, cosw, sinw)


def _mla_q_kernel(cq_ref, w_ref, cos_ref, sin_ref, q_ref, *, scale):
    res = jnp.dot(cq_ref[...], w_ref[...], preferred_element_type=F32)
    cosw = cos_ref[...]
    sinw = sin_ref[...]
    for h in range(MLA_HEADS):
        lo = res[:, h * QK_PAD:h * QK_PAD + NOPE_DIM]
        up = _rope_upper(res[:, h * QK_PAD + NOPE_DIM:(h + 1) * QK_PAD], cosw, sinw)
        q_ref[h * QK_PAD:h * QK_PAD + NOPE_DIM, :] = (lo * scale).T.astype(BF16)
        q_ref[h * QK_PAD + NOPE_DIM:(h + 1) * QK_PAD, :] = (up * scale).T.astype(BF16)


def _mla_q(cq, w_uq_pad, cosw, sinw, seq):
    m, c = cq.shape
    n = w_uq_pad.shape[1]
    tm = min(256, seq)
    nseq = seq // tm
    scale = (NOPE_DIM + ROPE_DIM) ** -0.5 * math.log2(math.e)
    return pl.pallas_call(
        functools.partial(_mla_q_kernel, scale=scale),
        grid=(m // tm,),
        in_specs=[pl.BlockSpec((tm, c), lambda i: (i, 0)), pl.BlockSpec((c, n), lambda i: (0, 0)),
                  pl.BlockSpec((tm, LANES), lambda i: (i % nseq, 0)),
                  pl.BlockSpec((tm, LANES), lambda i: (i % nseq, 0))],
        out_specs=pl.BlockSpec((n, tm), lambda i: (0, i)),
        out_shape=jax.ShapeDtypeStruct((n, m), BF16),
        compiler_params=_params("parallel"),
        name="mla_q",
    )(cq, w_uq_pad, cosw, sinw)


def _mla_kv_kernel(ckv_ref, w_ref, kn_ref, vt_ref):
    res = jnp.dot(ckv_ref[...], w_ref[...], preferred_element_type=F32)
    for h in range(MLA_HEADS):
        kn_ref[:, h * NOPE_DIM:(h + 1) * NOPE_DIM] = res[:, 2 * h * NOPE_DIM:(2 * h + 1) * NOPE_DIM].astype(BF16)
        vt_ref[h * V_DIM:(h + 1) * V_DIM, :] = res[:, (2 * h + 1) * V_DIM:(2 * h + 2) * V_DIM].T.astype(BF16)


def _mla_kv(ckv, w_ukv):
    m, c = ckv.shape
    n = w_ukv.shape[1]
    tm = min(256, m)
    return pl.pallas_call(
        _mla_kv_kernel,
        grid=(m // tm,),
        in_specs=[pl.BlockSpec((tm, c), lambda i: (i, 0)), pl.BlockSpec((c, n), lambda i: (0, 0))],
        out_specs=[pl.BlockSpec((tm, n // 2), lambda i: (i, 0)), pl.BlockSpec((n // 2, tm), lambda i: (0, i))],
        out_shape=[jax.ShapeDtypeStruct((m, n // 2), BF16), jax.ShapeDtypeStruct((n // 2, m), BF16)],
        compiler_params=_params("parallel"),
        name="mla_kv",
    )(ckv, w_ukv)


def _matmul_kernel(a_ref, w_ref, o_ref):
    o_ref[...] = jnp.dot(a_ref[...], w_ref[...], preferred_element_type=F32).astype(o_ref.dtype)


def _matmul(a, w, out_dtype, name):
    m, k = a.shape
    n = w.shape[1]
    tm = min(512, m)
    tn = min(1024, n)
    return pl.pallas_call(
        _matmul_kernel,
        grid=(m // tm, n // tn),
        in_specs=[pl.BlockSpec((tm, k), lambda i, j: (i, 0)), pl.BlockSpec((k, tn), lambda i, j: (0, j))],
        out_specs=pl.BlockSpec((tm, tn), lambda i, j: (i, j)),
        out_shape=jax.ShapeDtypeStruct((m, n), out_dtype),
        compiler_params=_params("parallel", "parallel"),
        name=name,
    )(a, w)


def _attn_kernel(qt_ref, kn_ref, kpe_ref, vt_ref, o_ref, m_sc, l_sc, acc_sc, *, n_split):
    j = pl.program_id(3)

    @pl.when(j == 0)
    def _():
        m_sc[...] = jnp.full(m_sc.shape, -jnp.inf, F32)
        l_sc[...] = jnp.zeros(l_sc.shape, F32)
        acc_sc[...] = jnp.zeros(acc_sc.shape, F32)

    k = jnp.concatenate([kn_ref[...], kpe_ref[...]], axis=1)
    vt = vt_ref[...]
    width = qt_ref.shape[1] // n_split
    scores = [jnp.dot(k, qt_ref[:, c * width:(c + 1) * width], preferred_element_type=F32)
              for c in range(n_split)]
    for c in range(n_split):
        cols = slice(c * width, (c + 1) * width)
        st = scores[c]
        m_prev = m_sc[:, cols]
        m_new = jnp.maximum(m_prev, jnp.max(st, axis=0, keepdims=True))
        alpha = jnp.exp2(m_prev - m_new)
        pt = jnp.exp2(st - m_new)
        l_sc[:, cols] = alpha * l_sc[:, cols] + jnp.sum(pt, axis=0, keepdims=True)
        acc_sc[:, cols] = alpha * acc_sc[:, cols] + jnp.dot(vt, pt.astype(BF16), preferred_element_type=F32)
        m_sc[:, cols] = m_new

    @pl.when(j == pl.num_programs(3) - 1)
    def _():
        o_ref[...] = (acc_sc[...] / l_sc[...]).T.astype(o_ref.dtype)


def _attention(qt, kn, kpe, vt, batch, seq):
    m = kn.shape[0]
    tq = min(2048, seq)
    tk = min(512, seq)
    nq = seq // tq
    nk = seq // tk
    return pl.pallas_call(
        functools.partial(_attn_kernel, n_split=max(tq // 512, 1)),
        grid=(batch, MLA_HEADS, nq, nk),
        in_specs=[pl.BlockSpec((QK_PAD, tq), lambda b, h, i, j: (h, b * nq + i)),
                  pl.BlockSpec((tk, NOPE_DIM), lambda b, h, i, j: (b * nk + j, h)),
                  pl.BlockSpec((tk, LANES), lambda b, h, i, j: (b * nk + j, 0)),
                  pl.BlockSpec((V_DIM, tk), lambda b, h, i, j: (h, b * nk + j))],
        out_specs=pl.BlockSpec((tq, V_DIM), lambda b, h, i, j: (b * nq + i, h)),
        out_shape=jax.ShapeDtypeStruct((m, MLA_HEADS * V_DIM), BF16),
        scratch_shapes=[pltpu.VMEM((1, tq), F32), pltpu.VMEM((1, tq), F32), pltpu.VMEM((V_DIM, tq), F32)],
        compiler_params=_params("parallel", "parallel", "parallel", "arbitrary"),
        name="mla_attention",
    )(qt, kn, kpe, vt)


def _proj_res_kernel(a_ref, w_ref, x_ref, g_ref, o_ref, acc_sc):
    k = pl.program_id(1)

    @pl.when(k == 0)
    def _():
        acc_sc[...] = jnp.zeros(acc_sc.shape, F32)

    acc_sc[...] += jnp.dot(a_ref[...], w_ref[...], preferred_element_type=F32)

    @pl.when(k == pl.num_programs(1) - 1)
    def _():
        o_ref[...] = x_ref[...] + _rms(acc_sc[...], g_ref[...])


def _proj_res(a, w, x, g, name):
    m, kdim = a.shape
    n = w.shape[1]
    tm = min(512, m)
    tk = min(512, kdim)
    return pl.pallas_call(
        _proj_res_kernel,
        grid=(m // tm, kdim // tk),
        in_specs=[pl.BlockSpec((tm, tk), lambda i, k: (i, k)), pl.BlockSpec((tk, n), lambda i, k: (k, 0)),
                  pl.BlockSpec((tm, n), lambda i, k: (i, 0)), pl.BlockSpec((1, n), lambda i, k: (0, 0))],
        out_specs=pl.BlockSpec((tm, n), lambda i, k: (i, 0)),
        out_shape=jax.ShapeDtypeStruct((m, n), F32),
        scratch_shapes=[pltpu.VMEM((tm, n), F32)],
        compiler_params=_params("parallel", "arbitrary"),
        name=name,
    )(a, w, x, g)


def _ffn_kernel(x_ref, g_pre_ref, wg_ref, wu_ref, wd_ref, g_post_ref, o_ref, xn_sc, acc_sc):
    f = pl.program_id(1)

    @pl.when(f == 0)
    def _():
        xn_sc[...] = _rms(x_ref[...], g_pre_ref[...]).astype(BF16)
        acc_sc[...] = jnp.zeros(acc_sc.shape, F32)

    xn = xn_sc[...]
    gate = jnp.dot(xn, wg_ref[...], preferred_element_type=F32)
    up = jnp.dot(xn, wu_ref[...], preferred_element_type=F32)
    act = (gate * _sigmoid(gate) * up).astype(BF16)
    acc_sc[...] += jnp.dot(act, wd_ref[...], preferred_element_type=F32)

    @pl.when(f == pl.num_programs(1) - 1)
    def _():
        o_ref[...] = x_ref[...] + _rms(acc_sc[...], g_post_ref[...])


def _ffn(x, g_pre, w_gu, w_down, g_post):
    m, d = x.shape
    d_ff = w_down.shape[0]
    tm = min(512, m)
    tf = 512
    nf = d_ff // tf
    return pl.pallas_call(
        _ffn_kernel,
        grid=(m // tm, nf),
        in_specs=[pl.BlockSpec((tm, d), lambda i, f: (i, 0)), pl.BlockSpec((1, d), lambda i, f: (0, 0)),
                  pl.BlockSpec((d, tf), lambda i, f: (0, f)), pl.BlockSpec((d, tf), lambda i, f: (0, f + nf)),
                  pl.BlockSpec((tf, d), lambda i, f: (f, 0)), pl.BlockSpec((1, d), lambda i, f: (0, 0))],
        out_specs=pl.BlockSpec((tm, d), lambda i, f: (i, 0)),
        out_shape=jax.ShapeDtypeStruct((m, d), F32),
        scratch_shapes=[pltpu.VMEM((tm, d), BF16), pltpu.VMEM((tm, d), F32)],
        compiler_params=_params("parallel", "arbitrary"),
        name="ffn",
    )(x, g_pre, w_gu, w_gu, w_down, g_post)


def _rwkv_mix_kernel(x_ref, xp_ref, xn_ref, g_ref, mu_ref, xr_ref, xw_ref, xk_ref, xv_ref, xa_ref, xg_ref,
                     *, tiles_per_seq):
    i = pl.program_id(0)
    g = g_ref[...]
    h = _rms(x_ref[...], g)
    tm = h.shape[0]
    first = (i % tiles_per_seq) == 0
    last = (i % tiles_per_seq) == tiles_per_seq - 1
    hp = jnp.where(first, 0.0, _rms(xp_ref[7:8, :], g))
    hn = jnp.where(last, 0.0, _rms(xn_ref[0:1, :], g))
    row = lax.broadcasted_iota(jnp.int32, h.shape, 0)
    h_prev = jnp.where(row == 0, hp, pltpu.roll(h, 1, 0))
    h_next = jnp.where(row == tm - 1, hn, pltpu.roll(h, tm - 1, 0))
    xx = 0.5 * (h_prev + h_next) - h
    for idx, ref in enumerate((xr_ref, xw_ref, xk_ref, xv_ref, xa_ref, xg_ref)):
        ref[...] = (h + xx * mu_ref[idx:idx + 1, :]).astype(BF16)


def _rwkv_mix(x, g, mu, seq):
    m, d = x.shape
    tm = min(256, seq)
    tps = seq // tm
    nb8 = m // 8
    r8 = tm // 8
    out = jax.ShapeDtypeStruct((m, d), BF16)
    row = lambda i: (i, 0)
    return pl.pallas_call(
        functools.partial(_rwkv_mix_kernel, tiles_per_seq=tps),
        grid=(m // tm,),
        in_specs=[pl.BlockSpec((tm, d), row),
                  pl.BlockSpec((8, d), lambda i: (jnp.maximum(i * r8 - 1, 0), 0)),
                  pl.BlockSpec((8, d), lambda i: (jnp.minimum((i + 1) * r8, nb8 - 1), 0)),
                  pl.BlockSpec((1, d), lambda i: (0, 0)), pl.BlockSpec((8, d), lambda i: (0, 0))],
        out_specs=[pl.BlockSpec((tm, d), row)] * 6,
        out_shape=[out] * 6,
        compiler_params=_params("parallel"),
        name="rwkv_mix",
    )(x, x, x, g, mu)


def _lora_kernel(x_ref, a_ref, b_ref, bias_ref, o_ref, *, mid, post):
    t = jnp.dot(x_ref[...], a_ref[...], preferred_element_type=F32)
    if mid == "tanh":
        t = jnp.tanh(t)
    elif mid == "sigmoid":
        t = _sigmoid(t)
    z = jnp.dot(t.astype(BF16), b_ref[...], preferred_element_type=F32) + bias_ref[...]
    if post == "sigmoid":
        z = _sigmoid(z)
    elif post == "logdecay":
        z = -math.exp(-0.5) * _sigmoid(z)
    o_ref[...] = z.astype(o_ref.dtype)


def _lora(x, a, b, bias, mid, post, out_dtype, name):
    m, d = x.shape
    r = a.shape[1]
    n = b.shape[1]
    tm = min(512, m)
    return pl.pallas_call(
        functools.partial(_lora_kernel, mid=mid, post=post),
        grid=(m // tm,),
        in_specs=[pl.BlockSpec((tm, d), lambda i: (i, 0)), pl.BlockSpec((d, r), lambda i: (0, 0)),
                  pl.BlockSpec((r, n), lambda i: (0, 0)), pl.BlockSpec((1, n), lambda i: (0, 0))],
        out_specs=pl.BlockSpec((tm, n), lambda i: (i, 0)),
        out_shape=jax.ShapeDtypeStruct((m, n), out_dtype),
        compiler_params=_params("parallel"),
        name=name,
    )(x, a, b, bias)


def _wkv_prep_kernel(r_ref, k_ref, v_ref, a0_ref, a1_ref, kk_par, ka_par, rk_par,
                     kk_ref, b0_ref, b1_ref, kd0_ref, kd1_ref, bonus_ref):
    ones_bd = _head_ones(LANES)
    d = r_ref.shape[1]
    for c in range(d // LANES):
        sl = slice(c * LANES, (c + 1) * LANES)
        k = k_ref[:, sl]
        a0 = a0_ref[:, sl]
        a1 = a1_ref[:, sl]
        k_a = ka_par[:, sl]
        kkr = k * kk_par[:, sl]
        nrm = jnp.maximum(jnp.sqrt(_head_sum(kkr * kkr, ones_bd)), 1e-12)
        kk = kkr / nrm
        kd0 = k * (1.0 + (a0 - 1.0) * k_a)
        kd1 = k * (1.0 + (a1 - 1.0) * k_a)
        kk_ref[:, sl] = kk
        b0_ref[:, sl] = kk * a0
        b1_ref[:, sl] = kk * a1
        kd0_ref[:, sl] = kd0
        kd1_ref[:, sl] = kd1
        bonus_ref[:, sl] = _head_sum(r_ref[:, sl] * (kd0 + kd1) * rk_par[:, sl], ones_bd) * v_ref[:, sl]


def _wkv_prep(r, k, v, a0, a1, k_k, k_a, r_k):
    m, d = r.shape
    tm = min(256, m)
    row = pl.BlockSpec((tm, d), lambda i: (i, 0))
    par = pl.BlockSpec((1, d), lambda i: (0, 0))
    out = jax.ShapeDtypeStruct((m, d), F32)
    return pl.pallas_call(
        _wkv_prep_kernel,
        grid=(m // tm,),
        in_specs=[row] * 5 + [par] * 3,
        out_specs=[row] * 6,
        out_shape=[out] * 6,
        compiler_params=_params("parallel"),
        name="wkv_prep",
    )(r, k, v, a0, a1, k_k, k_a, r_k)


def _wkv_kernel(r_ref, kd_ref, v_ref, kk_ref, b_ref, lw_ref, y_ref, st_ref, *, reverse):
    t_len = WKV_CHUNK
    gl = st_ref.shape[0]
    ng = gl // RWKV_HEAD
    gt = ng * t_len
    n_chunks = r_ref.shape[0] // t_len

    @pl.when(pl.program_id(2) == 0)
    def _():
        st_ref[...] = jnp.zeros(st_ref.shape, F32)

    row = lax.broadcasted_iota(jnp.int32, (gt, gt), 0)
    col = lax.broadcasted_iota(jnp.int32, (gt, gt), 1)
    same = (row // t_len) == (col // t_len)
    if reverse:
        strict = same & (col > row)
        incl = same & (col >= row)
    else:
        strict = same & (col < row)
        incl = same & (col <= row)
    eye = jnp.where(row == col, 1.0, 0.0)
    trow = lax.broadcasted_iota(jnp.int32, (t_len, t_len), 0)
    tcol = lax.broadcasted_iota(jnp.int32, (t_len, t_len), 1)
    tri = jnp.where((tcol >= trow) if reverse else (tcol <= trow), 1.0, 0.0).astype(BF16)
    blk_mask = (lax.broadcasted_iota(jnp.int32, (gt, gl), 0) // t_len
                == lax.broadcasted_iota(jnp.int32, (gt, gl), 1) // RWKV_HEAD)
    head_mask = (lax.broadcasted_iota(jnp.int32, (gl, gl), 0) // RWKV_HEAD
                 == lax.broadcasted_iota(jnp.int32, (gl, gl), 1) // RWKV_HEAD)

    def tile(x):
        return jnp.concatenate([x] * ng, axis=0)

    def blk(x):
        return jnp.where(blk_mask, tile(x), 0.0)

    def unblk(x):
        out = x[0:t_len]
        for h in range(1, ng):
            out = out + x[h * t_len:(h + 1) * t_len]
        return out

    def chunk(ci, carry):
        c = (n_chunks - 1 - ci) if reverse else ci
        rows = pl.ds(pl.multiple_of(c * t_len, t_len), t_len)
        r = r_ref[rows, :]
        kd = kd_ref[rows, :]
        v = v_ref[rows, :]
        kk = kk_ref[rows, :]
        b = b_ref[rows, :]
        lw = lw_ref[rows, :]

        lw_hi = lw.astype(BF16)
        lw_lo = (lw - lw_hi.astype(F32)).astype(BF16)
        cum = (jnp.dot(tri, lw_hi, preferred_element_type=F32)
               + jnp.dot(tri, lw_lo, preferred_element_type=F32))
        tot = cum[0:1] if reverse else cum[t_len - 1:t_len]
        inv_p = jnp.exp(-cum)
        rt = r * jnp.exp(cum)
        kkt = kk * jnp.exp(cum - lw)
        bh = b * inv_p
        kh = kd * inv_p
        to_end = jnp.exp(tot - cum)
        b_end = b * to_end
        k_end = kd * to_end

        lhs = jnp.concatenate([blk(kkt), blk(rt)], axis=0).astype(BF16)
        rhs = jnp.concatenate([tile(bh), tile(kh)], axis=0).astype(BF16)
        aa = lax.dot_general(lhs, rhs, NT, preferred_element_type=F32)
        a_ab = jnp.where(strict, aa[:gt, :gt], 0.0)
        a_ak = jnp.where(strict, aa[:gt, gt:], 0.0)
        a_rb = jnp.where(incl, aa[gt:, :gt], 0.0)
        a_rk = jnp.where(incl, aa[gt:, gt:], 0.0)

        x = eye - a_ab
        p = _mm(a_ab, a_ab)
        rounds = int(math.log2(t_len)) - 1
        for it in range(rounds):
            x = x + _mm(x, p)
            if it < rounds - 1:
                p = _mm(p, p)

        st = st_ref[...]
        st_b = st.astype(BF16)
        ks = lax.dot_general(kkt.astype(BF16), st_b, NT, preferred_element_type=F32)
        rs = lax.dot_general(rt.astype(BF16), st_b, NT, preferred_element_type=F32)
        v_blk = blk(v)
        sa_blk = -_mm(x, blk(ks) + _mm(a_ak, v_blk))
        y_blk = _mm(a_rb, sa_blk) + _mm(a_rk, v_blk)
        y_ref[rows, :] = rs + unblk(y_blk)
        sa = unblk(sa_blk)
        upd = lax.dot_general(jnp.concatenate([sa, v], axis=0).astype(BF16),
                              jnp.concatenate([b_end, k_end], axis=0).astype(BF16),
                              TN, preferred_element_type=F32)
        st_ref[...] = st * jnp.exp(tot) + jnp.where(head_mask, upd, 0.0)
        return carry

    lax.fori_loop(0, n_chunks, chunk, 0)


def _wkv(r, kd, v, kk, b, lw, batch, seq, reverse):
    m, d = r.shape
    gl = WKV_GROUP * RWKV_HEAD
    tb = min(WKV_BLOCK, seq)
    nb = seq // tb
    if reverse:
        idx = lambda bi, g, j: (bi * nb + nb - 1 - j, g)
    else:
        idx = lambda bi, g, j: (bi * nb + j, g)
    spec = pl.BlockSpec((tb, gl), idx)
    return pl.pallas_call(
        functools.partial(_wkv_kernel, reverse=reverse),
        grid=(batch, d // gl, nb),
        in_specs=[spec] * 6,
        out_specs=spec,
        out_shape=jax.ShapeDtypeStruct((m, d), F32),
        scratch_shapes=[pltpu.VMEM((gl, gl), F32)],
        compiler_params=_params("parallel", "parallel", "arbitrary"),
        name="wkv_bwd" if reverse else "wkv_fwd",
    )(r, kd, v, kk, b, lw)


def _wkv_post_kernel(y0_ref, y1_ref, bonus_ref, g_ref, lg_ref, lb_ref, o_ref):
    ones_bd = _head_ones(LANES)
    d = y0_ref.shape[1]
    inv_n = 1.0 / RWKV_HEAD
    for c in range(d // LANES):
        sl = slice(c * LANES, (c + 1) * LANES)
        y = y0_ref[:, sl] + y1_ref[:, sl]
        mean = _head_sum(y, ones_bd) * inv_n
        yc = y - mean
        var = _head_sum(yc * yc, ones_bd) * inv_n
        yn = yc * lax.rsqrt(var + LNX_EPS) * lg_ref[:, sl] + lb_ref[:, sl]
        o_ref[:, sl] = ((yn + bonus_ref[:, sl]) * g_ref[:, sl]).astype(o_ref.dtype)


def _wkv_post(y0, y1, bonus, g, lnx_g, lnx_b):
    m, d = y0.shape
    tm = min(256, m)
    row = pl.BlockSpec((tm, d), lambda i: (i, 0))
    par = pl.BlockSpec((1, d), lambda i: (0, 0))
    return pl.pallas_call(
        _wkv_post_kernel,
        grid=(m // tm,),
        in_specs=[row] * 4 + [par] * 2,
        out_specs=row,
        out_shape=jax.ShapeDtypeStruct((m, d), BF16),
        compiler_params=_params("parallel"),
        name="wkv_post",
    )(y0, y1, bonus, g, lnx_g, lnx_b)


def _rope_tables(seq):
    half = ROPE_DIM // 2
    inv = 1.0 / (ROPE_THETA ** (jnp.arange(half, dtype=F32) * (2.0 / ROPE_DIM)))
    ang = jnp.arange(seq, dtype=F32)[:, None] * inv[None, :]
    cos, sin = jnp.cos(ang), jnp.sin(ang)
    zero = jnp.zeros((seq, LANES - ROPE_DIM), F32)
    return (jnp.concatenate([cos, cos, zero], axis=1), jnp.concatenate([-sin, sin, zero], axis=1))


def _row(v):
    return v.reshape(1, -1).astype(F32)


def _pad_cols(w, n):
    return jnp.pad(w, ((0, 0), (0, n - w.shape[1])))


def _pad_rows(w, n):
    return jnp.pad(w, ((0, n - w.shape[0]), (0, 0)))


def _mla_layer(x, batch, seq, g_pre, g_post, w_in, g_q, g_kv, w_uq, w_ukv, w_o):
    cosw, sinw = _rope_tables(seq)
    w_in_pad = _pad_cols(w_in, Q_LORA + KV_LORA + LANES).astype(BF16)
    w_uq_pad = jnp.pad(w_uq.reshape(Q_LORA, MLA_HEADS, NOPE_DIM + ROPE_DIM),
                       ((0, 0), (0, 0), (0, QK_PAD - NOPE_DIM - ROPE_DIM))).reshape(Q_LORA, MLA_HEADS * QK_PAD)
    cq, ckv, kpe = _mla_in(x, _row(g_pre), w_in_pad, _row(g_q), _row(g_kv), cosw, sinw, seq)
    qt = _mla_q(cq, w_uq_pad.astype(BF16), cosw, sinw, seq)
    kn, vt = _mla_kv(ckv, w_ukv.astype(BF16))
    o = _attention(qt, kn, kpe, vt, batch, seq)
    return _proj_res(o, w_o.astype(BF16), x, _row(g_post), "mla_out")


def _rwkv_layer(x, batch, seq, g_pre, g_post, mu, w_r, w_k, w_v, w_o, w0, w1, w2, a0, a1, a2, g1, g2,
                k_k, k_a, r_k, lnx_g, lnx_b):
    d = x.shape[1]
    mu8 = jnp.pad(mu, ((0, 8 - mu.shape[0]), (0, 0)))
    xr, xw, xk, xv, xa, xg = _rwkv_mix(x, _row(g_pre), mu8, seq)
    r = _matmul(xr, w_r.astype(BF16), F32, "rwkv_r")
    k = _matmul(xk, w_k.astype(BF16), F32, "rwkv_k")
    v = _matmul(xv, w_v.astype(BF16), F32, "rwkv_v")
    zero = jnp.zeros((1, d), F32)
    g = _lora(xg, g1.astype(BF16), g2.astype(BF16), zero, "sigmoid", "none", BF16, "rwkv_gate")
    lws, avs = [], []
    for di in range(2):
        w1p = _pad_cols(w1[di], LANES).astype(BF16)
        w2p = _pad_rows(w2[di], LANES).astype(BF16)
        a1p = _pad_cols(a1[di], LANES).astype(BF16)
        a2p = _pad_rows(a2[di], LANES).astype(BF16)
        lws.append(_lora(xw, w1p, w2p, _row(w0[di]), "tanh", "logdecay", F32, "rwkv_decay%d" % di))
        avs.append(_lora(xa, a1p, a2p, _row(a0[di]), "none", "sigmoid", F32, "rwkv_a%d" % di))
    kk, b0, b1, kd0, kd1, bonus = _wkv_prep(r, k, v, avs[0], avs[1], _row(k_k), _row(k_a), _row(r_k))
    y0 = _wkv(r, kd0, v, kk, b0, lws[0], batch, seq, False)
    y1 = _wkv(r, kd1, v, kk, b1, lws[1], batch, seq, True)
    yg = _wkv_post(y0, y1, bonus, g, _row(lnx_g), _row(lnx_b))
    return _proj_res(yg, w_o.astype(BF16), x, _row(g_post), "rwkv_out")


def _trunk(x3, norm_g, mla_w_in, mla_g_q, mla_g_kv, mla_w_uq, mla_w_ukv, mla_w_o,
           rwkv_mu, rwkv_w_r, rwkv_w_k, rwkv_w_v, rwkv_w_o, rwkv_w0, rwkv_w1, rwkv_w2,
           rwkv_a0, rwkv_a1, rwkv_a2, rwkv_g1, rwkv_g2, rwkv_k_k, rwkv_k_a, rwkv_r_k,
           rwkv_lnx_g, rwkv_lnx_b, ffn_w_gu, ffn_w_down):
    batch, seq, d = x3.shape
    x = x3.reshape(batch * seq, d)
    depth = norm_g.shape[0]
    for i in range(depth):
        j = i // 2
        if i % 2 == 0:
            x = _mla_layer(x, batch, seq, norm_g[i, 0], norm_g[i, 1], mla_w_in[j], mla_g_q[j], mla_g_kv[j],
                           mla_w_uq[j], mla_w_ukv[j], mla_w_o[j])
        else:
            x = _rwkv_layer(x, batch, seq, norm_g[i, 0], norm_g[i, 1], rwkv_mu[j], rwkv_w_r[j], rwkv_w_k[j],
                            rwkv_w_v[j], rwkv_w_o[j], rwkv_w0[j], rwkv_w1[j], rwkv_w2[j], rwkv_a0[j],
                            rwkv_a1[j], rwkv_a2[j], rwkv_g1[j], rwkv_g2[j], rwkv_k_k[j], rwkv_k_a[j],
                            rwkv_r_k[j].reshape(-1), rwkv_lnx_g[j], rwkv_lnx_b[j])
        x = _ffn(x, _row(norm_g[i, 2]), ffn_w_gu[i].astype(BF16), ffn_w_down[i].astype(BF16), _row(norm_g[i, 3]))
    return x.reshape(batch, seq, d)


def kernel(x_prompt, x_sample, norm_g, mla_w_in, mla_g_q, mla_g_kv, mla_w_uq, mla_w_ukv, mla_w_o, rwkv_mu, rwkv_w_r, rwkv_w_k, rwkv_w_v, rwkv_w_o, rwkv_w0, rwkv_w1, rwkv_w2, rwkv_a0, rwkv_a1, rwkv_a2, rwkv_g1, rwkv_g2, rwkv_k_k, rwkv_k_a, rwkv_r_k, rwkv_lnx_g, rwkv_lnx_b, ffn_w_gu, ffn_w_down):
    params = (norm_g, mla_w_in, mla_g_q, mla_g_kv, mla_w_uq, mla_w_ukv, mla_w_o,
              rwkv_mu, rwkv_w_r, rwkv_w_k, rwkv_w_v, rwkv_w_o, rwkv_w0, rwkv_w1, rwkv_w2,
              rwkv_a0, rwkv_a1, rwkv_a2, rwkv_g1, rwkv_g2, rwkv_k_k, rwkv_k_a, rwkv_r_k,
              rwkv_lnx_g, rwkv_lnx_b, ffn_w_gu, ffn_w_down)
    return (_trunk(x_prompt, *params), _trunk(x_sample, *params))
```

```python
import functools
import math

import jax
import jax.numpy as jnp
from jax import lax
from jax.experimental import pallas as pl
from jax.experimental.pallas import tpu as pltpu

F32 = jnp.float32
BF16 = jnp.bfloat16

NORM_EPS = 1e-6
LNX_EPS = 64e-5
ROPE_THETA = 10000.0

MLA_HEADS = 16
Q_LORA = 512
KV_LORA = 512
NOPE_DIM = 128
ROPE_DIM = 64
V_DIM = 128
QK_PAD = 256
RWKV_HEAD = 64

LANES = 128
VMEM_LIMIT = 56 * 1024 * 1024

WKV_CHUNK = 64
WKV_GROUP = 4
WKV_BLOCK = 512

NT = (((1,), (1,)), ((), ()))
TN = (((0,), (0,)), ((), ()))


def _params(*sem):
    return pltpu.CompilerParams(dimension_semantics=sem, vmem_limit_bytes=VMEM_LIMIT)


def _mm(a, b):
    return jnp.dot(a.astype(BF16), b.astype(BF16), preferred_element_type=F32)


def _rms(x, g):
    return x * lax.rsqrt(jnp.mean(x * x, axis=-1, keepdims=True) + NORM_EPS) * g


def _sigmoid(z):
    return 1.0 / (1.0 + jnp.exp(-z))


def _rope_upper(up, cosw, sinw):
    lane = lax.broadcasted_iota(jnp.int32, up.shape, 1)
    swapped = jnp.where(lane < ROPE_DIM // 2, pltpu.roll(up, LANES - ROPE_DIM // 2, 1),
                        pltpu.roll(up, ROPE_DIM // 2, 1))
    return up * cosw + swapped * sinw


def _head_ones(n):
    r = lax.broadcasted_iota(jnp.int32, (n, n), 0) // RWKV_HEAD
    c = lax.broadcasted_iota(jnp.int32, (n, n), 1) // RWKV_HEAD
    return jnp.where(r == c, 1.0, 0.0).astype(BF16)


def _head_sum(z, ones_bd):
    hi = z.astype(BF16)
    lo = (z - hi.astype(F32)).astype(BF16)
    return (jnp.dot(hi, ones_bd, preferred_element_type=F32)
            + jnp.dot(lo, ones_bd, preferred_element_type=F32))


def _mla_in_kernel(x_ref, g_ref, w_ref, gq_ref, gkv_ref, cos_ref, sin_ref, cq_ref, ckv_ref, kpe_ref):
    xn = _rms(x_ref[...], g_ref[...]).astype(BF16)
    h = jnp.dot(xn, w_ref[...], preferred_element_type=F32)
    cq_ref[...] = _rms(h[:, :Q_LORA], gq_ref[...]).astype(BF16)
    ckv_ref[...] = _rms(h[:, Q_LORA:Q_LORA + KV_LORA], gkv_ref[...]).astype(BF16)
    kpe_ref[...] = _rope_upper(h[:, Q_LORA + KV_LORA:], cos_ref[...], sin_ref[...]).astype(BF16)


def _mla_in(x, g, w_in_pad, g_q, g_kv, cosw, sinw, seq):
    m, d = x.shape
    tm = min(512, seq)
    nseq = seq // tm
    n = w_in_pad.shape[1]
    row = lambda i: (i, 0)
    fix = lambda i: (0, 0)
    tab = lambda i: (i % nseq, 0)
    return pl.pallas_call(
        _mla_in_kernel,
        grid=(m // tm,),
        in_specs=[pl.BlockSpec((tm, d), row), pl.BlockSpec((1, d), fix), pl.BlockSpec((d, n), fix),
                  pl.BlockSpec((1, Q_LORA), fix), pl.BlockSpec((1, KV_LORA), fix),
                  pl.BlockSpec((tm, LANES), tab), pl.BlockSpec((tm, LANES), tab)],
        out_specs=[pl.BlockSpec((tm, Q_LORA), row), pl.BlockSpec((tm, KV_LORA), row),
                   pl.BlockSpec((tm, LANES), row)],
        out_shape=[jax.ShapeDtypeStruct((m, Q_LORA), BF16), jax.ShapeDtypeStruct((m, KV_LORA), BF16),
                   jax.ShapeDtypeStruct((m, LANES), BF16)],
        compiler_params=_params("parallel"),
        name="mla_in",
    )(x, g, w_in_pad, g_q, g_kv, cosw, sinw)


def _mla_q_kernel(cq_ref, w_ref, cos_ref, sin_ref, q_ref, *, scale):
    res = jnp.dot(cq_ref[...], w_ref[...], preferred_element_type=F32)
    cosw = cos_ref[...]
    sinw = sin_ref[...]
    for h in range(MLA_HEADS):
        lo = res[:, h * QK_PAD:h * QK_PAD + NOPE_DIM]
        up = _rope_upper(res[:, h * QK_PAD + NOPE_DIM:(h + 1) * QK_PAD], cosw, sinw)
        q_ref[h * QK_PAD:h * QK_PAD + NOPE_DIM, :] = (lo * scale).T.astype(BF16)
        q_ref[h * QK_PAD + NOPE_DIM:(h + 1) * QK_PAD, :] = (up * scale).T.astype(BF16)


def _mla_q(cq, w_uq_pad, cosw, sinw, seq):
    m, c = cq.shape
    n = w_uq_pad.shape[1]
    tm = min(256, seq)
    nseq = seq // tm
    scale = (NOPE_DIM + ROPE_DIM) ** -0.5 * math.log2(math.e)
    return pl.pallas_call(
        functools.partial(_mla_q_kernel, scale=scale),
        grid=(m // tm,),
        in_specs=[pl.BlockSpec((tm, c), lambda i: (i, 0)), pl.BlockSpec((c, n), lambda i: (0, 0)),
                  pl.BlockSpec((tm, LANES), lambda i: (i % nseq, 0)),
                  pl.BlockSpec((tm, LANES), lambda i: (i % nseq, 0))],
        out_specs=pl.BlockSpec((n, tm), lambda i: (0, i)),
        out_shape=jax.ShapeDtypeStruct((n, m), BF16),
        compiler_params=_params("parallel"),
        name="mla_q",
    )(cq, w_uq_pad, cosw, sinw)


def _mla_kv_kernel(ckv_ref, w_ref, kn_ref, vt_ref):
    res = jnp.dot(ckv_ref[...], w_ref[...], preferred_element_type=F32)
    for h in range(MLA_HEADS):
        kn_ref[:, h * NOPE_DIM:(h + 1) * NOPE_DIM] = res[:, 2 * h * NOPE_DIM:(2 * h + 1) * NOPE_DIM].astype(BF16)
        vt_ref[h * V_DIM:(h + 1) * V_DIM, :] = res[:, (2 * h + 1) * V_DIM:(2 * h + 2) * V_DIM].T.astype(BF16)


def _mla_kv(ckv, w_ukv):
    m, c = ckv.shape
    n = w_ukv.shape[1]
    tm = min(256, m)
    return pl.pallas_call(
        _mla_kv_kernel,
        grid=(m // tm,),
        in_specs=[pl.BlockSpec((tm, c), lambda i: (i, 0)), pl.BlockSpec((c, n), lambda i: (0, 0))],
        out_specs=[pl.BlockSpec((tm, n // 2), lambda i: (i, 0)), pl.BlockSpec((n // 2, tm), lambda i: (0, i))],
        out_shape=[jax.ShapeDtypeStruct((m, n // 2), BF16), jax.ShapeDtypeStruct((n // 2, m), BF16)],
        compiler_params=_params("parallel"),
        name="mla_kv",
    )(ckv, w_ukv)


def _matmul_kernel(a_ref, w_ref, o_ref):
    o_ref[...] = jnp.dot(a_ref[...], w_ref[...], preferred_element_type=F32).astype(o_ref.dtype)


def _matmul(a, w, out_dtype, name):
    m, k = a.shape
    n = w.shape[1]
    tm = min(512, m)
    tn = min(1024, n)
    return pl.pallas_call(
        _matmul_kernel,
        grid=(m // tm, n // tn),
        in_specs=[pl.BlockSpec((tm, k), lambda i, j: (i, 0)), pl.BlockSpec((k, tn), lambda i, j: (0, j))],
        out_specs=pl.BlockSpec((tm, tn), lambda i, j: (i, j)),
        out_shape=jax.ShapeDtypeStruct((m, n), out_dtype),
        compiler_params=_params("parallel", "parallel"),
        name=name,
    )(a, w)


def _attn_kernel(qt_ref, kn0_ref, kpe0_ref, kn_ref, kpe_ref, vt_ref, o_ref, sa_sc, sb_sc, m_sc, l_sc, acc_sc,
                 *, n_split):
    j = pl.program_id(3)
    width = qt_ref.shape[1] // n_split

    def scores_into(dst_ref, kn, kpe):
        k = jnp.concatenate([kn[...], kpe[...]], axis=1)
        for c in range(n_split):
            cols = slice(c * width, (c + 1) * width)
            dst_ref[:, cols] = jnp.dot(k, qt_ref[:, cols], preferred_element_type=F32)

    @pl.when(j == 0)
    def _():
        m_sc[...] = jnp.full(m_sc.shape, -jnp.inf, F32)
        l_sc[...] = jnp.zeros(l_sc.shape, F32)
        acc_sc[...] = jnp.zeros(acc_sc.shape, F32)
        scores_into(sa_sc, kn0_ref, kpe0_ref)

    def step(cur_sc, nxt_sc):
        k = jnp.concatenate([kn_ref[...], kpe_ref[...]], axis=1)
        vt = vt_ref[...]
        for c in range(n_split):
            cols = slice(c * width, (c + 1) * width)
            nxt_sc[:, cols] = jnp.dot(k, qt_ref[:, cols], preferred_element_type=F32)
            st = cur_sc[:, cols]
            m_prev = m_sc[:, cols]
            m_new = jnp.maximum(m_prev, jnp.max(st, axis=0, keepdims=True))
            alpha = jnp.exp2(m_prev - m_new)
            pt = jnp.exp2(st - m_new)
            l_sc[:, cols] = alpha * l_sc[:, cols] + jnp.sum(pt, axis=0, keepdims=True)
            acc_sc[:, cols] = alpha * acc_sc[:, cols] + jnp.dot(vt, pt.astype(BF16), preferred_element_type=F32)
            m_sc[:, cols] = m_new

    @pl.when(j % 2 == 0)
    def _():
        step(sa_sc, sb_sc)

    @pl.when(j % 2 == 1)
    def _():
        step(sb_sc, sa_sc)

    @pl.when(j == pl.num_programs(3) - 1)
    def _():
        o_ref[...] = (acc_sc[...] / l_sc[...]).T.astype(o_ref.dtype)


def _attention(qt, kn, kpe, vt, batch, seq):
    m = kn.shape[0]
    tq = min(2048, seq)
    tk = min(512, seq)
    nq = seq // tq
    nk = seq // tk
    first = lambda b, h, i, j: b * nk
    ahead = lambda b, h, i, j: b * nk + jnp.minimum(j + 1, nk - 1)
    return pl.pallas_call(
        functools.partial(_attn_kernel, n_split=max(tq // 512, 1)),
        grid=(batch, MLA_HEADS, nq, nk),
        in_specs=[pl.BlockSpec((QK_PAD, tq), lambda b, h, i, j: (h, b * nq + i)),
                  pl.BlockSpec((tk, NOPE_DIM), lambda b, h, i, j: (first(b, h, i, j), h)),
                  pl.BlockSpec((tk, LANES), lambda b, h, i, j: (first(b, h, i, j), 0)),
                  pl.BlockSpec((tk, NOPE_DIM), lambda b, h, i, j: (ahead(b, h, i, j), h)),
                  pl.BlockSpec((tk, LANES), lambda b, h, i, j: (ahead(b, h, i, j), 0)),
                  pl.BlockSpec((V_DIM, tk), lambda b, h, i, j: (h, b * nk + j))],
        out_specs=pl.BlockSpec((tq, V_DIM), lambda b, h, i, j: (b * nq + i, h)),
        out_shape=jax.ShapeDtypeStruct((m, MLA_HEADS * V_DIM), BF16),
        scratch_shapes=[pltpu.VMEM((tk, tq), F32), pltpu.VMEM((tk, tq), F32),
                        pltpu.VMEM((1, tq), F32), pltpu.VMEM((1, tq), F32), pltpu.VMEM((V_DIM, tq), F32)],
        compiler_params=_params("parallel", "parallel", "parallel", "arbitrary"),
        name="mla_attention",
    )(qt, kn, kpe, kn, kpe, vt)


def _proj_res_kernel(a_ref, w_ref, x_ref, g_ref, o_ref, acc_sc):
    k = pl.program_id(1)

    @pl.when(k == 0)
    def _():
        acc_sc[...] = jnp.zeros(acc_sc.shape, F32)

    acc_sc[...] += jnp.dot(a_ref[...], w_ref[...], preferred_element_type=F32)

    @pl.when(k == pl.num_programs(1) - 1)
    def _():
        o_ref[...] = x_ref[...] + _rms(acc_sc[...], g_ref[...])


def _proj_res(a, w, x, g, name):
    m, kdim = a.shape
    n = w.shape[1]
    tm = min(512, m)
    tk = min(512, kdim)
    return pl.pallas_call(
        _proj_res_kernel,
        grid=(m // tm, kdim // tk),
        in_specs=[pl.BlockSpec((tm, tk), lambda i, k: (i, k)), pl.BlockSpec((tk, n), lambda i, k: (k, 0)),
                  pl.BlockSpec((tm, n), lambda i, k: (i, 0)), pl.BlockSpec((1, n), lambda i, k: (0, 0))],
        out_specs=pl.BlockSpec((tm, n), lambda i, k: (i, 0)),
        out_shape=jax.ShapeDtypeStruct((m, n), F32),
        scratch_shapes=[pltpu.VMEM((tm, n), F32)],
        compiler_params=_params("parallel", "arbitrary"),
        name=name,
    )(a, w, x, g)


def _ffn_kernel(x_ref, g_pre_ref, wg_ref, wu_ref, wd_ref, g_post_ref, o_ref, xn_sc, acc_sc):
    f = pl.program_id(1)

    @pl.when(f == 0)
    def _():
        xn_sc[...] = _rms(x_ref[...], g_pre_ref[...]).astype(BF16)
        acc_sc[...] = jnp.zeros(acc_sc.shape, F32)

    xn = xn_sc[...]
    gate = jnp.dot(xn, wg_ref[...], preferred_element_type=F32)
    up = jnp.dot(xn, wu_ref[...], preferred_element_type=F32)
    act = (gate * _sigmoid(gate) * up).astype(BF16)
    acc_sc[...] += jnp.dot(act, wd_ref[...], preferred_element_type=F32)

    @pl.when(f == pl.num_programs(1) - 1)
    def _():
        o_ref[...] = x_ref[...] + _rms(acc_sc[...], g_post_ref[...])


def _ffn(x, g_pre, w_gu, w_down, g_post):
    m, d = x.shape
    d_ff = w_down.shape[0]
    tm = min(512, m)
    tf = 512
    nf = d_ff // tf
    return pl.pallas_call(
        _ffn_kernel,
        grid=(m // tm, nf),
        in_specs=[pl.BlockSpec((tm, d), lambda i, f: (i, 0)), pl.BlockSpec((1, d), lambda i, f: (0, 0)),
                  pl.BlockSpec((d, tf), lambda i, f: (0, f)), pl.BlockSpec((d, tf), lambda i, f: (0, f + nf)),
                  pl.BlockSpec((tf, d), lambda i, f: (f, 0)), pl.BlockSpec((1, d), lambda i, f: (0, 0))],
        out_specs=pl.BlockSpec((tm, d), lambda i, f: (i, 0)),
        out_shape=jax.ShapeDtypeStruct((m, d), F32),
        scratch_shapes=[pltpu.VMEM((tm, d), BF16), pltpu.VMEM((tm, d), F32)],
        compiler_params=_params("parallel", "arbitrary"),
        name="ffn",
    )(x, g_pre, w_gu, w_gu, w_down, g_post)


def _rwkv_mix_kernel(x_ref, xp_ref, xn_ref, g_ref, mu_ref, xr_ref, xw_ref, xk_ref, xv_ref, xa_ref, xg_ref,
                     *, tiles_per_seq):
    i = pl.program_id(0)
    g = g_ref[...]
    h = _rms(x_ref[...], g)
    tm = h.shape[0]
    first = (i % tiles_per_seq) == 0
    last = (i % tiles_per_seq) == tiles_per_seq - 1
    hp = jnp.where(first, 0.0, _rms(xp_ref[7:8, :], g))
    hn = jnp.where(last, 0.0, _rms(xn_ref[0:1, :], g))
    row = lax.broadcasted_iota(jnp.int32, h.shape, 0)
    h_prev = jnp.where(row == 0, hp, pltpu.roll(h, 1, 0))
    h_next = jnp.where(row == tm - 1, hn, pltpu.roll(h, tm - 1, 0))
    xx = 0.5 * (h_prev + h_next) - h
    for idx, ref in enumerate((xr_ref, xw_ref, xk_ref, xv_ref, xa_ref, xg_ref)):
        ref[...] = (h + xx * mu_ref[idx:idx + 1, :]).astype(BF16)


def _rwkv_mix(x, g, mu, seq):
    m, d = x.shape
    tm = min(256, seq)
    tps = seq // tm
    nb8 = m // 8
    r8 = tm // 8
    out = jax.ShapeDtypeStruct((m, d), BF16)
    row = lambda i: (i, 0)
    return pl.pallas_call(
        functools.partial(_rwkv_mix_kernel, tiles_per_seq=tps),
        grid=(m // tm,),
        in_specs=[pl.BlockSpec((tm, d), row),
                  pl.BlockSpec((8, d), lambda i: (jnp.maximum(i * r8 - 1, 0), 0)),
                  pl.BlockSpec((8, d), lambda i: (jnp.minimum((i + 1) * r8, nb8 - 1), 0)),
                  pl.BlockSpec((1, d), lambda i: (0, 0)), pl.BlockSpec((8, d), lambda i: (0, 0))],
        out_specs=[pl.BlockSpec((tm, d), row)] * 6,
        out_shape=[out] * 6,
        compiler_params=_params("parallel"),
        name="rwkv_mix",
    )(x, x, x, g, mu)


def _lora_kernel(x_ref, a_ref, b_ref, bias_ref, o_ref, *, mid, post):
    t = jnp.dot(x_ref[...], a_ref[...], preferred_element_type=F32)
    if mid == "tanh":
        t = jnp.tanh(t)
    elif mid == "sigmoid":
        t = _sigmoid(t)
    z = jnp.dot(t.astype(BF16), b_ref[...], preferred_element_type=F32) + bias_ref[...]
    if post == "sigmoid":
        z = _sigmoid(z)
    elif post == "logdecay":
        z = -math.exp(-0.5) * _sigmoid(z)
    o_ref[...] = z.astype(o_ref.dtype)


def _lora(x, a, b, bias, mid, post, out_dtype, name):
    m, d = x.shape
    r = a.shape[1]
    n = b.shape[1]
    tm = min(512, m)
    return pl.pallas_call(
        functools.partial(_lora_kernel, mid=mid, post=post),
        grid=(m // tm,),
        in_specs=[pl.BlockSpec((tm, d), lambda i: (i, 0)), pl.BlockSpec((d, r), lambda i: (0, 0)),
                  pl.BlockSpec((r, n), lambda i: (0, 0)), pl.BlockSpec((1, n), lambda i: (0, 0))],
        out_specs=pl.BlockSpec((tm, n), lambda i: (i, 0)),
        out_shape=jax.ShapeDtypeStruct((m, n), out_dtype),
        compiler_params=_params("parallel"),
        name=name,
    )(x, a, b, bias)


def _wkv_prep_kernel(r_ref, k_ref, v_ref, a0_ref, a1_ref, kk_par, ka_par, rk_par,
                     kk_ref, b0_ref, b1_ref, kd0_ref, kd1_ref, bonus_ref):
    ones_bd = _head_ones(LANES)
    d = r_ref.shape[1]
    for c in range(d // LANES):
        sl = slice(c * LANES, (c + 1) * LANES)
        k = k_ref[:, sl]
        a0 = a0_ref[:, sl]
        a1 = a1_ref[:, sl]
        k_a = ka_par[:, sl]
        kkr = k * kk_par[:, sl]
        nrm = jnp.maximum(jnp.sqrt(_head_sum(kkr * kkr, ones_bd)), 1e-12)
        kk = kkr / nrm
        kd0 = k * (1.0 + (a0 - 1.0) * k_a)
        kd1 = k * (1.0 + (a1 - 1.0) * k_a)
        kk_ref[:, sl] = kk
        b0_ref[:, sl] = kk * a0
        b1_ref[:, sl] = kk * a1
        kd0_ref[:, sl] = kd0
        kd1_ref[:, sl] = kd1
        bonus_ref[:, sl] = _head_sum(r_ref[:, sl] * (kd0 + kd1) * rk_par[:, sl], ones_bd) * v_ref[:, sl]


def _wkv_prep(r, k, v, a0, a1, k_k, k_a, r_k):
    m, d = r.shape
    tm = min(256, m)
    row = pl.BlockSpec((tm, d), lambda i: (i, 0))
    par = pl.BlockSpec((1, d), lambda i: (0, 0))
    out = jax.ShapeDtypeStruct((m, d), F32)
    return pl.pallas_call(
        _wkv_prep_kernel,
        grid=(m // tm,),
        in_specs=[row] * 5 + [par] * 3,
        out_specs=[row] * 6,
        out_shape=[out] * 6,
        compiler_params=_params("parallel"),
        name="wkv_prep",
    )(r, k, v, a0, a1, k_k, k_a, r_k)


def _wkv_pre(r_ref, kd_ref, v_ref, kk_ref, b_ref, lw_ref, blk, head_mask, gl, *, reverse):
    t_len = WKV_CHUNK
    ng = gl // RWKV_HEAD
    gt = ng * t_len
    n_chunks = r_ref.shape[0] // t_len

    wrow = lax.broadcasted_iota(jnp.int32, (t_len, gt), 0)
    wcol = lax.broadcasted_iota(jnp.int32, (t_len, gt), 1) % t_len
    strict = (wcol > wrow) if reverse else (wcol < wrow)
    incl = (wcol >= wrow) if reverse else (wcol <= wrow)
    eye_w = jnp.where(wcol == wrow, 1.0, 0.0)
    trow = lax.broadcasted_iota(jnp.int32, (t_len, t_len), 0)
    tcol = lax.broadcasted_iota(jnp.int32, (t_len, t_len), 1)
    tri = jnp.where((tcol >= trow) if reverse else (tcol <= trow), 1.0, 0.0).astype(BF16)
    bd_mask = jnp.where(lax.broadcasted_iota(jnp.int32, (gt, gt), 0) // t_len
                        == lax.broadcasted_iota(jnp.int32, (gt, gt), 1) // t_len, 1.0, 0.0).astype(BF16)

    def bdiag(w):
        return jnp.concatenate([w.astype(BF16)] * ng, axis=0) * bd_mask

    def mmb(a, b_bf16):
        return jnp.dot(a.astype(BF16), b_bf16, preferred_element_type=F32)

    items = []
    for ci in range(n_chunks):
        c = (n_chunks - 1 - ci) if reverse else ci
        items.append({"rows": slice(c * t_len, (c + 1) * t_len)})

    for it in items:
        lw = lw_ref[it["rows"], :]
        lw_hi = lw.astype(BF16)
        lw_lo = (lw - lw_hi.astype(F32)).astype(BF16)
        it["lw"] = lw
        cum2 = jnp.dot(tri, jnp.concatenate([lw_hi, lw_lo], axis=1), preferred_element_type=F32)
        it["cum"] = cum2[:, :gl] + cum2[:, gl:]

    for it in items:
        rows, cum, lw = it["rows"], it["cum"], it["lw"]
        kd = kd_ref[rows, :]
        b = b_ref[rows, :]
        tot = cum[0:1] if reverse else cum[t_len - 1:t_len]
        inv_p = jnp.exp(-cum)
        rt = r_ref[rows, :] * jnp.exp(cum)
        kkt = kk_ref[rows, :] * jnp.exp(cum - lw)
        to_end = jnp.exp(tot - cum)
        it["decay"] = jnp.exp(tot)
        it["b_end"] = (b * to_end).astype(BF16)
        it["k_end"] = (kd * to_end).astype(BF16)
        it["rt"] = rt.astype(BF16)
        it["kkt_blk"] = blk(kkt)
        lhs = jnp.concatenate([kkt, rt], axis=0).astype(BF16)
        rhs = jnp.concatenate([blk(b * inv_p), blk(kd * inv_p)], axis=0)
        aa = lax.dot_general(lhs, rhs, NT, preferred_element_type=F32)
        it["a_ab"] = jnp.where(strict, aa[:t_len, :gt], 0.0)
        it["a_ak"] = jnp.where(strict, aa[:t_len, gt:], 0.0).astype(BF16)
        it["a_r"] = jnp.concatenate([jnp.where(incl, aa[t_len:, :gt], 0.0),
                                     jnp.where(incl, aa[t_len:, gt:], 0.0)], axis=1).astype(BF16)

    for it in items:
        it["x"] = eye_w - it["a_ab"]
        it["p"] = mmb(it["a_ab"], bdiag(it["a_ab"]))
    rounds = int(math.log2(t_len)) - 1
    for rnd in range(rounds):
        for it in items:
            p_bd = bdiag(it["p"])
            if rnd < rounds - 1:
                xp = mmb(jnp.concatenate([it["x"], it["p"]], axis=0), p_bd)
                it["x"] = it["x"] + xp[:t_len]
                it["p"] = xp[t_len:]
            else:
                it["x"] = it["x"] + mmb(it["x"], p_bd)

    for it in items:
        v = v_ref[it["rows"], :]
        it["v"] = v
        it["v_blk"] = blk(v)
        it["akv"] = jnp.dot(it["a_ak"], it["v_blk"], preferred_element_type=F32)
        it["x_b"] = it["x"].astype(BF16)
        it["w_k"] = jnp.dot(it["x_b"], it["kkt_blk"], preferred_element_type=F32)
    for it in items:
        it["u"] = jnp.dot(it["x_b"], blk(it["akv"]), preferred_element_type=F32)
    for it in items:
        m_full = lax.dot_general(it["w_k"].astype(BF16), it["b_end"], TN, preferred_element_type=F32)
        it["m"] = jnp.where(head_mask, m_full, 0.0).astype(BF16)
    for it in items:
        c_full = lax.dot_general(jnp.concatenate([-it["u"], it["v"]], axis=0).astype(BF16),
                                 jnp.concatenate([it["b_end"], it["k_end"]], axis=0),
                                 TN, preferred_element_type=F32)
        it["c"] = jnp.where(head_mask, c_full, 0.0)
        it["wr"] = jnp.concatenate([it["w_k"].astype(BF16), it["rt"]], axis=0)
    return items


def _wkv2_kernel(rf_ref, kdf_ref, vf_ref, kkf_ref, bf_ref, lwf_ref,
                 rb_ref, kdb_ref, vb_ref, kkb_ref, bb_ref, lwb_ref, yf_ref, yb_ref, stf_ref, stb_ref):
    @pl.when(pl.program_id(2) == 0)
    def _():
        stf_ref[...] = jnp.zeros(stf_ref.shape, F32)
        stb_ref[...] = jnp.zeros(stb_ref.shape, F32)

    gl = stf_ref.shape[0]
    gt = (gl // RWKV_HEAD) * WKV_CHUNK
    blk_mask = jnp.where(lax.broadcasted_iota(jnp.int32, (gt, gl), 0) // WKV_CHUNK
                         == lax.broadcasted_iota(jnp.int32, (gt, gl), 1) // RWKV_HEAD, 1.0, 0.0).astype(BF16)
    head_mask = (lax.broadcasted_iota(jnp.int32, (gl, gl), 0) // RWKV_HEAD
                 == lax.broadcasted_iota(jnp.int32, (gl, gl), 1) // RWKV_HEAD)

    def blk(x):
        return jnp.concatenate([x.astype(BF16)] * (gl // RWKV_HEAD), axis=0) * blk_mask

    t_len = WKV_CHUNK
    chains = [
        (_wkv_pre(rf_ref, kdf_ref, vf_ref, kkf_ref, bf_ref, lwf_ref, blk, head_mask, gl, reverse=False),
         yf_ref, stf_ref),
        (_wkv_pre(rb_ref, kdb_ref, vb_ref, kkb_ref, bb_ref, lwb_ref, blk, head_mask, gl, reverse=True),
         yb_ref, stb_ref),
    ]
    states = [st_ref[...] for _, _, st_ref in chains]

    def emit_y(y_ref, it, sa, rs):
        y_ref[it["rows"], :] = rs + jnp.dot(it["a_r"], jnp.concatenate([blk(sa), it["v_blk"]], axis=0),
                                            preferred_element_type=F32)

    pending = []
    for ci in range(len(chains[0][0])):
        st_bf = [st.astype(BF16) for st in states]
        its = [items[ci] for items, _, _ in chains]
        st_m = [jnp.dot(sb, it["m"], preferred_element_type=F32) for sb, it in zip(st_bf, its)]
        ws = [lax.dot_general(it["wr"], sb, NT, preferred_element_type=F32) for sb, it in zip(st_bf, its)]
        for args in pending:
            emit_y(*args)
        pending = [(y_ref, it, -(w[:t_len] + it["u"]), w[t_len:])
                   for (_, y_ref, _), it, w in zip(chains, its, ws)]
        states = [st * it["decay"] - sm + it["c"] for st, it, sm in zip(states, its, st_m)]
    for args in pending:
        emit_y(*args)
    for (_, _, st_ref), st in zip(chains, states):
        st_ref[...] = st


def _wkv2(r, v, kk, kd0, b0, lw0, kd1, b1, lw1, batch, seq):
    m, d = r.shape
    gl = WKV_GROUP * RWKV_HEAD
    tb = min(WKV_BLOCK, seq)
    nb = seq // tb
    fwd = pl.BlockSpec((tb, gl), lambda bi, g, j: (bi * nb + j, g))
    bwd = pl.BlockSpec((tb, gl), lambda bi, g, j: (bi * nb + nb - 1 - j, g))
    out = jax.ShapeDtypeStruct((m, d), F32)
    return pl.pallas_call(
        _wkv2_kernel,
        grid=(batch, d // gl, nb),
        in_specs=[fwd] * 6 + [bwd] * 6,
        out_specs=[fwd, bwd],
        out_shape=[out, out],
        scratch_shapes=[pltpu.VMEM((gl, gl), F32), pltpu.VMEM((gl, gl), F32)],
        compiler_params=_params("parallel", "parallel", "arbitrary"),
        name="wkv",
    )(r, kd0, v, kk, b0, lw0, r, kd1, v, kk, b1, lw1)


def _wkv_post_kernel(y0_ref, y1_ref, bonus_ref, g_ref, lg_ref, lb_ref, o_ref):
    ones_bd = _head_ones(LANES)
    d = y0_ref.shape[1]
    inv_n = 1.0 / RWKV_HEAD
    for c in range(d // LANES):
        sl = slice(c * LANES, (c + 1) * LANES)
        y = y0_ref[:, sl] + y1_ref[:, sl]
        mean = _head_sum(y, ones_bd) * inv_n
        yc = y - mean
        var = _head_sum(yc * yc, ones_bd) * inv_n
        yn = yc * lax.rsqrt(var + LNX_EPS) * lg_ref[:, sl] + lb_ref[:, sl]
        o_ref[:, sl] = ((yn + bonus_ref[:, sl]) * g_ref[:, sl]).astype(o_ref.dtype)


def _wkv_post(y0, y1, bonus, g, lnx_g, lnx_b):
    m, d = y0.shape
    tm = min(256, m)
    row = pl.BlockSpec((tm, d), lambda i: (i, 0))
    par = pl.BlockSpec((1, d), lambda i: (0, 0))
    return pl.pallas_call(
        _wkv_post_kernel,
        grid=(m // tm,),
        in_specs=[row] * 4 + [par] * 2,
        out_specs=row,
        out_shape=jax.ShapeDtypeStruct((m, d), BF16),
        compiler_params=_params("parallel"),
        name="wkv_post",
    )(y0, y1, bonus, g, lnx_g, lnx_b)


def _rope_tables(seq):
    half = ROPE_DIM // 2
    inv = 1.0 / (ROPE_THETA ** (jnp.arange(half, dtype=F32) * (2.0 / ROPE_DIM)))
    ang = jnp.arange(seq, dtype=F32)[:, None] * inv[None, :]
    cos, sin = jnp.cos(ang), jnp.sin(ang)
    zero = jnp.zeros((seq, LANES - ROPE_DIM), F32)
    return (jnp.concatenate([cos, cos, zero], axis=1), jnp.concatenate([-sin, sin, zero], axis=1))


def _row(v):
    return v.reshape(1, -1).astype(F32)


def _pad_cols(w, n):
    return jnp.pad(w, ((0, 0), (0, n - w.shape[1])))


def _pad_rows(w, n):
    return jnp.pad(w, ((0, n - w.shape[0]), (0, 0)))


def _mla_layer(x, batch, seq, g_pre, g_post, w_in, g_q, g_kv, w_uq, w_ukv, w_o):
    cosw, sinw = _rope_tables(seq)
    w_in_pad = _pad_cols(w_in, Q_LORA + KV_LORA + LANES).astype(BF16)
    w_uq_pad = jnp.pad(w_uq.reshape(Q_LORA, MLA_HEADS, NOPE_DIM + ROPE_DIM),
                       ((0, 0), (0, 0), (0, QK_PAD - NOPE_DIM - ROPE_DIM))).reshape(Q_LORA, MLA_HEADS * QK_PAD)
    cq, ckv, kpe = _mla_in(x, _row(g_pre), w_in_pad, _row(g_q), _row(g_kv), cosw, sinw, seq)
    qt = _mla_q(cq, w_uq_pad.astype(BF16), cosw, sinw, seq)
    kn, vt = _mla_kv(ckv, w_ukv.astype(BF16))
    o = _attention(qt, kn, kpe, vt, batch, seq)
    return _proj_res(o, w_o.astype(BF16), x, _row(g_post), "mla_out")


def _rwkv_layer(x, batch, seq, g_pre, g_post, mu, w_r, w_k, w_v, w_o, w0, w1, w2, a0, a1, a2, g1, g2,
                k_k, k_a, r_k, lnx_g, lnx_b):
    d = x.shape[1]
    mu8 = jnp.pad(mu, ((0, 8 - mu.shape[0]), (0, 0)))
    xr, xw, xk, xv, xa, xg = _rwkv_mix(x, _row(g_pre), mu8, seq)
    r = _matmul(xr, w_r.astype(BF16), F32, "rwkv_r")
    k = _matmul(xk, w_k.astype(BF16), F32, "rwkv_k")
    v = _matmul(xv, w_v.astype(BF16), F32, "rwkv_v")
    zero = jnp.zeros((1, d), F32)
    g = _lora(xg, g1.astype(BF16), g2.astype(BF16), zero, "sigmoid", "none", BF16, "rwkv_gate")
    lws, avs = [], []
    for di in range(2):
        w1p = _pad_cols(w1[di], LANES).astype(BF16)
        w2p = _pad_rows(w2[di], LANES).astype(BF16)
        a1p = _pad_cols(a1[di], LANES).astype(BF16)
        a2p = _pad_rows(a2[di], LANES).astype(BF16)
        lws.append(_lora(xw, w1p, w2p, _row(w0[di]), "tanh", "logdecay", F32, "rwkv_decay%d" % di))
        avs.append(_lora(xa, a1p, a2p, _row(a0[di]), "none", "sigmoid", F32, "rwkv_a%d" % di))
    kk, b0, b1, kd0, kd1, bonus = _wkv_prep(r, k, v, avs[0], avs[1], _row(k_k), _row(k_a), _row(r_k))
    y0, y1 = _wkv2(r, v, kk, kd0, b0, lws[0], kd1, b1, lws[1], batch, seq)
    yg = _wkv_post(y0, y1, bonus, g, _row(lnx_g), _row(lnx_b))
    return _proj_res(yg, w_o.astype(BF16), x, _row(g_post), "rwkv_out")


def _trunk(x3, norm_g, mla_w_in, mla_g_q, mla_g_kv, mla_w_uq, mla_w_ukv, mla_w_o,
           rwkv_mu, rwkv_w_r, rwkv_w_k, rwkv_w_v, rwkv_w_o, rwkv_w0, rwkv_w1, rwkv_w2,
           rwkv_a0, rwkv_a1, rwkv_a2, rwkv_g1, rwkv_g2, rwkv_k_k, rwkv_k_a, rwkv_r_k,
           rwkv_lnx_g, rwkv_lnx_b, ffn_w_gu, ffn_w_down):
    batch, seq, d = x3.shape
    x = x3.reshape(batch * seq, d)
    depth = norm_g.shape[0]
    for i in range(depth):
        j = i // 2
        if i % 2 == 0:
            x = _mla_layer(x, batch, seq, norm_g[i, 0], norm_g[i, 1], mla_w_in[j], mla_g_q[j], mla_g_kv[j],
                           mla_w_uq[j], mla_w_ukv[j], mla_w_o[j])
        else:
            x = _rwkv_layer(x, batch, seq, norm_g[i, 0], norm_g[i, 1], rwkv_mu[j], rwkv_w_r[j], rwkv_w_k[j],
                            rwkv_w_v[j], rwkv_w_o[j], rwkv_w0[j], rwkv_w1[j], rwkv_w2[j], rwkv_a0[j],
                            rwkv_a1[j], rwkv_a2[j], rwkv_g1[j], rwkv_g2[j], rwkv_k_k[j], rwkv_k_a[j],
                            rwkv_r_k[j].reshape(-1), rwkv_lnx_g[j], rwkv_lnx_b[j])
        x = _ffn(x, _row(norm_g[i, 2]), ffn_w_gu[i].astype(BF16), ffn_w_down[i].astype(BF16), _row(norm_g[i, 3]))
    return x.reshape(batch, seq, d)


def kernel(x_prompt, x_sample, norm_g, mla_w_in, mla_g_q, mla_g_kv, mla_w_uq, mla_w_ukv, mla_w_o, rwkv_mu, rwkv_w_r, rwkv_w_k, rwkv_w_v, rwkv_w_o, rwkv_w0, rwkv_w1, rwkv_w2, rwkv_a0, rwkv_a1, rwkv_a2, rwkv_g1, rwkv_g2, rwkv_k_k, rwkv_k_a, rwkv_r_k, rwkv_lnx_g, rwkv_lnx_b, ffn_w_gu, ffn_w_down):
    params = (norm_g, mla_w_in, mla_g_q, mla_g_kv, mla_w_uq, mla_w_ukv, mla_w_o,
              rwkv_mu, rwkv_w_r, rwkv_w_k, rwkv_w_v, rwkv_w_o, rwkv_w0, rwkv_w1, rwkv_w2,
              rwkv_a0, rwkv_a1, rwkv_a2, rwkv_g1, rwkv_g2, rwkv_k_k, rwkv_k_a, rwkv_r_k,
              rwkv_lnx_g, rwkv_lnx_b, ffn_w_gu, ffn_w_down)
    return (_trunk(x_prompt, *params), _trunk(x_sample, *params))
```

```python
import functools
import math

import jax
import jax.numpy as jnp
from jax import lax
from jax.experimental import pallas as pl
from jax.experimental.pallas import tpu as pltpu

F32 = jnp.float32
BF16 = jnp.bfloat16

NORM_EPS = 1e-6
LNX_EPS = 64e-5
ROPE_THETA = 10000.0

MLA_HEADS = 16
Q_LORA = 512
KV_LORA = 512
NOPE_DIM = 128
ROPE_DIM = 64
V_DIM = 128
QK_PAD = 256
RWKV_HEAD = 64

LANES = 128
VMEM_LIMIT = 56 * 1024 * 1024

WKV_CHUNK = 64
WKV_GROUP = 4
WKV_BLOCK = 512

NT = (((1,), (1,)), ((), ()))
TN = (((0,), (0,)), ((), ()))


def _params(*sem):
    return pltpu.CompilerParams(dimension_semantics=sem, vmem_limit_bytes=VMEM_LIMIT)


def _mm(a, b):
    return jnp.dot(a.astype(BF16), b.astype(BF16), preferred_element_type=F32)


def _rms(x, g):
    return x * lax.rsqrt(jnp.mean(x * x, axis=-1, keepdims=True) + NORM_EPS) * g


def _sigmoid(z):
    return 1.0 / (1.0 + jnp.exp(-z))


def _rope_upper(up, cosw, sinw):
    lane = lax.broadcasted_iota(jnp.int32, up.shape, 1)
    swapped = jnp.where(lane < ROPE_DIM // 2, pltpu.roll(up, LANES - ROPE_DIM // 2, 1),
                        pltpu.roll(up, ROPE_DIM // 2, 1))
    return up * cosw + swapped * sinw


def _head_ones(n):
    r = lax.broadcasted_iota(jnp.int32, (n, n), 0) // RWKV_HEAD
    c = lax.broadcasted_iota(jnp.int32, (n, n), 1) // RWKV_HEAD
    return jnp.where(r == c, 1.0, 0.0).astype(BF16)


def _head_sum(z, ones_bd):
    hi = z.astype(BF16)
    lo = (z - hi.astype(F32)).astype(BF16)
    return (jnp.dot(hi, ones_bd, preferred_element_type=F32)
            + jnp.dot(lo, ones_bd, preferred_element_type=F32))


def _mla_in_kernel(x_ref, g_ref, w_ref, gq_ref, gkv_ref, cos_ref, sin_ref, cq_ref, ckv_ref, kpe_ref):
    xn = _rms(x_ref[...], g_ref[...]).astype(BF16)
    h = jnp.dot(xn, w_ref[...], preferred_element_type=F32)
    cq_ref[...] = _rms(h[:, :Q_LORA], gq_ref[...]).astype(BF16)
    ckv_ref[...] = _rms(h[:, Q_LORA:Q_LORA + KV_LORA], gkv_ref[...]).astype(BF16)
    kpe_ref[...] = _rope_upper(h[:, Q_LORA + KV_LORA:], cos_ref[...], sin_ref[...]).astype(BF16)


def _mla_in(x, g, w_in_pad, g_q, g_kv, cosw, sinw, seq):
    m, d = x.shape
    tm = min(512, seq)
    nseq = seq // tm
    n = w_in_pad.shape[1]
    row = lambda i: (i, 0)
    fix = lambda i: (0, 0)
    tab = lambda i: (i % nseq, 0)
    return pl.pallas_call(
        _mla_in_kernel,
        grid=(m // tm,),
        in_specs=[pl.BlockSpec((tm, d), row), pl.BlockSpec((1, d), fix), pl.BlockSpec((d, n), fix),
                  pl.BlockSpec((1, Q_LORA), fix), pl.BlockSpec((1, KV_LORA), fix),
                  pl.BlockSpec((tm, LANES), tab), pl.BlockSpec((tm, LANES), tab)],
        out_specs=[pl.BlockSpec((tm, Q_LORA), row), pl.BlockSpec((tm, KV_LORA), row),
                   pl.BlockSpec((tm, LANES), row)],
        out_shape=[jax.ShapeDtypeStruct((m, Q_LORA), BF16), jax.ShapeDtypeStruct((m, KV_LORA), BF16),
                   jax.ShapeDtypeStruct((m, LANES), BF16)],
        compiler_params=_params("parallel"),
        name="mla_in",
    )(x, g, w_in_pad, g_q, g_kv, cosw, sinw)


def _mla_q_kernel(cq_ref, w_ref, cos_ref, sin_ref, q_ref, *, scale):
    res = jnp.dot(cq_ref[...], w_ref[...], preferred_element_type=F32)
    cosw = cos_ref[...]
    sinw = sin_ref[...]
    for h in range(MLA_HEADS):
        lo = res[:, h * QK_PAD:h * QK_PAD + NOPE_DIM]
        up = _rope_upper(res[:, h * QK_PAD + NOPE_DIM:(h + 1) * QK_PAD], cosw, sinw)
        q_ref[h * QK_PAD:h * QK_PAD + NOPE_DIM, :] = (lo * scale).T.astype(BF16)
        q_ref[h * QK_PAD + NOPE_DIM:(h + 1) * QK_PAD, :] = (up * scale).T.astype(BF16)


def _mla_q(cq, w_uq_pad, cosw, sinw, seq):
    m, c = cq.shape
    n = w_uq_pad.shape[1]
    tm = min(256, seq)
    nseq = seq // tm
    scale = (NOPE_DIM + ROPE_DIM) ** -0.5 * math.log2(math.e)
    return pl.pallas_call(
        functools.partial(_mla_q_kernel, scale=scale),
        grid=(m // tm,),
        in_specs=[pl.BlockSpec((tm, c), lambda i: (i, 0)), pl.BlockSpec((c, n), lambda i: (0, 0)),
                  pl.BlockSpec((tm, LANES), lambda i: (i % nseq, 0)),
                  pl.BlockSpec((tm, LANES), lambda i: (i % nseq, 0))],
        out_specs=pl.BlockSpec((n, tm), lambda i: (0, i)),
        out_shape=jax.ShapeDtypeStruct((n, m), BF16),
        compiler_params=_params("parallel"),
        name="mla_q",
    )(cq, w_uq_pad, cosw, sinw)


def _mla_kv_kernel(ckv_ref, kpe_ref, w_ref, k_ref, vt_ref):
    res = jnp.dot(ckv_ref[...], w_ref[...], preferred_element_type=F32)
    kpe = kpe_ref[...]
    for h in range(MLA_HEADS):
        k_ref[h, :, :NOPE_DIM] = res[:, 2 * h * NOPE_DIM:(2 * h + 1) * NOPE_DIM].astype(BF16)
        k_ref[h, :, NOPE_DIM:] = kpe
        vt_ref[h * V_DIM:(h + 1) * V_DIM, :] = res[:, (2 * h + 1) * V_DIM:(2 * h + 2) * V_DIM].T.astype(BF16)


def _mla_kv(ckv, kpe, w_ukv):
    m, c = ckv.shape
    n = w_ukv.shape[1]
    tm = min(256, m)
    return pl.pallas_call(
        _mla_kv_kernel,
        grid=(m // tm,),
        in_specs=[pl.BlockSpec((tm, c), lambda i: (i, 0)), pl.BlockSpec((tm, LANES), lambda i: (i, 0)),
                  pl.BlockSpec((c, n), lambda i: (0, 0))],
        out_specs=[pl.BlockSpec((MLA_HEADS, tm, QK_PAD), lambda i: (0, i, 0)),
                   pl.BlockSpec((n // 2, tm), lambda i: (0, i))],
        out_shape=[jax.ShapeDtypeStruct((MLA_HEADS, m, QK_PAD), BF16), jax.ShapeDtypeStruct((n // 2, m), BF16)],
        compiler_params=_params("parallel"),
        name="mla_kv",
    )(ckv, kpe, w_ukv)


def _matmul_kernel(a_ref, w_ref, o_ref):
    o_ref[...] = jnp.dot(a_ref[...], w_ref[...], preferred_element_type=F32).astype(o_ref.dtype)


def _matmul(a, w, out_dtype, name):
    m, k = a.shape
    n = w.shape[1]
    tm = min(512, m)
    tn = min(1024, n)
    return pl.pallas_call(
        _matmul_kernel,
        grid=(m // tm, n // tn),
        in_specs=[pl.BlockSpec((tm, k), lambda i, j: (i, 0)), pl.BlockSpec((k, tn), lambda i, j: (0, j))],
        out_specs=pl.BlockSpec((tm, tn), lambda i, j: (i, j)),
        out_shape=jax.ShapeDtypeStruct((m, n), out_dtype),
        compiler_params=_params("parallel", "parallel"),
        name=name,
    )(a, w)


def _attn_kernel(qt_ref, k0_ref, k_ref, vt_ref, o_ref, sa_sc, sb_sc, m_sc, l_sc, acc_sc, *, n_split, n_kv):
    j = pl.program_id(3)
    width = qt_ref.shape[1] // n_split

    @pl.when(j == 0)
    def _():
        m_sc[...] = jnp.full(m_sc.shape, -jnp.inf, F32)
        l_sc[...] = jnp.zeros(l_sc.shape, F32)
        acc_sc[...] = jnp.zeros(acc_sc.shape, F32)
        k0 = k0_ref[0]
        for c in range(n_split):
            cols = slice(c * width, (c + 1) * width)
            sa_sc[:, cols] = jnp.dot(k0, qt_ref[:, cols], preferred_element_type=F32)

    def step(cur_sc, nxt_sc):
        k = k_ref[0]
        vt = vt_ref[...]
        for c in range(n_split):
            cols = slice(c * width, (c + 1) * width)
            if nxt_sc is not None:
                nxt_sc[:, cols] = jnp.dot(k, qt_ref[:, cols], preferred_element_type=F32)
            st = cur_sc[:, cols]
            m_prev = m_sc[:, cols]
            m_new = jnp.maximum(m_prev, jnp.max(st, axis=0, keepdims=True))
            alpha = jnp.exp2(m_prev - m_new)
            pt = jnp.exp2(st - m_new)
            l_sc[:, cols] = alpha * l_sc[:, cols] + jnp.sum(pt, axis=0, keepdims=True)
            acc_sc[:, cols] = alpha * acc_sc[:, cols] + jnp.dot(vt, pt.astype(BF16), preferred_element_type=F32)
            m_sc[:, cols] = m_new

    last = n_kv - 1

    @pl.when((j % 2 == 0) & (j < last))
    def _():
        step(sa_sc, sb_sc)

    @pl.when((j % 2 == 1) & (j < last))
    def _():
        step(sb_sc, sa_sc)

    @pl.when(j == last)
    def _():
        step(sa_sc if last % 2 == 0 else sb_sc, None)
        o_ref[...] = (acc_sc[...] / l_sc[...]).T.astype(o_ref.dtype)


def _attention(qt, k, vt, batch, seq):
    m = k.shape[1]
    tq = min(2048, seq)
    tk = min(1024, seq)
    nq = seq // tq
    nk = seq // tk
    return pl.pallas_call(
        functools.partial(_attn_kernel, n_split=max(tq // 512, 1), n_kv=nk),
        grid=(batch, MLA_HEADS, nq, nk),
        in_specs=[pl.BlockSpec((QK_PAD, tq), lambda b, h, i, j: (h, b * nq + i)),
                  pl.BlockSpec((1, tk, QK_PAD), lambda b, h, i, j: (h, b * nk, 0)),
                  pl.BlockSpec((1, tk, QK_PAD), lambda b, h, i, j: (h, b * nk + jnp.minimum(j + 1, nk - 1), 0)),
                  pl.BlockSpec((V_DIM, tk), lambda b, h, i, j: (h, b * nk + j))],
        out_specs=pl.BlockSpec((tq, V_DIM), lambda b, h, i, j: (b * nq + i, h)),
        out_shape=jax.ShapeDtypeStruct((m, MLA_HEADS * V_DIM), BF16),
        scratch_shapes=[pltpu.VMEM((tk, tq), F32), pltpu.VMEM((tk, tq), F32),
                        pltpu.VMEM((1, tq), F32), pltpu.VMEM((1, tq), F32), pltpu.VMEM((V_DIM, tq), F32)],
        compiler_params=_params("parallel", "parallel", "parallel", "arbitrary"),
        name="mla_attention",
    )(qt, k, k, vt)


def _proj_res_kernel(a_ref, w_ref, x_ref, g_ref, o_ref):
    h = jnp.dot(a_ref[...], w_ref[...], preferred_element_type=F32)
    o_ref[...] = x_ref[...] + _rms(h, g_ref[...])


def _proj_res(a, w, x, g, name):
    m, kdim = a.shape
    n = w.shape[1]
    tm = min(512, m)
    return pl.pallas_call(
        _proj_res_kernel,
        grid=(m // tm,),
        in_specs=[pl.BlockSpec((tm, kdim), lambda i: (i, 0)), pl.BlockSpec((kdim, n), lambda i: (0, 0)),
                  pl.BlockSpec((tm, n), lambda i: (i, 0)), pl.BlockSpec((1, n), lambda i: (0, 0))],
        out_specs=pl.BlockSpec((tm, n), lambda i: (i, 0)),
        out_shape=jax.ShapeDtypeStruct((m, n), F32),
        compiler_params=_params("parallel"),
        name=name,
    )(a, w, x, g)


def _ffn_kernel(x_ref, g_pre_ref, wg_ref, wu_ref, wd_ref, g_post_ref, o_ref, xn_sc, acc_sc):
    f = pl.program_id(1)

    @pl.when(f == 0)
    def _():
        xn_sc[...] = _rms(x_ref[...], g_pre_ref[...]).astype(BF16)
        acc_sc[...] = jnp.zeros(acc_sc.shape, F32)

    xn = xn_sc[...]
    gate = jnp.dot(xn, wg_ref[...], preferred_element_type=F32)
    up = jnp.dot(xn, wu_ref[...], preferred_element_type=F32)
    act = (gate * _sigmoid(gate) * up).astype(BF16)
    acc_sc[...] += jnp.dot(act, wd_ref[...], preferred_element_type=F32)

    @pl.when(f == pl.num_programs(1) - 1)
    def _():
        o_ref[...] = x_ref[...] + _rms(acc_sc[...], g_post_ref[...])


def _ffn(x, g_pre, w_gu, w_down, g_post):
    m, d = x.shape
    d_ff = w_down.shape[0]
    tm = min(512, m)
    tf = 512
    nf = d_ff // tf
    return pl.pallas_call(
        _ffn_kernel,
        grid=(m // tm, nf),
        in_specs=[pl.BlockSpec((tm, d), lambda i, f: (i, 0)), pl.BlockSpec((1, d), lambda i, f: (0, 0)),
                  pl.BlockSpec((d, tf), lambda i, f: (0, f)), pl.BlockSpec((d, tf), lambda i, f: (0, f + nf)),
                  pl.BlockSpec((tf, d), lambda i, f: (f, 0)), pl.BlockSpec((1, d), lambda i, f: (0, 0))],
        out_specs=pl.BlockSpec((tm, d), lambda i, f: (i, 0)),
        out_shape=jax.ShapeDtypeStruct((m, d), F32),
        scratch_shapes=[pltpu.VMEM((tm, d), BF16), pltpu.VMEM((tm, d), F32)],
        compiler_params=_params("parallel", "arbitrary"),
        name="ffn",
    )(x, g_pre, w_gu, w_gu, w_down, g_post)


def _rwkv_mix_kernel(x_ref, xp_ref, xn_ref, g_ref, mu_ref, xr_ref, xw_ref, xk_ref, xv_ref, xa_ref, xg_ref,
                     *, tiles_per_seq):
    i = pl.program_id(0)
    g = g_ref[...]
    h = _rms(x_ref[...], g)
    tm = h.shape[0]
    first = (i % tiles_per_seq) == 0
    last = (i % tiles_per_seq) == tiles_per_seq - 1
    hp = jnp.where(first, 0.0, _rms(xp_ref[7:8, :], g))
    hn = jnp.where(last, 0.0, _rms(xn_ref[0:1, :], g))
    row = lax.broadcasted_iota(jnp.int32, h.shape, 0)
    h_prev = jnp.where(row == 0, hp, pltpu.roll(h, 1, 0))
    h_next = jnp.where(row == tm - 1, hn, pltpu.roll(h, tm - 1, 0))
    xx = 0.5 * (h_prev + h_next) - h
    for idx, ref in enumerate((xr_ref, xw_ref, xk_ref, xv_ref, xa_ref, xg_ref)):
        ref[...] = (h + xx * mu_ref[idx:idx + 1, :]).astype(BF16)


def _rwkv_mix(x, g, mu, seq):
    m, d = x.shape
    tm = min(256, seq)
    tps = seq // tm
    nb8 = m // 8
    r8 = tm // 8
    out = jax.ShapeDtypeStruct((m, d), BF16)
    row = lambda i: (i, 0)
    return pl.pallas_call(
        functools.partial(_rwkv_mix_kernel, tiles_per_seq=tps),
        grid=(m // tm,),
        in_specs=[pl.BlockSpec((tm, d), row),
                  pl.BlockSpec((8, d), lambda i: (jnp.maximum(i * r8 - 1, 0), 0)),
                  pl.BlockSpec((8, d), lambda i: (jnp.minimum((i + 1) * r8, nb8 - 1), 0)),
                  pl.BlockSpec((1, d), lambda i: (0, 0)), pl.BlockSpec((8, d), lambda i: (0, 0))],
        out_specs=[pl.BlockSpec((tm, d), row)] * 6,
        out_shape=[out] * 6,
        compiler_params=_params("parallel"),
        name="rwkv_mix",
    )(x, x, x, g, mu)


def _lora_kernel(x_ref, a_ref, b_ref, bias_ref, o_ref, *, mid, post):
    t = jnp.dot(x_ref[...], a_ref[...], preferred_element_type=F32)
    if mid == "tanh":
        t = jnp.tanh(t)
    elif mid == "sigmoid":
        t = _sigmoid(t)
    z = jnp.dot(t.astype(BF16), b_ref[...], preferred_element_type=F32) + bias_ref[...]
    if post == "sigmoid":
        z = _sigmoid(z)
    elif post == "logdecay":
        z = -math.exp(-0.5) * _sigmoid(z)
    o_ref[...] = z.astype(o_ref.dtype)


def _lora(x, a, b, bias, mid, post, out_dtype, name):
    m, d = x.shape
    r = a.shape[1]
    n = b.shape[1]
    tm = min(512, m)
    return pl.pallas_call(
        functools.partial(_lora_kernel, mid=mid, post=post),
        grid=(m // tm,),
        in_specs=[pl.BlockSpec((tm, d), lambda i: (i, 0)), pl.BlockSpec((d, r), lambda i: (0, 0)),
                  pl.BlockSpec((r, n), lambda i: (0, 0)), pl.BlockSpec((1, n), lambda i: (0, 0))],
        out_specs=pl.BlockSpec((tm, n), lambda i: (i, 0)),
        out_shape=jax.ShapeDtypeStruct((m, n), out_dtype),
        compiler_params=_params("parallel"),
        name=name,
    )(x, a, b, bias)


def _wkv_prep_kernel(r_ref, k_ref, v_ref, a0_ref, a1_ref, kk_par, ka_par, rk_par,
                     kk_ref, b0_ref, b1_ref, kd0_ref, kd1_ref, bonus_ref):
    ones_bd = _head_ones(LANES)
    d = r_ref.shape[1]
    for c in range(d // LANES):
        sl = slice(c * LANES, (c + 1) * LANES)
        k = k_ref[:, sl]
        a0 = a0_ref[:, sl]
        a1 = a1_ref[:, sl]
        k_a = ka_par[:, sl]
        kkr = k * kk_par[:, sl]
        nrm = jnp.maximum(jnp.sqrt(_head_sum(kkr * kkr, ones_bd)), 1e-12)
        kk = kkr / nrm
        kd0 = k * (1.0 + (a0 - 1.0) * k_a)
        kd1 = k * (1.0 + (a1 - 1.0) * k_a)
        kk_ref[:, sl] = kk
        b0_ref[:, sl] = kk * a0
        b1_ref[:, sl] = kk * a1
        kd0_ref[:, sl] = kd0
        kd1_ref[:, sl] = kd1
        bonus_ref[:, sl] = _head_sum(r_ref[:, sl] * (kd0 + kd1) * rk_par[:, sl], ones_bd) * v_ref[:, sl]


def _wkv_prep(r, k, v, a0, a1, k_k, k_a, r_k):
    m, d = r.shape
    tm = min(256, m)
    row = pl.BlockSpec((tm, d), lambda i: (i, 0))
    par = pl.BlockSpec((1, d), lambda i: (0, 0))
    out = jax.ShapeDtypeStruct((m, d), F32)
    return pl.pallas_call(
        _wkv_prep_kernel,
        grid=(m // tm,),
        in_specs=[row] * 5 + [par] * 3,
        out_specs=[row] * 6,
        out_shape=[out] * 6,
        compiler_params=_params("parallel"),
        name="wkv_prep",
    )(r, k, v, a0, a1, k_k, k_a, r_k)


def _wkv_pre(r_ref, kd_ref, v_ref, kk_ref, b_ref, lw_ref, blk, head_mask, gl, *, reverse):
    t_len = WKV_CHUNK
    ng = gl // RWKV_HEAD
    gt = ng * t_len
    n_chunks = r_ref.shape[0] // t_len

    wrow = lax.broadcasted_iota(jnp.int32, (t_len, gt), 0)
    wcol = lax.broadcasted_iota(jnp.int32, (t_len, gt), 1) % t_len
    strict = (wcol > wrow) if reverse else (wcol < wrow)
    incl = (wcol >= wrow) if reverse else (wcol <= wrow)
    eye_w = jnp.where(wcol == wrow, 1.0, 0.0)
    trow = lax.broadcasted_iota(jnp.int32, (t_len, t_len), 0)
    tcol = lax.broadcasted_iota(jnp.int32, (t_len, t_len), 1)
    tri = jnp.where((tcol >= trow) if reverse else (tcol <= trow), 1.0, 0.0).astype(BF16)
    bd_mask = jnp.where(lax.broadcasted_iota(jnp.int32, (gt, gt), 0) // t_len
                        == lax.broadcasted_iota(jnp.int32, (gt, gt), 1) // t_len, 1.0, 0.0).astype(BF16)

    def bdiag(w):
        return jnp.concatenate([w.astype(BF16)] * ng, axis=0) * bd_mask

    def mmb(a, b_bf16):
        return jnp.dot(a.astype(BF16), b_bf16, preferred_element_type=F32)

    items = []
    for ci in range(n_chunks):
        c = (n_chunks - 1 - ci) if reverse else ci
        items.append({"rows": slice(c * t_len, (c + 1) * t_len)})

    for it in items:
        lw = lw_ref[it["rows"], :]
        lw_hi = lw.astype(BF16)
        lw_lo = (lw - lw_hi.astype(F32)).astype(BF16)
        it["lw"] = lw
        cum2 = jnp.dot(tri, jnp.concatenate([lw_hi, lw_lo], axis=1), preferred_element_type=F32)
        it["cum"] = cum2[:, :gl] + cum2[:, gl:]

    for it in items:
        rows, cum, lw = it["rows"], it["cum"], it["lw"]
        kd = kd_ref[rows, :]
        b = b_ref[rows, :]
        tot = cum[0:1] if reverse else cum[t_len - 1:t_len]
        inv_p = jnp.exp(-cum)
        rt = r_ref[rows, :] * jnp.exp(cum)
        kkt = kk_ref[rows, :] * jnp.exp(cum - lw)
        to_end = jnp.exp(tot - cum)
        it["decay"] = jnp.exp(tot)
        it["b_end"] = (b * to_end).astype(BF16)
        it["k_end"] = (kd * to_end).astype(BF16)
        it["rt"] = rt.astype(BF16)
        it["kkt_blk"] = blk(kkt)
        lhs = jnp.concatenate([kkt, rt], axis=0).astype(BF16)
        rhs = jnp.concatenate([blk(b * inv_p), blk(kd * inv_p)], axis=0)
        aa = lax.dot_general(lhs, rhs, NT, preferred_element_type=F32)
        it["a_ab"] = jnp.where(strict, aa[:t_len, :gt], 0.0)
        it["a_ak"] = jnp.where(strict, aa[:t_len, gt:], 0.0).astype(BF16)
        it["a_r"] = jnp.concatenate([jnp.where(incl, aa[t_len:, :gt], 0.0),
                                     jnp.where(incl, aa[t_len:, gt:], 0.0)], axis=1).astype(BF16)

    for it in items:
        it["x"] = eye_w - it["a_ab"]
        it["p"] = mmb(it["a_ab"], bdiag(it["a_ab"]))
    rounds = int(math.log2(t_len)) - 1
    for rnd in range(rounds):
        for it in items:
            p_bd = bdiag(it["p"])
            if rnd < rounds - 1:
                xp = mmb(jnp.concatenate([it["x"], it["p"]], axis=0), p_bd)
                it["x"] = it["x"] + xp[:t_len]
                it["p"] = xp[t_len:]
            else:
                it["x"] = it["x"] + mmb(it["x"], p_bd)

    for it in items:
        v = v_ref[it["rows"], :]
        it["v"] = v
        it["v_blk"] = blk(v)
        it["akv"] = jnp.dot(it["a_ak"], it["v_blk"], preferred_element_type=F32)
        it["x_b"] = it["x"].astype(BF16)
        it["w_k"] = jnp.dot(it["x_b"], it["kkt_blk"], preferred_element_type=F32)
    for it in items:
        it["u"] = jnp.dot(it["x_b"], blk(it["akv"]), preferred_element_type=F32)
    for it in items:
        m_full = lax.dot_general(it["w_k"].astype(BF16), it["b_end"], TN, preferred_element_type=F32)
        it["m"] = jnp.where(head_mask, m_full, 0.0).astype(BF16)
    for it in items:
        c_full = lax.dot_general(jnp.concatenate([-it["u"], it["v"]], axis=0).astype(BF16),
                                 jnp.concatenate([it["b_end"], it["k_end"]], axis=0),
                                 TN, preferred_element_type=F32)
        it["c"] = jnp.where(head_mask, c_full, 0.0)
        it["wr"] = jnp.concatenate([it["w_k"].astype(BF16), it["rt"]], axis=0)
    return items


def _wkv2_kernel(rf_ref, kdf_ref, vf_ref, kkf_ref, bf_ref, lwf_ref,
                 rb_ref, kdb_ref, vb_ref, kkb_ref, bb_ref, lwb_ref, yf_ref, yb_ref, stf_ref, stb_ref):
    @pl.when(pl.program_id(2) == 0)
    def _():
        stf_ref[...] = jnp.zeros(stf_ref.shape, F32)
        stb_ref[...] = jnp.zeros(stb_ref.shape, F32)

    gl = stf_ref.shape[0]
    gt = (gl // RWKV_HEAD) * WKV_CHUNK
    blk_mask = jnp.where(lax.broadcasted_iota(jnp.int32, (gt, gl), 0) // WKV_CHUNK
                         == lax.broadcasted_iota(jnp.int32, (gt, gl), 1) // RWKV_HEAD, 1.0, 0.0).astype(BF16)
    head_mask = (lax.broadcasted_iota(jnp.int32, (gl, gl), 0) // RWKV_HEAD
                 == lax.broadcasted_iota(jnp.int32, (gl, gl), 1) // RWKV_HEAD)

    def blk(x):
        return jnp.concatenate([x.astype(BF16)] * (gl // RWKV_HEAD), axis=0) * blk_mask

    t_len = WKV_CHUNK
    chains = [
        (_wkv_pre(rf_ref, kdf_ref, vf_ref, kkf_ref, bf_ref, lwf_ref, blk, head_mask, gl, reverse=False),
         yf_ref, stf_ref),
        (_wkv_pre(rb_ref, kdb_ref, vb_ref, kkb_ref, bb_ref, lwb_ref, blk, head_mask, gl, reverse=True),
         yb_ref, stb_ref),
    ]
    states = [st_ref[...] for _, _, st_ref in chains]

    def emit_y(y_ref, it, sa, rs):
        y_ref[it["rows"], :] = rs + jnp.dot(it["a_r"], jnp.concatenate([blk(sa), it["v_blk"]], axis=0),
                                            preferred_element_type=F32)

    pending = []
    for ci in range(len(chains[0][0])):
        st_bf = [st.astype(BF16) for st in states]
        its = [items[ci] for items, _, _ in chains]
        st_m = [jnp.dot(sb, it["m"], preferred_element_type=F32) for sb, it in zip(st_bf, its)]
        ws = [lax.dot_general(it["wr"], sb, NT, preferred_element_type=F32) for sb, it in zip(st_bf, its)]
        for args in pending:
            emit_y(*args)
        pending = [(y_ref, it, -(w[:t_len] + it["u"]), w[t_len:])
                   for (_, y_ref, _), it, w in zip(chains, its, ws)]
        states = [st * it["decay"] - sm + it["c"] for st, it, sm in zip(states, its, st_m)]
    for args in pending:
        emit_y(*args)
    for (_, _, st_ref), st in zip(chains, states):
        st_ref[...] = st


def _wkv2(r, v, kk, kd0, b0, lw0, kd1, b1, lw1, batch, seq):
    m, d = r.shape
    gl = WKV_GROUP * RWKV_HEAD
    tb = min(WKV_BLOCK, seq)
    nb = seq // tb
    fwd = pl.BlockSpec((tb, gl), lambda bi, g, j: (bi * nb + j, g))
    bwd = pl.BlockSpec((tb, gl), lambda bi, g, j: (bi * nb + nb - 1 - j, g))
    out = jax.ShapeDtypeStruct((m, d), F32)
    return pl.pallas_call(
        _wkv2_kernel,
        grid=(batch, d // gl, nb),
        in_specs=[fwd] * 6 + [bwd] * 6,
        out_specs=[fwd, bwd],
        out_shape=[out, out],
        scratch_shapes=[pltpu.VMEM((gl, gl), F32), pltpu.VMEM((gl, gl), F32)],
        compiler_params=_params("parallel", "parallel", "arbitrary"),
        name="wkv",
    )(r, kd0, v, kk, b0, lw0, r, kd1, v, kk, b1, lw1)


def _wkv_post_kernel(y0_ref, y1_ref, bonus_ref, g_ref, lg_ref, lb_ref, o_ref):
    ones_bd = _head_ones(LANES)
    d = y0_ref.shape[1]
    inv_n = 1.0 / RWKV_HEAD
    for c in range(d // LANES):
        sl = slice(c * LANES, (c + 1) * LANES)
        y = y0_ref[:, sl] + y1_ref[:, sl]
        mean = _head_sum(y, ones_bd) * inv_n
        yc = y - mean
        var = _head_sum(yc * yc, ones_bd) * inv_n
        yn = yc * lax.rsqrt(var + LNX_EPS) * lg_ref[:, sl] + lb_ref[:, sl]
        o_ref[:, sl] = ((yn + bonus_ref[:, sl]) * g_ref[:, sl]).astype(o_ref.dtype)


def _wkv_post(y0, y1, bonus, g, lnx_g, lnx_b):
    m, d = y0.shape
    tm = min(256, m)
    row = pl.BlockSpec((tm, d), lambda i: (i, 0))
    par = pl.BlockSpec((1, d), lambda i: (0, 0))
    return pl.pallas_call(
        _wkv_post_kernel,
        grid=(m // tm,),
        in_specs=[row] * 4 + [par] * 2,
        out_specs=row,
        out_shape=jax.ShapeDtypeStruct((m, d), BF16),
        compiler_params=_params("parallel"),
        name="wkv_post",
    )(y0, y1, bonus, g, lnx_g, lnx_b)


def _rope_tables(seq):
    half = ROPE_DIM // 2
    inv = 1.0 / (ROPE_THETA ** (jnp.arange(half, dtype=F32) * (2.0 / ROPE_DIM)))
    ang = jnp.arange(seq, dtype=F32)[:, None] * inv[None, :]
    cos, sin = jnp.cos(ang), jnp.sin(ang)
    zero = jnp.zeros((seq, LANES - ROPE_DIM), F32)
    return (jnp.concatenate([cos, cos, zero], axis=1), jnp.concatenate([-sin, sin, zero], axis=1))


def _row(v):
    return v.reshape(1, -1).astype(F32)


def _pad_cols(w, n):
    return jnp.pad(w, ((0, 0), (0, n - w.shape[1])))


def _pad_rows(w, n):
    return jnp.pad(w, ((0, n - w.shape[0]), (0, 0)))


def _mla_layer(x, batch, seq, g_pre, g_post, w_in, g_q, g_kv, w_uq, w_ukv, w_o):
    cosw, sinw = _rope_tables(seq)
    w_in_pad = _pad_cols(w_in, Q_LORA + KV_LORA + LANES).astype(BF16)
    w_uq_pad = jnp.pad(w_uq.reshape(Q_LORA, MLA_HEADS, NOPE_DIM + ROPE_DIM),
                       ((0, 0), (0, 0), (0, QK_PAD - NOPE_DIM - ROPE_DIM))).reshape(Q_LORA, MLA_HEADS * QK_PAD)
    cq, ckv, kpe = _mla_in(x, _row(g_pre), w_in_pad, _row(g_q), _row(g_kv), cosw, sinw, seq)
    qt = _mla_q(cq, w_uq_pad.astype(BF16), cosw, sinw, seq)
    k, vt = _mla_kv(ckv, kpe, w_ukv.astype(BF16))
    o = _attention(qt, k, vt, batch, seq)
    return _proj_res(o, w_o.astype(BF16), x, _row(g_post), "mla_out")


def _rwkv_layer(x, batch, seq, g_pre, g_post, mu, w_r, w_k, w_v, w_o, w0, w1, w2, a0, a1, a2, g1, g2,
                k_k, k_a, r_k, lnx_g, lnx_b):
    d = x.shape[1]
    mu8 = jnp.pad(mu, ((0, 8 - mu.shape[0]), (0, 0)))
    xr, xw, xk, xv, xa, xg = _rwkv_mix(x, _row(g_pre), mu8, seq)
    r = _matmul(xr, w_r.astype(BF16), F32, "rwkv_r")
    k = _matmul(xk, w_k.astype(BF16), F32, "rwkv_k")
    v = _matmul(xv, w_v.astype(BF16), F32, "rwkv_v")
    zero = jnp.zeros((1, d), F32)
    g = _lora(xg, g1.astype(BF16), g2.astype(BF16), zero, "sigmoid", "none", BF16, "rwkv_gate")
    lws, avs = [], []
    for di in range(2):
        w1p = _pad_cols(w1[di], LANES).astype(BF16)
        w2p = _pad_rows(w2[di], LANES).astype(BF16)
        a1p = _pad_cols(a1[di], LANES).astype(BF16)
        a2p = _pad_rows(a2[di], LANES).astype(BF16)
        lws.append(_lora(xw, w1p, w2p, _row(w0[di]), "tanh", "logdecay", F32, "rwkv_decay%d" % di))
        avs.append(_lora(xa, a1p, a2p, _row(a0[di]), "none", "sigmoid", F32, "rwkv_a%d" % di))
    kk, b0, b1, kd0, kd1, bonus = _wkv_prep(r, k, v, avs[0], avs[1], _row(k_k), _row(k_a), _row(r_k))
    y0, y1 = _wkv2(r, v, kk, kd0, b0, lws[0], kd1, b1, lws[1], batch, seq)
    yg = _wkv_post(y0, y1, bonus, g, _row(lnx_g), _row(lnx_b))
    return _proj_res(yg, w_o.astype(BF16), x, _row(g_post), "rwkv_out")


def _trunk(x3, norm_g, mla_w_in, mla_g_q, mla_g_kv, mla_w_uq, mla_w_ukv, mla_w_o,
           rwkv_mu, rwkv_w_r, rwkv_w_k, rwkv_w_v, rwkv_w_o, rwkv_w0, rwkv_w1, rwkv_w2,
           rwkv_a0, rwkv_a1, rwkv_a2, rwkv_g1, rwkv_g2, rwkv_k_k, rwkv_k_a, rwkv_r_k,
           rwkv_lnx_g, rwkv_lnx_b, ffn_w_gu, ffn_w_down):
    batch, seq, d = x3.shape
    x = x3.reshape(batch * seq, d)
    depth = norm_g.shape[0]
    for i in range(depth):
        j = i // 2
        if i % 2 == 0:
            x = _mla_layer(x, batch, seq, norm_g[i, 0], norm_g[i, 1], mla_w_in[j], mla_g_q[j], mla_g_kv[j],
                           mla_w_uq[j], mla_w_ukv[j], mla_w_o[j])
        else:
            x = _rwkv_layer(x, batch, seq, norm_g[i, 0], norm_g[i, 1], rwkv_mu[j], rwkv_w_r[j], rwkv_w_k[j],
                            rwkv_w_v[j], rwkv_w_o[j], rwkv_w0[j], rwkv_w1[j], rwkv_w2[j], rwkv_a0[j],
                            rwkv_a1[j], rwkv_a2[j], rwkv_g1[j], rwkv_g2[j], rwkv_k_k[j], rwkv_k_a[j],
                            rwkv_r_k[j].reshape(-1), rwkv_lnx_g[j], rwkv_lnx_b[j])
        x = _ffn(x, _row(norm_g[i, 2]), ffn_w_gu[i].astype(BF16), ffn_w_down[i].astype(BF16), _row(norm_g[i, 3]))
    return x.reshape(batch, seq, d)


def kernel(x_prompt, x_sample, norm_g, mla_w_in, mla_g_q, mla_g_kv, mla_w_uq, mla_w_ukv, mla_w_o, rwkv_mu, rwkv_w_r, rwkv_w_k, rwkv_w_v, rwkv_w_o, rwkv_w0, rwkv_w1, rwkv_w2, rwkv_a0, rwkv_a1, rwkv_a2, rwkv_g1, rwkv_g2, rwkv_k_k, rwkv_k_a, rwkv_r_k, rwkv_lnx_g, rwkv_lnx_b, ffn_w_gu, ffn_w_down):
    params = (norm_g, mla_w_in, mla_g_q, mla_g_kv, mla_w_uq, mla_w_ukv, mla_w_o,
              rwkv_mu, rwkv_w_r, rwkv_w_k, rwkv_w_v, rwkv_w_o, rwkv_w0, rwkv_w1, rwkv_w2,
              rwkv_a0, rwkv_a1, rwkv_a2, rwkv_g1, rwkv_g2, rwkv_k_k, rwkv_k_a, rwkv_r_k,
              rwkv_lnx_g, rwkv_lnx_b, ffn_w_gu, ffn_w_down)
    return (_trunk(x_prompt, *params), _trunk(x_sample, *params))
```

```python
import functools
import math

import jax
import jax.numpy as jnp
from jax import lax
from jax.experimental import pallas as pl
from jax.experimental.pallas import tpu as pltpu

F32 = jnp.float32
BF16 = jnp.bfloat16

NORM_EPS = 1e-6
LNX_EPS = 64e-5
ROPE_THETA = 10000.0

MLA_HEADS = 16
Q_LORA = 512
KV_LORA = 512
NOPE_DIM = 128
ROPE_DIM = 64
V_DIM = 128
QK_PAD = 256
ONES_ROWS = 16
RWKV_HEAD = 64

LANES = 128
VMEM_LIMIT = 56 * 1024 * 1024

WKV_CHUNK = 64
WKV_GROUP = 4
WKV_BLOCK = 512

NT = (((1,), (1,)), ((), ()))
TN = (((0,), (0,)), ((), ()))


def _params(*sem):
    return pltpu.CompilerParams(dimension_semantics=sem, vmem_limit_bytes=VMEM_LIMIT)


def _mm(a, b):
    return jnp.dot(a.astype(BF16), b.astype(BF16), preferred_element_type=F32)


def _rms(x, g):
    return x * lax.rsqrt(jnp.mean(x * x, axis=-1, keepdims=True) + NORM_EPS) * g


def _sigmoid(z):
    return 1.0 / (1.0 + jnp.exp(-z))


def _rope_upper(up, cosw, sinw):
    lane = lax.broadcasted_iota(jnp.int32, up.shape, 1)
    swapped = jnp.where(lane < ROPE_DIM // 2, pltpu.roll(up, LANES - ROPE_DIM // 2, 1),
                        pltpu.roll(up, ROPE_DIM // 2, 1))
    return up * cosw + swapped * sinw


def _head_ones(n):
    r = lax.broadcasted_iota(jnp.int32, (n, n), 0) // RWKV_HEAD
    c = lax.broadcasted_iota(jnp.int32, (n, n), 1) // RWKV_HEAD
    return jnp.where(r == c, 1.0, 0.0).astype(BF16)


def _head_sum(z, ones_bd):
    hi = z.astype(BF16)
    lo = (z - hi.astype(F32)).astype(BF16)
    return (jnp.dot(hi, ones_bd, preferred_element_type=F32)
            + jnp.dot(lo, ones_bd, preferred_element_type=F32))


def _mla_in_kernel(x_ref, g_ref, w_ref, gq_ref, gkv_ref, cos_ref, sin_ref, cq_ref, ckv_ref, kpe_ref):
    xn = _rms(x_ref[...], g_ref[...]).astype(BF16)
    h = jnp.dot(xn, w_ref[...], preferred_element_type=F32)
    cq_ref[...] = _rms(h[:, :Q_LORA], gq_ref[...]).astype(BF16)
    ckv_ref[...] = _rms(h[:, Q_LORA:Q_LORA + KV_LORA], gkv_ref[...]).astype(BF16)
    kpe_ref[...] = _rope_upper(h[:, Q_LORA + KV_LORA:], cos_ref[...], sin_ref[...]).astype(BF16)


def _mla_in(x, g, w_in_pad, g_q, g_kv, cosw, sinw, seq):
    m, d = x.shape
    tm = min(512, seq)
    nseq = seq // tm
    n = w_in_pad.shape[1]
    row = lambda i: (i, 0)
    fix = lambda i: (0, 0)
    tab = lambda i: (i % nseq, 0)
    return pl.pallas_call(
        _mla_in_kernel,
        grid=(m // tm,),
        in_specs=[pl.BlockSpec((tm, d), row), pl.BlockSpec((1, d), fix), pl.BlockSpec((d, n), fix),
                  pl.BlockSpec((1, Q_LORA), fix), pl.BlockSpec((1, KV_LORA), fix),
                  pl.BlockSpec((tm, LANES), tab), pl.BlockSpec((tm, LANES), tab)],
        out_specs=[pl.BlockSpec((tm, Q_LORA), row), pl.BlockSpec((tm, KV_LORA), row),
                   pl.BlockSpec((tm, LANES), row)],
        out_shape=[jax.ShapeDtypeStruct((m, Q_LORA), BF16), jax.ShapeDtypeStruct((m, KV_LORA), BF16),
                   jax.ShapeDtypeStruct((m, LANES), BF16)],
        compiler_params=_params("parallel"),
        name="mla_in",
    )(x, g, w_in_pad, g_q, g_kv, cosw, sinw)


def _mla_q_kernel(cq_ref, w_ref, cos_ref, sin_ref, q_ref, *, scale):
    res = jnp.dot(cq_ref[...], w_ref[...], preferred_element_type=F32)
    cosw = cos_ref[...]
    sinw = sin_ref[...]
    for h in range(MLA_HEADS):
        lo = res[:, h * QK_PAD:h * QK_PAD + NOPE_DIM]
        up = _rope_upper(res[:, h * QK_PAD + NOPE_DIM:(h + 1) * QK_PAD], cosw, sinw)
        q_ref[h * QK_PAD:h * QK_PAD + NOPE_DIM, :] = (lo * scale).T.astype(BF16)
        q_ref[h * QK_PAD + NOPE_DIM:(h + 1) * QK_PAD, :] = (up * scale).T.astype(BF16)


def _mla_q(cq, w_uq_pad, cosw, sinw, seq):
    m, c = cq.shape
    n = w_uq_pad.shape[1]
    tm = min(256, seq)
    nseq = seq // tm
    scale = (NOPE_DIM + ROPE_DIM) ** -0.5 * math.log2(math.e)
    return pl.pallas_call(
        functools.partial(_mla_q_kernel, scale=scale),
        grid=(m // tm,),
        in_specs=[pl.BlockSpec((tm, c), lambda i: (i, 0)), pl.BlockSpec((c, n), lambda i: (0, 0)),
                  pl.BlockSpec((tm, LANES), lambda i: (i % nseq, 0)),
                  pl.BlockSpec((tm, LANES), lambda i: (i % nseq, 0))],
        out_specs=pl.BlockSpec((n, tm), lambda i: (0, i)),
        out_shape=jax.ShapeDtypeStruct((n, m), BF16),
        compiler_params=_params("parallel"),
        name="mla_q",
    )(cq, w_uq_pad, cosw, sinw)


def _mla_kv_kernel(ckv_ref, kpe_ref, w_ref, k_ref, vt_ref):
    res = jnp.dot(ckv_ref[...], w_ref[...], preferred_element_type=F32)
    kpe = kpe_ref[...]
    for h in range(MLA_HEADS):
        k_ref[h, :, :NOPE_DIM] = res[:, 2 * h * NOPE_DIM:(2 * h + 1) * NOPE_DIM].astype(BF16)
        k_ref[h, :, NOPE_DIM:] = kpe
        vt_ref[h * V_DIM:(h + 1) * V_DIM, :] = res[:, (2 * h + 1) * V_DIM:(2 * h + 2) * V_DIM].T.astype(BF16)


def _mla_kv(ckv, kpe, w_ukv):
    m, c = ckv.shape
    n = w_ukv.shape[1]
    tm = min(256, m)
    return pl.pallas_call(
        _mla_kv_kernel,
        grid=(m // tm,),
        in_specs=[pl.BlockSpec((tm, c), lambda i: (i, 0)), pl.BlockSpec((tm, LANES), lambda i: (i, 0)),
                  pl.BlockSpec((c, n), lambda i: (0, 0))],
        out_specs=[pl.BlockSpec((MLA_HEADS, tm, QK_PAD), lambda i: (0, i, 0)),
                   pl.BlockSpec((n // 2, tm), lambda i: (0, i))],
        out_shape=[jax.ShapeDtypeStruct((MLA_HEADS, m, QK_PAD), BF16), jax.ShapeDtypeStruct((n // 2, m), BF16)],
        compiler_params=_params("parallel"),
        name="mla_kv",
    )(ckv, kpe, w_ukv)


def _matmul_kernel(a_ref, w_ref, o_ref):
    o_ref[...] = jnp.dot(a_ref[...], w_ref[...], preferred_element_type=F32).astype(o_ref.dtype)


def _matmul(a, w, out_dtype, name):
    m, k = a.shape
    n = w.shape[1]
    tm = min(512, m)
    tn = min(1024, n)
    return pl.pallas_call(
        _matmul_kernel,
        grid=(m // tm, n // tn),
        in_specs=[pl.BlockSpec((tm, k), lambda i, j: (i, 0)), pl.BlockSpec((k, tn), lambda i, j: (0, j))],
        out_specs=pl.BlockSpec((tm, tn), lambda i, j: (i, j)),
        out_shape=jax.ShapeDtypeStruct((m, n), out_dtype),
        compiler_params=_params("parallel", "parallel"),
        name=name,
    )(a, w)


def _attn_kernel(qt_ref, k0_ref, k_ref, vt_ref, o_ref, sa_sc, sb_sc, xa_sc, xb_sc, m_sc, acc_sc, *, n_split, n_kv):
    j = pl.program_id(3)
    width = qt_ref.shape[1] // n_split

    def score(k, dst_sc, dst_mx, cols):
        s = jnp.dot(k, qt_ref[:, cols], preferred_element_type=F32)
        dst_sc[:, cols] = s
        dst_mx[:, cols] = jnp.max(s, axis=0, keepdims=True)

    @pl.when(j == 0)
    def _():
        m_sc[...] = jnp.full(m_sc.shape, -jnp.inf, F32)
        acc_sc[...] = jnp.zeros(acc_sc.shape, F32)
        k0 = k0_ref[0]
        for c in range(n_split):
            score(k0, sa_sc, xa_sc, slice(c * width, (c + 1) * width))

    def step(cur, nxt):
        k = k_ref[0]
        vt = vt_ref[...]
        vt1 = jnp.concatenate([vt, jnp.ones((ONES_ROWS, vt.shape[1]), BF16)], axis=0)
        for c in range(n_split):
            cols = slice(c * width, (c + 1) * width)
            if nxt is not None:
                score(k, nxt[0], nxt[1], cols)
            m_prev = m_sc[:, cols]
            m_new = jnp.maximum(m_prev, cur[1][:, cols])
            alpha = jnp.exp2(m_prev - m_new)
            pt = jnp.exp2((cur[0][:, cols] - m_new).astype(BF16))
            acc_sc[:, cols] = alpha * acc_sc[:, cols] + jnp.dot(vt1, pt, preferred_element_type=F32)
            m_sc[:, cols] = m_new

    last = n_kv - 1
    buf_a = (sa_sc, xa_sc)
    buf_b = (sb_sc, xb_sc)

    @pl.when((j % 2 == 0) & (j < last))
    def _():
        step(buf_a, buf_b)

    @pl.when((j % 2 == 1) & (j < last))
    def _():
        step(buf_b, buf_a)

    @pl.when(j == last)
    def _():
        step(buf_a if last % 2 == 0 else buf_b, None)
        o_ref[...] = (acc_sc[:V_DIM, :] / acc_sc[V_DIM:V_DIM + 1, :]).T.astype(o_ref.dtype)


def _attention(qt, k, vt, batch, seq):
    m = k.shape[1]
    tq = min(2048, seq)
    tk = min(1024, seq)
    nq = seq // tq
    nk = seq // tk
    return pl.pallas_call(
        functools.partial(_attn_kernel, n_split=max(tq // 512, 1), n_kv=nk),
        grid=(batch, MLA_HEADS, nq, nk),
        in_specs=[pl.BlockSpec((QK_PAD, tq), lambda b, h, i, j: (h, b * nq + i)),
                  pl.BlockSpec((1, tk, QK_PAD), lambda b, h, i, j: (h, b * nk, 0)),
                  pl.BlockSpec((1, tk, QK_PAD), lambda b, h, i, j: (h, b * nk + jnp.minimum(j + 1, nk - 1), 0)),
                  pl.BlockSpec((V_DIM, tk), lambda b, h, i, j: (h, b * nk + j))],
        out_specs=pl.BlockSpec((tq, V_DIM), lambda b, h, i, j: (b * nq + i, h)),
        out_shape=jax.ShapeDtypeStruct((m, MLA_HEADS * V_DIM), BF16),
        scratch_shapes=[pltpu.VMEM((tk, tq), F32), pltpu.VMEM((tk, tq), F32),
                        pltpu.VMEM((1, tq), F32), pltpu.VMEM((1, tq), F32),
                        pltpu.VMEM((1, tq), F32), pltpu.VMEM((V_DIM + ONES_ROWS, tq), F32)],
        compiler_params=_params("parallel", "parallel", "parallel", "arbitrary"),
        name="mla_attention",
    )(qt, k, k, vt)


def _proj_res_kernel(a_ref, w_ref, x_ref, g_ref, o_ref):
    h = jnp.dot(a_ref[...], w_ref[...], preferred_element_type=F32)
    o_ref[...] = x_ref[...] + _rms(h, g_ref[...])


def _proj_res(a, w, x, g, name):
    m, kdim = a.shape
    n = w.shape[1]
    tm = min(512, m)
    return pl.pallas_call(
        _proj_res_kernel,
        grid=(m // tm,),
        in_specs=[pl.BlockSpec((tm, kdim), lambda i: (i, 0)), pl.BlockSpec((kdim, n), lambda i: (0, 0)),
                  pl.BlockSpec((tm, n), lambda i: (i, 0)), pl.BlockSpec((1, n), lambda i: (0, 0))],
        out_specs=pl.BlockSpec((tm, n), lambda i: (i, 0)),
        out_shape=jax.ShapeDtypeStruct((m, n), F32),
        compiler_params=_params("parallel"),
        name=name,
    )(a, w, x, g)


def _ffn_kernel(x_ref, g_pre_ref, wg_ref, wu_ref, wd_ref, g_post_ref, o_ref, xn_sc, acc_sc):
    f = pl.program_id(1)

    @pl.when(f == 0)
    def _():
        xn_sc[...] = _rms(x_ref[...], g_pre_ref[...]).astype(BF16)
        acc_sc[...] = jnp.zeros(acc_sc.shape, F32)

    xn = xn_sc[...]
    gate = jnp.dot(xn, wg_ref[...], preferred_element_type=F32)
    up = jnp.dot(xn, wu_ref[...], preferred_element_type=F32)
    act = (gate * _sigmoid(gate) * up).astype(BF16)
    acc_sc[...] += jnp.dot(act, wd_ref[...], preferred_element_type=F32)

    @pl.when(f == pl.num_programs(1) - 1)
    def _():
        o_ref[...] = x_ref[...] + _rms(acc_sc[...], g_post_ref[...])


def _ffn(x, g_pre, w_gu, w_down, g_post):
    m, d = x.shape
    d_ff = w_down.shape[0]
    tm = min(512, m)
    tf = 512
    nf = d_ff // tf
    return pl.pallas_call(
        _ffn_kernel,
        grid=(m // tm, nf),
        in_specs=[pl.BlockSpec((tm, d), lambda i, f: (i, 0)), pl.BlockSpec((1, d), lambda i, f: (0, 0)),
                  pl.BlockSpec((d, tf), lambda i, f: (0, f)), pl.BlockSpec((d, tf), lambda i, f: (0, f + nf)),
                  pl.BlockSpec((tf, d), lambda i, f: (f, 0)), pl.BlockSpec((1, d), lambda i, f: (0, 0))],
        out_specs=pl.BlockSpec((tm, d), lambda i, f: (i, 0)),
        out_shape=jax.ShapeDtypeStruct((m, d), F32),
        scratch_shapes=[pltpu.VMEM((tm, d), BF16), pltpu.VMEM((tm, d), F32)],
        compiler_params=_params("parallel", "arbitrary"),
        name="ffn",
    )(x, g_pre, w_gu, w_gu, w_down, g_post)


def _rwkv_mix_kernel(x_ref, xp_ref, xn_ref, g_ref, mu_ref, xr_ref, xw_ref, xk_ref, xv_ref, xa_ref, xg_ref,
                     *, tiles_per_seq):
    i = pl.program_id(0)
    g = g_ref[...]
    h = _rms(x_ref[...], g)
    tm = h.shape[0]
    first = (i % tiles_per_seq) == 0
    last = (i % tiles_per_seq) == tiles_per_seq - 1
    hp = jnp.where(first, 0.0, _rms(xp_ref[7:8, :], g))
    hn = jnp.where(last, 0.0, _rms(xn_ref[0:1, :], g))
    row = lax.broadcasted_iota(jnp.int32, h.shape, 0)
    h_prev = jnp.where(row == 0, hp, pltpu.roll(h, 1, 0))
    h_next = jnp.where(row == tm - 1, hn, pltpu.roll(h, tm - 1, 0))
    xx = 0.5 * (h_prev + h_next) - h
    for idx, ref in enumerate((xr_ref, xw_ref, xk_ref, xv_ref, xa_ref, xg_ref)):
        ref[...] = (h + xx * mu_ref[idx:idx + 1, :]).astype(BF16)


def _rwkv_mix(x, g, mu, seq):
    m, d = x.shape
    tm = min(256, seq)
    tps = seq // tm
    nb8 = m // 8
    r8 = tm // 8
    out = jax.ShapeDtypeStruct((m, d), BF16)
    row = lambda i: (i, 0)
    return pl.pallas_call(
        functools.partial(_rwkv_mix_kernel, tiles_per_seq=tps),
        grid=(m // tm,),
        in_specs=[pl.BlockSpec((tm, d), row),
                  pl.BlockSpec((8, d), lambda i: (jnp.maximum(i * r8 - 1, 0), 0)),
                  pl.BlockSpec((8, d), lambda i: (jnp.minimum((i + 1) * r8, nb8 - 1), 0)),
                  pl.BlockSpec((1, d), lambda i: (0, 0)), pl.BlockSpec((8, d), lambda i: (0, 0))],
        out_specs=[pl.BlockSpec((tm, d), row)] * 6,
        out_shape=[out] * 6,
        compiler_params=_params("parallel"),
        name="rwkv_mix",
    )(x, x, x, g, mu)


def _lora_kernel(x_ref, a_ref, b_ref, bias_ref, o_ref, *, mid, post):
    t = jnp.dot(x_ref[...], a_ref[...], preferred_element_type=F32)
    if mid == "tanh":
        t = jnp.tanh(t)
    elif mid == "sigmoid":
        t = _sigmoid(t)
    z = jnp.dot(t.astype(BF16), b_ref[...], preferred_element_type=F32) + bias_ref[...]
    if post == "sigmoid":
        z = _sigmoid(z)
    elif post == "logdecay":
        z = -math.exp(-0.5) * _sigmoid(z)
    o_ref[...] = z.astype(o_ref.dtype)


def _lora(x, a, b, bias, mid, post, out_dtype, name):
    m, d = x.shape
    r = a.shape[1]
    n = b.shape[1]
    tm = min(512, m)
    return pl.pallas_call(
        functools.partial(_lora_kernel, mid=mid, post=post),
        grid=(m // tm,),
        in_specs=[pl.BlockSpec((tm, d), lambda i: (i, 0)), pl.BlockSpec((d, r), lambda i: (0, 0)),
                  pl.BlockSpec((r, n), lambda i: (0, 0)), pl.BlockSpec((1, n), lambda i: (0, 0))],
        out_specs=pl.BlockSpec((tm, n), lambda i: (i, 0)),
        out_shape=jax.ShapeDtypeStruct((m, n), out_dtype),
        compiler_params=_params("parallel"),
        name=name,
    )(x, a, b, bias)


def _wkv_pre(r_ref, k_ref, v_ref, a_ref, lw_ref, k_k, k_a, ones_bd, blk, head_mask, gl, *, reverse):
    t_len = WKV_CHUNK
    ng = gl // RWKV_HEAD
    gt = ng * t_len
    n_chunks = r_ref.shape[0] // t_len

    wrow = lax.broadcasted_iota(jnp.int32, (t_len, gt), 0)
    wcol = lax.broadcasted_iota(jnp.int32, (t_len, gt), 1) % t_len
    strict = (wcol > wrow) if reverse else (wcol < wrow)
    incl = (wcol >= wrow) if reverse else (wcol <= wrow)
    eye_w = jnp.where(wcol == wrow, 1.0, 0.0)
    trow = lax.broadcasted_iota(jnp.int32, (t_len, t_len), 0)
    tcol = lax.broadcasted_iota(jnp.int32, (t_len, t_len), 1)
    tri = jnp.where((tcol >= trow) if reverse else (tcol <= trow), 1.0, 0.0).astype(BF16)
    bd_mask = jnp.where(lax.broadcasted_iota(jnp.int32, (gt, gt), 0) // t_len
                        == lax.broadcasted_iota(jnp.int32, (gt, gt), 1) // t_len, 1.0, 0.0).astype(BF16)

    def bdiag(w):
        return jnp.concatenate([w.astype(BF16)] * ng, axis=0) * bd_mask

    def mmb(a, b_bf16):
        return jnp.dot(a.astype(BF16), b_bf16, preferred_element_type=F32)

    items = []
    for ci in range(n_chunks):
        c = (n_chunks - 1 - ci) if reverse else ci
        items.append({"rows": slice(c * t_len, (c + 1) * t_len)})

    for it in items:
        lw = lw_ref[it["rows"], :]
        lw_hi = lw.astype(BF16)
        lw_lo = (lw - lw_hi.astype(F32)).astype(BF16)
        it["lw"] = lw
        cum2 = jnp.dot(tri, jnp.concatenate([lw_hi, lw_lo], axis=1), preferred_element_type=F32)
        it["cum"] = cum2[:, :gl] + cum2[:, gl:]

    kkr_all = k_ref[...] * k_k
    nrm_all = jnp.maximum(jnp.sqrt(jnp.dot((kkr_all * kkr_all).astype(BF16), ones_bd,
                                           preferred_element_type=F32)), 1e-12)
    kk_all = kkr_all / nrm_all

    for it in items:
        rows, cum, lw = it["rows"], it["cum"], it["lw"]
        a = a_ref[rows, :]
        kk = kk_all[rows]
        kd = k_ref[rows, :] * (1.0 + (a - 1.0) * k_a)
        b = kk * a
        tot = cum[0:1] if reverse else cum[t_len - 1:t_len]
        inv_p = jnp.exp(-cum)
        rt = r_ref[rows, :] * jnp.exp(cum)
        kkt = kk * jnp.exp(cum - lw)
        to_end = jnp.exp(tot - cum)
        it["decay"] = jnp.exp(tot)
        it["b_end"] = (b * to_end).astype(BF16)
        it["k_end"] = (kd * to_end).astype(BF16)
        it["rt"] = rt.astype(BF16)
        it["kkt_blk"] = blk(kkt)
        lhs = jnp.concatenate([kkt, rt], axis=0).astype(BF16)
        rhs = jnp.concatenate([blk(b * inv_p), blk(kd * inv_p)], axis=0)
        aa = lax.dot_general(lhs, rhs, NT, preferred_element_type=F32)
        it["a_ab"] = jnp.where(strict, aa[:t_len, :gt], 0.0)
        it["a_ak"] = jnp.where(strict, aa[:t_len, gt:], 0.0).astype(BF16)
        it["a_r"] = jnp.concatenate([jnp.where(incl, aa[t_len:, :gt], 0.0),
                                     jnp.where(incl, aa[t_len:, gt:], 0.0)], axis=1).astype(BF16)

    for it in items:
        it["x"] = eye_w - it["a_ab"]
        it["p"] = mmb(it["a_ab"], bdiag(it["a_ab"]))
    rounds = int(math.log2(t_len)) - 1
    for rnd in range(rounds):
        for it in items:
            p_bd = bdiag(it["p"])
            if rnd < rounds - 1:
                xp = mmb(jnp.concatenate([it["x"], it["p"]], axis=0), p_bd)
                it["x"] = it["x"] + xp[:t_len]
                it["p"] = xp[t_len:]
            else:
                it["x"] = it["x"] + mmb(it["x"], p_bd)

    for it in items:
        v = v_ref[it["rows"], :]
        it["v"] = v
        it["v_blk"] = blk(v)
        it["akv"] = jnp.dot(it["a_ak"], it["v_blk"], preferred_element_type=F32)
        it["x_b"] = it["x"].astype(BF16)
        it["w_k"] = jnp.dot(it["x_b"], it["kkt_blk"], preferred_element_type=F32)
    for it in items:
        it["u"] = jnp.dot(it["x_b"], blk(it["akv"]), preferred_element_type=F32)
    for it in items:
        m_full = lax.dot_general(it["w_k"].astype(BF16), it["b_end"], TN, preferred_element_type=F32)
        it["m"] = jnp.where(head_mask, m_full, 0.0).astype(BF16)
    for it in items:
        c_full = lax.dot_general(jnp.concatenate([-it["u"], it["v"]], axis=0).astype(BF16),
                                 jnp.concatenate([it["b_end"], it["k_end"]], axis=0),
                                 TN, preferred_element_type=F32)
        it["c"] = jnp.where(head_mask, c_full, 0.0)
        it["wr"] = jnp.concatenate([it["w_k"].astype(BF16), it["rt"]], axis=0)
    return items


def _wkv2_kernel(rf_ref, kf_ref, vf_ref, a0f_ref, lwf_ref, a1f_ref, rb_ref, kb_ref, vb_ref, a1b_ref, lwb_ref,
                 kk_par, ka_par, rk_par, yf_ref, yb_ref, bonus_ref, stf_ref, stb_ref):
    @pl.when(pl.program_id(2) == 0)
    def _():
        stf_ref[...] = jnp.zeros(stf_ref.shape, F32)
        stb_ref[...] = jnp.zeros(stb_ref.shape, F32)

    gl = stf_ref.shape[0]
    gt = (gl // RWKV_HEAD) * WKV_CHUNK
    ones_bd = _head_ones(gl)
    k_k = kk_par[...]
    k_a = ka_par[...]

    kf = kf_ref[...]
    kd_sum = kf * (2.0 + (a0f_ref[...] + a1f_ref[...] - 2.0) * k_a)
    bonus_ref[...] = jnp.dot((rf_ref[...] * kd_sum * rk_par[...]).astype(BF16), ones_bd,
                             preferred_element_type=F32) * vf_ref[...]

    blk_mask = jnp.where(lax.broadcasted_iota(jnp.int32, (gt, gl), 0) // WKV_CHUNK
                         == lax.broadcasted_iota(jnp.int32, (gt, gl), 1) // RWKV_HEAD, 1.0, 0.0).astype(BF16)
    head_mask = (lax.broadcasted_iota(jnp.int32, (gl, gl), 0) // RWKV_HEAD
                 == lax.broadcasted_iota(jnp.int32, (gl, gl), 1) // RWKV_HEAD)

    def blk(x):
        return jnp.concatenate([x.astype(BF16)] * (gl // RWKV_HEAD), axis=0) * blk_mask

    t_len = WKV_CHUNK
    chains = [
        (_wkv_pre(rf_ref, kf_ref, vf_ref, a0f_ref, lwf_ref, k_k, k_a, ones_bd, blk, head_mask, gl, reverse=False),
         yf_ref, stf_ref),
        (_wkv_pre(rb_ref, kb_ref, vb_ref, a1b_ref, lwb_ref, k_k, k_a, ones_bd, blk, head_mask, gl, reverse=True),
         yb_ref, stb_ref),
    ]
    states = [st_ref[...] for _, _, st_ref in chains]

    def emit_y(y_ref, it, sa, rs):
        y_ref[it["rows"], :] = rs + jnp.dot(it["a_r"], jnp.concatenate([blk(sa), it["v_blk"]], axis=0),
                                            preferred_element_type=F32)

    pending = []
    for ci in range(len(chains[0][0])):
        st_bf = [st.astype(BF16) for st in states]
        its = [items[ci] for items, _, _ in chains]
        st_m = [jnp.dot(sb, it["m"], preferred_element_type=F32) for sb, it in zip(st_bf, its)]
        ws = [lax.dot_general(it["wr"], sb, NT, preferred_element_type=F32) for sb, it in zip(st_bf, its)]
        for args in pending:
            emit_y(*args)
        pending = [(y_ref, it, -(w[:t_len] + it["u"]), w[t_len:])
                   for (_, y_ref, _), it, w in zip(chains, its, ws)]
        states = [st * it["decay"] - sm + it["c"] for st, it, sm in zip(states, its, st_m)]
    for args in pending:
        emit_y(*args)
    for (_, _, st_ref), st in zip(chains, states):
        st_ref[...] = st


def _wkv2(r, k, v, a0, lw0, a1, lw1, k_k, k_a, r_k, batch, seq):
    m, d = r.shape
    gl = WKV_GROUP * RWKV_HEAD
    tb = min(WKV_BLOCK, seq)
    nb = seq // tb
    fwd = pl.BlockSpec((tb, gl), lambda bi, g, j: (bi * nb + j, g))
    bwd = pl.BlockSpec((tb, gl), lambda bi, g, j: (bi * nb + nb - 1 - j, g))
    par = pl.BlockSpec((1, gl), lambda bi, g, j: (0, g))
    out = jax.ShapeDtypeStruct((m, d), F32)
    return pl.pallas_call(
        _wkv2_kernel,
        grid=(batch, d // gl, nb),
        in_specs=[fwd] * 6 + [bwd] * 5 + [par] * 3,
        out_specs=[fwd, bwd, fwd],
        out_shape=[out, out, out],
        scratch_shapes=[pltpu.VMEM((gl, gl), F32), pltpu.VMEM((gl, gl), F32)],
        compiler_params=_params("parallel", "parallel", "arbitrary"),
        name="wkv",
    )(r, k, v, a0, lw0, a1, r, k, v, a1, lw1, k_k, k_a, r_k)


def _wkv_post_kernel(y0_ref, y1_ref, bonus_ref, g_ref, lg_ref, lb_ref, o_ref):
    ones_bd = _head_ones(LANES)
    d = y0_ref.shape[1]
    inv_n = 1.0 / RWKV_HEAD
    for c in range(d // LANES):
        sl = slice(c * LANES, (c + 1) * LANES)
        y = y0_ref[:, sl] + y1_ref[:, sl]
        mean = _head_sum(y, ones_bd) * inv_n
        yc = y - mean
        var = _head_sum(yc * yc, ones_bd) * inv_n
        yn = yc * lax.rsqrt(var + LNX_EPS) * lg_ref[:, sl] + lb_ref[:, sl]
        o_ref[:, sl] = ((yn + bonus_ref[:, sl]) * g_ref[:, sl]).astype(o_ref.dtype)


def _wkv_post(y0, y1, bonus, g, lnx_g, lnx_b):
    m, d = y0.shape
    tm = min(256, m)
    row = pl.BlockSpec((tm, d), lambda i: (i, 0))
    par = pl.BlockSpec((1, d), lambda i: (0, 0))
    return pl.pallas_call(
        _wkv_post_kernel,
        grid=(m // tm,),
        in_specs=[row] * 4 + [par] * 2,
        out_specs=row,
        out_shape=jax.ShapeDtypeStruct((m, d), BF16),
        compiler_params=_params("parallel"),
        name="wkv_post",
    )(y0, y1, bonus, g, lnx_g, lnx_b)


def _rope_tables(seq):
    half = ROPE_DIM // 2
    inv = 1.0 / (ROPE_THETA ** (jnp.arange(half, dtype=F32) * (2.0 / ROPE_DIM)))
    ang = jnp.arange(seq, dtype=F32)[:, None] * inv[None, :]
    cos, sin = jnp.cos(ang), jnp.sin(ang)
    zero = jnp.zeros((seq, LANES - ROPE_DIM), F32)
    return (jnp.concatenate([cos, cos, zero], axis=1), jnp.concatenate([-sin, sin, zero], axis=1))


def _row(v):
    return v.reshape(1, -1).astype(F32)


def _pad_cols(w, n):
    return jnp.pad(w, ((0, 0), (0, n - w.shape[1])))


def _pad_rows(w, n):
    return jnp.pad(w, ((0, n - w.shape[0]), (0, 0)))


def _mla_layer(x, batch, seq, g_pre, g_post, w_in, g_q, g_kv, w_uq, w_ukv, w_o):
    cosw, sinw = _rope_tables(seq)
    w_in_pad = _pad_cols(w_in, Q_LORA + KV_LORA + LANES).astype(BF16)
    w_uq_pad = jnp.pad(w_uq.reshape(Q_LORA, MLA_HEADS, NOPE_DIM + ROPE_DIM),
                       ((0, 0), (0, 0), (0, QK_PAD - NOPE_DIM - ROPE_DIM))).reshape(Q_LORA, MLA_HEADS * QK_PAD)
    cq, ckv, kpe = _mla_in(x, _row(g_pre), w_in_pad, _row(g_q), _row(g_kv), cosw, sinw, seq)
    qt = _mla_q(cq, w_uq_pad.astype(BF16), cosw, sinw, seq)
    k, vt = _mla_kv(ckv, kpe, w_ukv.astype(BF16))
    o = _attention(qt, k, vt, batch, seq)
    return _proj_res(o, w_o.astype(BF16), x, _row(g_post), "mla_out")


def _rwkv_layer(x, batch, seq, g_pre, g_post, mu, w_r, w_k, w_v, w_o, w0, w1, w2, a0, a1, a2, g1, g2,
                k_k, k_a, r_k, lnx_g, lnx_b):
    d = x.shape[1]
    mu8 = jnp.pad(mu, ((0, 8 - mu.shape[0]), (0, 0)))
    xr, xw, xk, xv, xa, xg = _rwkv_mix(x, _row(g_pre), mu8, seq)
    r = _matmul(xr, w_r.astype(BF16), F32, "rwkv_r")
    k = _matmul(xk, w_k.astype(BF16), F32, "rwkv_k")
    v = _matmul(xv, w_v.astype(BF16), F32, "rwkv_v")
    zero = jnp.zeros((1, d), F32)
    g = _lora(xg, g1.astype(BF16), g2.astype(BF16), zero, "sigmoid", "none", BF16, "rwkv_gate")
    lws, avs = [], []
    for di in range(2):
        w1p = _pad_cols(w1[di], LANES).astype(BF16)
        w2p = _pad_rows(w2[di], LANES).astype(BF16)
        a1p = _pad_cols(a1[di], LANES).astype(BF16)
        a2p = _pad_rows(a2[di], LANES).astype(BF16)
        lws.append(_lora(xw, w1p, w2p, _row(w0[di]), "tanh", "logdecay", F32, "rwkv_decay%d" % di))
        avs.append(_lora(xa, a1p, a2p, _row(a0[di]), "none", "sigmoid", F32, "rwkv_a%d" % di))
    y0, y1, bonus = _wkv2(r, k, v, avs[0], lws[0], avs[1], lws[1], _row(k_k), _row(k_a), _row(r_k), batch, seq)
    yg = _wkv_post(y0, y1, bonus, g, _row(lnx_g), _row(lnx_b))
    return _proj_res(yg, w_o.astype(BF16), x, _row(g_post), "rwkv_out")


def _trunk(x3, norm_g, mla_w_in, mla_g_q, mla_g_kv, mla_w_uq, mla_w_ukv, mla_w_o,
           rwkv_mu, rwkv_w_r, rwkv_w_k, rwkv_w_v, rwkv_w_o, rwkv_w0, rwkv_w1, rwkv_w2,
           rwkv_a0, rwkv_a1, rwkv_a2, rwkv_g1, rwkv_g2, rwkv_k_k, rwkv_k_a, rwkv_r_k,
           rwkv_lnx_g, rwkv_lnx_b, ffn_w_gu, ffn_w_down):
    batch, seq, d = x3.shape
    x = x3.reshape(batch * seq, d)
    depth = norm_g.shape[0]
    for i in range(depth):
        j = i // 2
        if i % 2 == 0:
            x = _mla_layer(x, batch, seq, norm_g[i, 0], norm_g[i, 1], mla_w_in[j], mla_g_q[j], mla_g_kv[j],
                           mla_w_uq[j], mla_w_ukv[j], mla_w_o[j])
        else:
            x = _rwkv_layer(x, batch, seq, norm_g[i, 0], norm_g[i, 1], rwkv_mu[j], rwkv_w_r[j], rwkv_w_k[j],
                            rwkv_w_v[j], rwkv_w_o[j], rwkv_w0[j], rwkv_w1[j], rwkv_w2[j], rwkv_a0[j],
                            rwkv_a1[j], rwkv_a2[j], rwkv_g1[j], rwkv_g2[j], rwkv_k_k[j], rwkv_k_a[j],
                            rwkv_r_k[j].reshape(-1), rwkv_lnx_g[j], rwkv_lnx_b[j])
        x = _ffn(x, _row(norm_g[i, 2]), ffn_w_gu[i].astype(BF16), ffn_w_down[i].astype(BF16), _row(norm_g[i, 3]))
    return x.reshape(batch, seq, d)


def kernel(x_prompt, x_sample, norm_g, mla_w_in, mla_g_q, mla_g_kv, mla_w_uq, mla_w_ukv, mla_w_o, rwkv_mu, rwkv_w_r, rwkv_w_k, rwkv_w_v, rwkv_w_o, rwkv_w0, rwkv_w1, rwkv_w2, rwkv_a0, rwkv_a1, rwkv_a2, rwkv_g1, rwkv_g2, rwkv_k_k, rwkv_k_a, rwkv_r_k, rwkv_lnx_g, rwkv_lnx_b, ffn_w_gu, ffn_w_down):
    params = (norm_g, mla_w_in, mla_g_q, mla_g_kv, mla_w_uq, mla_w_ukv, mla_w_o,
              rwkv_mu, rwkv_w_r, rwkv_w_k, rwkv_w_v, rwkv_w_o, rwkv_w0, rwkv_w1, rwkv_w2,
              rwkv_a0, rwkv_a1, rwkv_a2, rwkv_g1, rwkv_g2, rwkv_k_k, rwkv_k_a, rwkv_r_k,
              rwkv_lnx_g, rwkv_lnx_b, ffn_w_gu, ffn_w_down)
    return (_trunk(x_prompt, *params), _trunk(x_sample, *params))
```

```python
import functools
import math

import jax
import jax.numpy as jnp
from jax import lax
from jax.experimental import pallas as pl
from jax.experimental.pallas import tpu as pltpu

F32 = jnp.float32
BF16 = jnp.bfloat16

NORM_EPS = 1e-6
LNX_EPS = 64e-5
ROPE_THETA = 10000.0

MLA_HEADS = 16
Q_LORA = 512
KV_LORA = 512
NOPE_DIM = 128
ROPE_DIM = 64
V_DIM = 128
QK_PAD = 256
ONES_ROWS = 16
RWKV_HEAD = 64

LANES = 128
VMEM_LIMIT = 56 * 1024 * 1024

WKV_CHUNK = 64
WKV_GROUP = 4
WKV_BLOCK = 512

NT = (((1,), (1,)), ((), ()))
TN = (((0,), (0,)), ((), ()))


def _params(*sem):
    return pltpu.CompilerParams(dimension_semantics=sem, vmem_limit_bytes=VMEM_LIMIT)


def _mm(a, b):
    return jnp.dot(a.astype(BF16), b.astype(BF16), preferred_element_type=F32)


def _rms(x, g):
    return x * lax.rsqrt(jnp.mean(x * x, axis=-1, keepdims=True) + NORM_EPS) * g


def _sigmoid(z):
    return 1.0 / (1.0 + jnp.exp(-z))


def _rope_upper(up, cosw, sinw):
    lane = lax.broadcasted_iota(jnp.int32, up.shape, 1)
    swapped = jnp.where(lane < ROPE_DIM // 2, pltpu.roll(up, LANES - ROPE_DIM // 2, 1),
                        pltpu.roll(up, ROPE_DIM // 2, 1))
    return up * cosw + swapped * sinw


def _head_ones(n):
    r = lax.broadcasted_iota(jnp.int32, (n, n), 0) // RWKV_HEAD
    c = lax.broadcasted_iota(jnp.int32, (n, n), 1) // RWKV_HEAD
    return jnp.where(r == c, 1.0, 0.0).astype(BF16)


def _head_sum(z, ones_bd):
    hi = z.astype(BF16)
    lo = (z - hi.astype(F32)).astype(BF16)
    return (jnp.dot(hi, ones_bd, preferred_element_type=F32)
            + jnp.dot(lo, ones_bd, preferred_element_type=F32))


def _mla_in_kernel(x_ref, g_ref, w_ref, gq_ref, gkv_ref, cos_ref, sin_ref, cq_ref, ckv_ref, kpe_ref):
    xn = _rms(x_ref[...], g_ref[...]).astype(BF16)
    h = jnp.dot(xn, w_ref[...], preferred_element_type=F32)
    cq_ref[...] = _rms(h[:, :Q_LORA], gq_ref[...]).astype(BF16)
    ckv_ref[...] = _rms(h[:, Q_LORA:Q_LORA + KV_LORA], gkv_ref[...]).astype(BF16)
    kpe_ref[...] = _rope_upper(h[:, Q_LORA + KV_LORA:], cos_ref[...], sin_ref[...]).astype(BF16)


def _mla_in(x, g, w_in_pad, g_q, g_kv, cosw, sinw, seq):
    m, d = x.shape
    tm = min(512, seq)
    nseq = seq // tm
    n = w_in_pad.shape[1]
    row = lambda i: (i, 0)
    fix = lambda i: (0, 0)
    tab = lambda i: (i % nseq, 0)
    return pl.pallas_call(
        _mla_in_kernel,
        grid=(m // tm,),
        in_specs=[pl.BlockSpec((tm, d), row), pl.BlockSpec((1, d), fix), pl.BlockSpec((d, n), fix),
                  pl.BlockSpec((1, Q_LORA), fix), pl.BlockSpec((1, KV_LORA), fix),
                  pl.BlockSpec((tm, LANES), tab), pl.BlockSpec((tm, LANES), tab)],
        out_specs=[pl.BlockSpec((tm, Q_LORA), row), pl.BlockSpec((tm, KV_LORA), row),
                   pl.BlockSpec((tm, LANES), row)],
        out_shape=[jax.ShapeDtypeStruct((m, Q_LORA), BF16), jax.ShapeDtypeStruct((m, KV_LORA), BF16),
                   jax.ShapeDtypeStruct((m, LANES), BF16)],
        compiler_params=_params("parallel"),
        name="mla_in",
    )(x, g, w_in_pad, g_q, g_kv, cosw, sinw)


def _mla_q_kernel(cq_ref, w_ref, cos_ref, sin_ref, q_ref, *, scale):
    res = jnp.dot(cq_ref[...], w_ref[...], preferred_element_type=F32)
    cosw = cos_ref[...]
    sinw = sin_ref[...]
    for h in range(MLA_HEADS):
        lo = res[:, h * QK_PAD:h * QK_PAD + NOPE_DIM]
        up = _rope_upper(res[:, h * QK_PAD + NOPE_DIM:(h + 1) * QK_PAD], cosw, sinw)
        q_ref[h * QK_PAD:h * QK_PAD + NOPE_DIM, :] = (lo * scale).T.astype(BF16)
        q_ref[h * QK_PAD + NOPE_DIM:(h + 1) * QK_PAD, :] = (up * scale).T.astype(BF16)


def _mla_q(cq, w_uq_pad, cosw, sinw, seq):
    m, c = cq.shape
    n = w_uq_pad.shape[1]
    tm = min(256, seq)
    nseq = seq // tm
    scale = (NOPE_DIM + ROPE_DIM) ** -0.5 * math.log2(math.e)
    return pl.pallas_call(
        functools.partial(_mla_q_kernel, scale=scale),
        grid=(m // tm,),
        in_specs=[pl.BlockSpec((tm, c), lambda i: (i, 0)), pl.BlockSpec((c, n), lambda i: (0, 0)),
                  pl.BlockSpec((tm, LANES), lambda i: (i % nseq, 0)),
                  pl.BlockSpec((tm, LANES), lambda i: (i % nseq, 0))],
        out_specs=pl.BlockSpec((n, tm), lambda i: (0, i)),
        out_shape=jax.ShapeDtypeStruct((n, m), BF16),
        compiler_params=_params("parallel"),
        name="mla_q",
    )(cq, w_uq_pad, cosw, sinw)


def _mla_kv_kernel(ckv_ref, kpe_ref, w_ref, k_ref, vt_ref):
    res = jnp.dot(ckv_ref[...], w_ref[...], preferred_element_type=F32)
    kpe = kpe_ref[...]
    for h in range(MLA_HEADS):
        k_ref[h, :, :NOPE_DIM] = res[:, 2 * h * NOPE_DIM:(2 * h + 1) * NOPE_DIM].astype(BF16)
        k_ref[h, :, NOPE_DIM:] = kpe
        vt_ref[h * V_DIM:(h + 1) * V_DIM, :] = res[:, (2 * h + 1) * V_DIM:(2 * h + 2) * V_DIM].T.astype(BF16)


def _mla_kv(ckv, kpe, w_ukv):
    m, c = ckv.shape
    n = w_ukv.shape[1]
    tm = min(256, m)
    return pl.pallas_call(
        _mla_kv_kernel,
        grid=(m // tm,),
        in_specs=[pl.BlockSpec((tm, c), lambda i: (i, 0)), pl.BlockSpec((tm, LANES), lambda i: (i, 0)),
                  pl.BlockSpec((c, n), lambda i: (0, 0))],
        out_specs=[pl.BlockSpec((MLA_HEADS, tm, QK_PAD), lambda i: (0, i, 0)),
                   pl.BlockSpec((n // 2, tm), lambda i: (0, i))],
        out_shape=[jax.ShapeDtypeStruct((MLA_HEADS, m, QK_PAD), BF16), jax.ShapeDtypeStruct((n // 2, m), BF16)],
        compiler_params=_params("parallel"),
        name="mla_kv",
    )(ckv, kpe, w_ukv)


def _matmul_kernel(a_ref, w_ref, o_ref):
    o_ref[...] = jnp.dot(a_ref[...], w_ref[...], preferred_element_type=F32).astype(o_ref.dtype)


def _matmul(a, w, out_dtype, name):
    m, k = a.shape
    n = w.shape[1]
    tm = min(512, m)
    tn = min(1024, n)
    return pl.pallas_call(
        _matmul_kernel,
        grid=(m // tm, n // tn),
        in_specs=[pl.BlockSpec((tm, k), lambda i, j: (i, 0)), pl.BlockSpec((k, tn), lambda i, j: (0, j))],
        out_specs=pl.BlockSpec((tm, tn), lambda i, j: (i, j)),
        out_shape=jax.ShapeDtypeStruct((m, n), out_dtype),
        compiler_params=_params("parallel", "parallel"),
        name=name,
    )(a, w)


def _attn_kernel(qt_ref, k0_ref, k_ref, vt_ref, o_ref, sa_sc, sb_sc, xa_sc, xb_sc, m_sc, acc_sc, *, n_split, n_kv):
    j = pl.program_id(3)
    width = qt_ref.shape[1] // n_split

    def score(k, dst_sc, dst_mx, cols):
        s = jnp.dot(k, qt_ref[:, cols], preferred_element_type=F32)
        dst_sc[:, cols] = s
        dst_mx[:, cols] = jnp.max(s, axis=0, keepdims=True)

    @pl.when(j == 0)
    def _():
        m_sc[...] = jnp.full(m_sc.shape, -jnp.inf, F32)
        acc_sc[...] = jnp.zeros(acc_sc.shape, F32)
        k0 = k0_ref[0]
        for c in range(n_split):
            score(k0, sa_sc, xa_sc, slice(c * width, (c + 1) * width))

    def step(cur, nxt):
        k = k_ref[0]
        vt = vt_ref[...]
        vt1 = jnp.concatenate([vt, jnp.ones((ONES_ROWS, vt.shape[1]), BF16)], axis=0)
        for c in range(n_split):
            cols = slice(c * width, (c + 1) * width)
            if nxt is not None:
                score(k, nxt[0], nxt[1], cols)
            m_prev = m_sc[:, cols]
            m_new = jnp.maximum(m_prev, cur[1][:, cols])
            alpha = jnp.exp2(m_prev - m_new)
            pt = jnp.exp2((cur[0][:, cols] - m_new).astype(BF16))
            acc_sc[:, cols] = alpha * acc_sc[:, cols] + jnp.dot(vt1, pt, preferred_element_type=F32)
            m_sc[:, cols] = m_new

    last = n_kv - 1
    buf_a = (sa_sc, xa_sc)
    buf_b = (sb_sc, xb_sc)

    @pl.when((j % 2 == 0) & (j < last))
    def _():
        step(buf_a, buf_b)

    @pl.when((j % 2 == 1) & (j < last))
    def _():
        step(buf_b, buf_a)

    @pl.when(j == last)
    def _():
        step(buf_a if last % 2 == 0 else buf_b, None)
        o_ref[...] = (acc_sc[:V_DIM, :] / acc_sc[V_DIM:V_DIM + 1, :]).T.astype(o_ref.dtype)


def _attention(qt, k, vt, batch, seq):
    m = k.shape[1]
    tq = min(2048, seq)
    tk = min(1024, seq)
    nq = seq // tq
    nk = seq // tk
    return pl.pallas_call(
        functools.partial(_attn_kernel, n_split=max(tq // 512, 1), n_kv=nk),
        grid=(batch, MLA_HEADS, nq, nk),
        in_specs=[pl.BlockSpec((QK_PAD, tq), lambda b, h, i, j: (h, b * nq + i)),
                  pl.BlockSpec((1, tk, QK_PAD), lambda b, h, i, j: (h, b * nk, 0)),
                  pl.BlockSpec((1, tk, QK_PAD), lambda b, h, i, j: (h, b * nk + jnp.minimum(j + 1, nk - 1), 0)),
                  pl.BlockSpec((V_DIM, tk), lambda b, h, i, j: (h, b * nk + j))],
        out_specs=pl.BlockSpec((tq, V_DIM), lambda b, h, i, j: (b * nq + i, h)),
        out_shape=jax.ShapeDtypeStruct((m, MLA_HEADS * V_DIM), BF16),
        scratch_shapes=[pltpu.VMEM((tk, tq), F32), pltpu.VMEM((tk, tq), F32),
                        pltpu.VMEM((1, tq), F32), pltpu.VMEM((1, tq), F32),
                        pltpu.VMEM((1, tq), F32), pltpu.VMEM((V_DIM + ONES_ROWS, tq), F32)],
        compiler_params=_params("parallel", "parallel", "parallel", "arbitrary"),
        name="mla_attention",
    )(qt, k, k, vt)


def _proj_res_kernel(a_ref, w_ref, x_ref, g_ref, o_ref):
    h = jnp.dot(a_ref[...], w_ref[...], preferred_element_type=F32)
    o_ref[...] = x_ref[...] + _rms(h, g_ref[...])


def _proj_res(a, w, x, g, name):
    m, kdim = a.shape
    n = w.shape[1]
    tm = min(512, m)
    return pl.pallas_call(
        _proj_res_kernel,
        grid=(m // tm,),
        in_specs=[pl.BlockSpec((tm, kdim), lambda i: (i, 0)), pl.BlockSpec((kdim, n), lambda i: (0, 0)),
                  pl.BlockSpec((tm, n), lambda i: (i, 0)), pl.BlockSpec((1, n), lambda i: (0, 0))],
        out_specs=pl.BlockSpec((tm, n), lambda i: (i, 0)),
        out_shape=jax.ShapeDtypeStruct((m, n), F32),
        compiler_params=_params("parallel"),
        name=name,
    )(a, w, x, g)


def _ffn_kernel(x_ref, g_pre_ref, wg_ref, wu_ref, wd_ref, g_post_ref, o_ref, xn_sc, acc_sc):
    f = pl.program_id(1)

    @pl.when(f == 0)
    def _():
        xn_sc[...] = _rms(x_ref[...], g_pre_ref[...]).astype(BF16)
        acc_sc[...] = jnp.zeros(acc_sc.shape, F32)

    xn = xn_sc[...]
    half = wg_ref.shape[1] // 2
    acts = []
    for c in range(2):
        cols = slice(c * half, (c + 1) * half)
        gate = jnp.dot(xn, wg_ref[:, cols], preferred_element_type=F32)
        up = jnp.dot(xn, wu_ref[:, cols], preferred_element_type=F32)
        acts.append((gate * _sigmoid(gate) * up).astype(BF16))
    acc_sc[...] += jnp.dot(jnp.concatenate(acts, axis=1), wd_ref[...], preferred_element_type=F32)

    @pl.when(f == pl.num_programs(1) - 1)
    def _():
        o_ref[...] = x_ref[...] + _rms(acc_sc[...], g_post_ref[...])


def _ffn(x, g_pre, w_gu, w_down, g_post):
    m, d = x.shape
    d_ff = w_down.shape[0]
    tm = min(512, m)
    tf = 512
    nf = d_ff // tf
    return pl.pallas_call(
        _ffn_kernel,
        grid=(m // tm, nf),
        in_specs=[pl.BlockSpec((tm, d), lambda i, f: (i, 0)), pl.BlockSpec((1, d), lambda i, f: (0, 0)),
                  pl.BlockSpec((d, tf), lambda i, f: (0, f)), pl.BlockSpec((d, tf), lambda i, f: (0, f + nf)),
                  pl.BlockSpec((tf, d), lambda i, f: (f, 0)), pl.BlockSpec((1, d), lambda i, f: (0, 0))],
        out_specs=pl.BlockSpec((tm, d), lambda i, f: (i, 0)),
        out_shape=jax.ShapeDtypeStruct((m, d), F32),
        scratch_shapes=[pltpu.VMEM((tm, d), BF16), pltpu.VMEM((tm, d), F32)],
        compiler_params=_params("parallel", "arbitrary"),
        name="ffn",
    )(x, g_pre, w_gu, w_gu, w_down, g_post)


def _rwkv_mix_kernel(x_ref, xp_ref, xn_ref, g_ref, mu_ref, xr_ref, xw_ref, xk_ref, xv_ref, xa_ref, xg_ref,
                     *, tiles_per_seq):
    i = pl.program_id(0)
    g = g_ref[...]
    h = _rms(x_ref[...], g)
    tm = h.shape[0]
    first = (i % tiles_per_seq) == 0
    last = (i % tiles_per_seq) == tiles_per_seq - 1
    hp = jnp.where(first, 0.0, _rms(xp_ref[7:8, :], g))
    hn = jnp.where(last, 0.0, _rms(xn_ref[0:1, :], g))
    row = lax.broadcasted_iota(jnp.int32, h.shape, 0)
    h_prev = jnp.where(row == 0, hp, pltpu.roll(h, 1, 0))
    h_next = jnp.where(row == tm - 1, hn, pltpu.roll(h, tm - 1, 0))
    xx = 0.5 * (h_prev + h_next) - h
    for idx, ref in enumerate((xr_ref, xw_ref, xk_ref, xv_ref, xa_ref, xg_ref)):
        ref[...] = (h + xx * mu_ref[idx:idx + 1, :]).astype(BF16)


def _rwkv_mix(x, g, mu, seq):
    m, d = x.shape
    tm = min(256, seq)
    tps = seq // tm
    nb8 = m // 8
    r8 = tm // 8
    out = jax.ShapeDtypeStruct((m, d), BF16)
    row = lambda i: (i, 0)
    return pl.pallas_call(
        functools.partial(_rwkv_mix_kernel, tiles_per_seq=tps),
        grid=(m // tm,),
        in_specs=[pl.BlockSpec((tm, d), row),
                  pl.BlockSpec((8, d), lambda i: (jnp.maximum(i * r8 - 1, 0), 0)),
                  pl.BlockSpec((8, d), lambda i: (jnp.minimum((i + 1) * r8, nb8 - 1), 0)),
                  pl.BlockSpec((1, d), lambda i: (0, 0)), pl.BlockSpec((8, d), lambda i: (0, 0))],
        out_specs=[pl.BlockSpec((tm, d), row)] * 6,
        out_shape=[out] * 6,
        compiler_params=_params("parallel"),
        name="rwkv_mix",
    )(x, x, x, g, mu)


def _lora_kernel(x_ref, a_ref, b_ref, bias_ref, o_ref, *, mid, post):
    t = jnp.dot(x_ref[...], a_ref[...], preferred_element_type=F32)
    if mid == "tanh":
        t = jnp.tanh(t)
    elif mid == "sigmoid":
        t = _sigmoid(t)
    z = jnp.dot(t.astype(BF16), b_ref[...], preferred_element_type=F32) + bias_ref[...]
    if post == "sigmoid":
        z = _sigmoid(z)
    elif post == "logdecay":
        z = -math.exp(-0.5) * _sigmoid(z)
    o_ref[...] = z.astype(o_ref.dtype)


def _lora(x, a, b, bias, mid, post, out_dtype, name):
    m, d = x.shape
    r = a.shape[1]
    n = b.shape[1]
    tm = min(512, m)
    return pl.pallas_call(
        functools.partial(_lora_kernel, mid=mid, post=post),
        grid=(m // tm,),
        in_specs=[pl.BlockSpec((tm, d), lambda i: (i, 0)), pl.BlockSpec((d, r), lambda i: (0, 0)),
                  pl.BlockSpec((r, n), lambda i: (0, 0)), pl.BlockSpec((1, n), lambda i: (0, 0))],
        out_specs=pl.BlockSpec((tm, n), lambda i: (i, 0)),
        out_shape=jax.ShapeDtypeStruct((m, n), out_dtype),
        compiler_params=_params("parallel"),
        name=name,
    )(x, a, b, bias)


def _wkv_pre(r_ref, k_ref, v_ref, a_ref, lw_ref, k_k, k_a, ones_bd, blk, head_mask, gl, *, reverse):
    t_len = WKV_CHUNK
    ng = gl // RWKV_HEAD
    gt = ng * t_len
    n_chunks = r_ref.shape[0] // t_len

    wrow = lax.broadcasted_iota(jnp.int32, (t_len, gt), 0)
    wcol = lax.broadcasted_iota(jnp.int32, (t_len, gt), 1) % t_len
    strict = (wcol > wrow) if reverse else (wcol < wrow)
    incl = (wcol >= wrow) if reverse else (wcol <= wrow)
    eye_w = jnp.where(wcol == wrow, 1.0, 0.0)
    trow = lax.broadcasted_iota(jnp.int32, (t_len, t_len), 0)
    tcol = lax.broadcasted_iota(jnp.int32, (t_len, t_len), 1)
    tri = jnp.where((tcol >= trow) if reverse else (tcol <= trow), 1.0, 0.0).astype(BF16)
    bd_mask = jnp.where(lax.broadcasted_iota(jnp.int32, (gt, gt), 0) // t_len
                        == lax.broadcasted_iota(jnp.int32, (gt, gt), 1) // t_len, 1.0, 0.0).astype(BF16)

    def bdiag(w):
        return jnp.concatenate([w.astype(BF16)] * ng, axis=0) * bd_mask

    def mmb(a, b_bf16):
        return jnp.dot(a.astype(BF16), b_bf16, preferred_element_type=F32)

    items = []
    for ci in range(n_chunks):
        c = (n_chunks - 1 - ci) if reverse else ci
        items.append({"rows": slice(c * t_len, (c + 1) * t_len)})

    for it in items:
        lw = lw_ref[it["rows"], :]
        lw_hi = lw.astype(BF16)
        lw_lo = (lw - lw_hi.astype(F32)).astype(BF16)
        it["lw"] = lw
        cum2 = jnp.dot(tri, jnp.concatenate([lw_hi, lw_lo], axis=1), preferred_element_type=F32)
        it["cum"] = cum2[:, :gl] + cum2[:, gl:]
    yield

    kkr_all = k_ref[...] * k_k
    nrm_all = jnp.maximum(jnp.sqrt(jnp.dot((kkr_all * kkr_all).astype(BF16), ones_bd,
                                           preferred_element_type=F32)), 1e-12)
    kk_all = kkr_all / nrm_all

    for it in items:
        rows, cum, lw = it["rows"], it["cum"], it["lw"]
        a = a_ref[rows, :]
        kk = kk_all[rows]
        kd = k_ref[rows, :] * (1.0 + (a - 1.0) * k_a)
        b = kk * a
        tot = cum[0:1] if reverse else cum[t_len - 1:t_len]
        inv_p = jnp.exp(-cum)
        rt = r_ref[rows, :] * jnp.exp(cum)
        kkt = kk * jnp.exp(cum - lw)
        to_end = jnp.exp(tot - cum)
        it["decay"] = jnp.exp(tot)
        it["b_end"] = (b * to_end).astype(BF16)
        it["k_end"] = (kd * to_end).astype(BF16)
        it["rt"] = rt.astype(BF16)
        it["kkt_blk"] = blk(kkt)
        lhs = jnp.concatenate([kkt, rt], axis=0).astype(BF16)
        rhs = jnp.concatenate([blk(b * inv_p), blk(kd * inv_p)], axis=0)
        aa = lax.dot_general(lhs, rhs, NT, preferred_element_type=F32)
        it["a_ab"] = jnp.where(strict, aa[:t_len, :gt], 0.0)
        it["a_ak"] = jnp.where(strict, aa[:t_len, gt:], 0.0).astype(BF16)
        it["a_r"] = jnp.concatenate([jnp.where(incl, aa[t_len:, :gt], 0.0),
                                     jnp.where(incl, aa[t_len:, gt:], 0.0)], axis=1).astype(BF16)
    yield

    for it in items:
        it["x"] = eye_w - it["a_ab"]
        it["p"] = mmb(it["a_ab"], bdiag(it["a_ab"]))
    yield
    rounds = int(math.log2(t_len)) - 1
    for rnd in range(rounds):
        for it in items:
            p_bd = bdiag(it["p"])
            if rnd < rounds - 1:
                xp = mmb(jnp.concatenate([it["x"], it["p"]], axis=0), p_bd)
                it["x"] = it["x"] + xp[:t_len]
                it["p"] = xp[t_len:]
            else:
                it["x"] = it["x"] + mmb(it["x"], p_bd)
        yield

    for it in items:
        v = v_ref[it["rows"], :]
        it["v"] = v
        it["v_blk"] = blk(v)
        it["akv"] = jnp.dot(it["a_ak"], it["v_blk"], preferred_element_type=F32)
        it["x_b"] = it["x"].astype(BF16)
        it["w_k"] = jnp.dot(it["x_b"], it["kkt_blk"], preferred_element_type=F32)
    yield
    for it in items:
        it["u"] = jnp.dot(it["x_b"], blk(it["akv"]), preferred_element_type=F32)
    yield
    for it in items:
        m_full = lax.dot_general(it["w_k"].astype(BF16), it["b_end"], TN, preferred_element_type=F32)
        it["m"] = jnp.where(head_mask, m_full, 0.0).astype(BF16)
    yield
    for it in items:
        c_full = lax.dot_general(jnp.concatenate([-it["u"], it["v"]], axis=0).astype(BF16),
                                 jnp.concatenate([it["b_end"], it["k_end"]], axis=0),
                                 TN, preferred_element_type=F32)
        it["c"] = jnp.where(head_mask, c_full, 0.0)
        it["wr"] = jnp.concatenate([it["w_k"].astype(BF16), it["rt"]], axis=0)
    return items


def _staggered(first, second):
    results = [None, None]
    live = [first, second]
    next(first)
    while any(g is not None for g in live):
        for idx in (1, 0):
            if live[idx] is not None:
                try:
                    next(live[idx])
                except StopIteration as stop:
                    results[idx] = stop.value
                    live[idx] = None
    return results


def _wkv2_kernel(rf_ref, kf_ref, vf_ref, a0f_ref, lwf_ref, a1f_ref, rb_ref, kb_ref, vb_ref, a1b_ref, lwb_ref,
                 kk_par, ka_par, rk_par, yf_ref, yb_ref, bonus_ref, stf_ref, stb_ref):
    @pl.when(pl.program_id(2) == 0)
    def _():
        stf_ref[...] = jnp.zeros(stf_ref.shape, F32)
        stb_ref[...] = jnp.zeros(stb_ref.shape, F32)

    gl = stf_ref.shape[0]
    gt = (gl // RWKV_HEAD) * WKV_CHUNK
    ones_bd = _head_ones(gl)
    k_k = kk_par[...]
    k_a = ka_par[...]

    kf = kf_ref[...]
    kd_sum = kf * (2.0 + (a0f_ref[...] + a1f_ref[...] - 2.0) * k_a)
    bonus_ref[...] = jnp.dot((rf_ref[...] * kd_sum * rk_par[...]).astype(BF16), ones_bd,
                             preferred_element_type=F32) * vf_ref[...]

    blk_mask = jnp.where(lax.broadcasted_iota(jnp.int32, (gt, gl), 0) // WKV_CHUNK
                         == lax.broadcasted_iota(jnp.int32, (gt, gl), 1) // RWKV_HEAD, 1.0, 0.0).astype(BF16)
    head_mask = (lax.broadcasted_iota(jnp.int32, (gl, gl), 0) // RWKV_HEAD
                 == lax.broadcasted_iota(jnp.int32, (gl, gl), 1) // RWKV_HEAD)

    def blk(x):
        return jnp.concatenate([x.astype(BF16)] * (gl // RWKV_HEAD), axis=0) * blk_mask

    t_len = WKV_CHUNK
    items_f, items_b = _staggered(
        _wkv_pre(rf_ref, kf_ref, vf_ref, a0f_ref, lwf_ref, k_k, k_a, ones_bd, blk, head_mask, gl, reverse=False),
        _wkv_pre(rb_ref, kb_ref, vb_ref, a1b_ref, lwb_ref, k_k, k_a, ones_bd, blk, head_mask, gl, reverse=True))
    chains = [(items_f, yf_ref, stf_ref), (items_b, yb_ref, stb_ref)]
    states = [st_ref[...] for _, _, st_ref in chains]

    def emit_y(y_ref, it, sa, rs):
        y_ref[it["rows"], :] = rs + jnp.dot(it["a_r"], jnp.concatenate([blk(sa), it["v_blk"]], axis=0),
                                            preferred_element_type=F32)

    pending = []
    for ci in range(len(chains[0][0])):
        st_bf = [st.astype(BF16) for st in states]
        its = [items[ci] for items, _, _ in chains]
        st_m = [jnp.dot(sb, it["m"], preferred_element_type=F32) for sb, it in zip(st_bf, its)]
        ws = [lax.dot_general(it["wr"], sb, NT, preferred_element_type=F32) for sb, it in zip(st_bf, its)]
        for args in pending:
            emit_y(*args)
        pending = [(y_ref, it, -(w[:t_len] + it["u"]), w[t_len:])
                   for (_, y_ref, _), it, w in zip(chains, its, ws)]
        states = [st * it["decay"] - sm + it["c"] for st, it, sm in zip(states, its, st_m)]
    for args in pending:
        emit_y(*args)
    for (_, _, st_ref), st in zip(chains, states):
        st_ref[...] = st


def _wkv2(r, k, v, a0, lw0, a1, lw1, k_k, k_a, r_k, batch, seq):
    m, d = r.shape
    gl = WKV_GROUP * RWKV_HEAD
    tb = min(WKV_BLOCK, seq)
    nb = seq // tb
    fwd = pl.BlockSpec((tb, gl), lambda bi, g, j: (bi * nb + j, g))
    bwd = pl.BlockSpec((tb, gl), lambda bi, g, j: (bi * nb + nb - 1 - j, g))
    par = pl.BlockSpec((1, gl), lambda bi, g, j: (0, g))
    out = jax.ShapeDtypeStruct((m, d), F32)
    return pl.pallas_call(
        _wkv2_kernel,
        grid=(batch, d // gl, nb),
        in_specs=[fwd] * 6 + [bwd] * 5 + [par] * 3,
        out_specs=[fwd, bwd, fwd],
        out_shape=[out, out, out],
        scratch_shapes=[pltpu.VMEM((gl, gl), F32), pltpu.VMEM((gl, gl), F32)],
        compiler_params=_params("parallel", "parallel", "arbitrary"),
        name="wkv",
    )(r, k, v, a0, lw0, a1, r, k, v, a1, lw1, k_k, k_a, r_k)


def _wkv_post_kernel(y0_ref, y1_ref, bonus_ref, g_ref, lg_ref, lb_ref, o_ref):
    ones_bd = _head_ones(LANES)
    d = y0_ref.shape[1]
    inv_n = 1.0 / RWKV_HEAD
    for c in range(d // LANES):
        sl = slice(c * LANES, (c + 1) * LANES)
        y = y0_ref[:, sl] + y1_ref[:, sl]
        mean = _head_sum(y, ones_bd) * inv_n
        yc = y - mean
        var = _head_sum(yc * yc, ones_bd) * inv_n
        yn = yc * lax.rsqrt(var + LNX_EPS) * lg_ref[:, sl] + lb_ref[:, sl]
        o_ref[:, sl] = ((yn + bonus_ref[:, sl]) * g_ref[:, sl]).astype(o_ref.dtype)


def _wkv_post(y0, y1, bonus, g, lnx_g, lnx_b):
    m, d = y0.shape
    tm = min(256, m)
    row = pl.BlockSpec((tm, d), lambda i: (i, 0))
    par = pl.BlockSpec((1, d), lambda i: (0, 0))
    return pl.pallas_call(
        _wkv_post_kernel,
        grid=(m // tm,),
        in_specs=[row] * 4 + [par] * 2,
        out_specs=row,
        out_shape=jax.ShapeDtypeStruct((m, d), BF16),
        compiler_params=_params("parallel"),
        name="wkv_post",
    )(y0, y1, bonus, g, lnx_g, lnx_b)


def _rope_tables(seq):
    half = ROPE_DIM // 2
    inv = 1.0 / (ROPE_THETA ** (jnp.arange(half, dtype=F32) * (2.0 / ROPE_DIM)))
    ang = jnp.arange(seq, dtype=F32)[:, None] * inv[None, :]
    cos, sin = jnp.cos(ang), jnp.sin(ang)
    zero = jnp.zeros((seq, LANES - ROPE_DIM), F32)
    return (jnp.concatenate([cos, cos, zero], axis=1), jnp.concatenate([-sin, sin, zero], axis=1))


def _row(v):
    return v.reshape(1, -1).astype(F32)


def _pad_cols(w, n):
    return jnp.pad(w, ((0, 0), (0, n - w.shape[1])))


def _pad_rows(w, n):
    return jnp.pad(w, ((0, n - w.shape[0]), (0, 0)))


def _mla_layer(x, batch, seq, g_pre, g_post, w_in, g_q, g_kv, w_uq, w_ukv, w_o):
    cosw, sinw = _rope_tables(seq)
    w_in_pad = _pad_cols(w_in, Q_LORA + KV_LORA + LANES).astype(BF16)
    w_uq_pad = jnp.pad(w_uq.reshape(Q_LORA, MLA_HEADS, NOPE_DIM + ROPE_DIM),
                       ((0, 0), (0, 0), (0, QK_PAD - NOPE_DIM - ROPE_DIM))).reshape(Q_LORA, MLA_HEADS * QK_PAD)
    cq, ckv, kpe = _mla_in(x, _row(g_pre), w_in_pad, _row(g_q), _row(g_kv), cosw, sinw, seq)
    qt = _mla_q(cq, w_uq_pad.astype(BF16), cosw, sinw, seq)
    k, vt = _mla_kv(ckv, kpe, w_ukv.astype(BF16))
    o = _attention(qt, k, vt, batch, seq)
    return _proj_res(o, w_o.astype(BF16), x, _row(g_post), "mla_out")


def _rwkv_layer(x, batch, seq, g_pre, g_post, mu, w_r, w_k, w_v, w_o, w0, w1, w2, a0, a1, a2, g1, g2,
                k_k, k_a, r_k, lnx_g, lnx_b):
    d = x.shape[1]
    mu8 = jnp.pad(mu, ((0, 8 - mu.shape[0]), (0, 0)))
    xr, xw, xk, xv, xa, xg = _rwkv_mix(x, _row(g_pre), mu8, seq)
    r = _matmul(xr, w_r.astype(BF16), F32, "rwkv_r")
    k = _matmul(xk, w_k.astype(BF16), F32, "rwkv_k")
    v = _matmul(xv, w_v.astype(BF16), F32, "rwkv_v")
    zero = jnp.zeros((1, d), F32)
    g = _lora(xg, g1.astype(BF16), g2.astype(BF16), zero, "sigmoid", "none", BF16, "rwkv_gate")
    lws, avs = [], []
    for di in range(2):
        w1p = _pad_cols(w1[di], LANES).astype(BF16)
        w2p = _pad_rows(w2[di], LANES).astype(BF16)
        a1p = _pad_cols(a1[di], LANES).astype(BF16)
        a2p = _pad_rows(a2[di], LANES).astype(BF16)
        lws.append(_lora(xw, w1p, w2p, _row(w0[di]), "tanh", "logdecay", F32, "rwkv_decay%d" % di))
        avs.append(_lora(xa, a1p, a2p, _row(a0[di]), "none", "sigmoid", F32, "rwkv_a%d" % di))
    y0, y1, bonus = _wkv2(r, k, v, avs[0], lws[0], avs[1], lws[1], _row(k_k), _row(k_a), _row(r_k), batch, seq)
    yg = _wkv_post(y0, y1, bonus, g, _row(lnx_g), _row(lnx_b))
    return _proj_res(yg, w_o.astype(BF16), x, _row(g_post), "rwkv_out")


def _trunk(x3, norm_g, mla_w_in, mla_g_q, mla_g_kv, mla_w_uq, mla_w_ukv, mla_w_o,
           rwkv_mu, rwkv_w_r, rwkv_w_k, rwkv_w_v, rwkv_w_o, rwkv_w0, rwkv_w1, rwkv_w2,
           rwkv_a0, rwkv_a1, rwkv_a2, rwkv_g1, rwkv_g2, rwkv_k_k, rwkv_k_a, rwkv_r_k,
           rwkv_lnx_g, rwkv_lnx_b, ffn_w_gu, ffn_w_down):
    batch, seq, d = x3.shape
    x = x3.reshape(batch * seq, d)
    depth = norm_g.shape[0]
    for i in range(depth):
        j = i // 2
        if i % 2 == 0:
            x = _mla_layer(x, batch, seq, norm_g[i, 0], norm_g[i, 1], mla_w_in[j], mla_g_q[j], mla_g_kv[j],
                           mla_w_uq[j], mla_w_ukv[j], mla_w_o[j])
        else:
            x = _rwkv_layer(x, batch, seq, norm_g[i, 0], norm_g[i, 1], rwkv_mu[j], rwkv_w_r[j], rwkv_w_k[j],
                            rwkv_w_v[j], rwkv_w_o[j], rwkv_w0[j], rwkv_w1[j], rwkv_w2[j], rwkv_a0[j],
                            rwkv_a1[j], rwkv_a2[j], rwkv_g1[j], rwkv_g2[j], rwkv_k_k[j], rwkv_k_a[j],
                            rwkv_r_k[j].reshape(-1), rwkv_lnx_g[j], rwkv_lnx_b[j])
        x = _ffn(x, _row(norm_g[i, 2]), ffn_w_gu[i].astype(BF16), ffn_w_down[i].astype(BF16), _row(norm_g[i, 3]))
    return x.reshape(batch, seq, d)


def kernel(x_prompt, x_sample, norm_g, mla_w_in, mla_g_q, mla_g_kv, mla_w_uq, mla_w_ukv, mla_w_o, rwkv_mu, rwkv_w_r, rwkv_w_k, rwkv_w_v, rwkv_w_o, rwkv_w0, rwkv_w1, rwkv_w2, rwkv_a0, rwkv_a1, rwkv_a2, rwkv_g1, rwkv_g2, rwkv_k_k, rwkv_k_a, rwkv_r_k, rwkv_lnx_g, rwkv_lnx_b, ffn_w_gu, ffn_w_down):
    params = (norm_g, mla_w_in, mla_g_q, mla_g_kv, mla_w_uq, mla_w_ukv, mla_w_o,
              rwkv_mu, rwkv_w_r, rwkv_w_k, rwkv_w_v, rwkv_w_o, rwkv_w0, rwkv_w1, rwkv_w2,
              rwkv_a0, rwkv_a1, rwkv_a2, rwkv_g1, rwkv_g2, rwkv_k_k, rwkv_k_a, rwkv_r_k,
              rwkv_lnx_g, rwkv_lnx_b, ffn_w_gu, ffn_w_down)
    return (_trunk(x_prompt, *params), _trunk(x_sample, *params))
```

```python
import functools
import math

import jax
import jax.numpy as jnp
from jax import lax
from jax.experimental import pallas as pl
from jax.experimental.pallas import tpu as pltpu

F32 = jnp.float32
BF16 = jnp.bfloat16

NORM_EPS = 1e-6
LNX_EPS = 64e-5
ROPE_THETA = 10000.0

MLA_HEADS = 16
Q_LORA = 512
KV_LORA = 512
NOPE_DIM = 128
ROPE_DIM = 64
V_DIM = 128
QK_PAD = 256
ONES_ROWS = 16
RWKV_HEAD = 64

LANES = 128
VMEM_LIMIT = 56 * 1024 * 1024

WKV_CHUNK = 64
WKV_GROUP = 4
WKV_BLOCK = 512

NT = (((1,), (1,)), ((), ()))
TN = (((0,), (0,)), ((), ()))


def _params(*sem):
    return pltpu.CompilerParams(dimension_semantics=sem, vmem_limit_bytes=VMEM_LIMIT)


def _rms(x, g):
    return x * lax.rsqrt(jnp.mean(x * x, axis=-1, keepdims=True) + NORM_EPS) * g


def _sigmoid(z):
    return 1.0 / (1.0 + jnp.exp(-z))


def _rope_upper(up, cosw, sinw):
    lane = lax.broadcasted_iota(jnp.int32, up.shape, 1)
    swapped = jnp.where(lane < ROPE_DIM // 2, pltpu.roll(up, LANES - ROPE_DIM // 2, 1),
                        pltpu.roll(up, ROPE_DIM // 2, 1))
    return up * cosw + swapped * sinw


def _head_ones(n):
    r = lax.broadcasted_iota(jnp.int32, (n, n), 0) // RWKV_HEAD
    c = lax.broadcasted_iota(jnp.int32, (n, n), 1) // RWKV_HEAD
    return jnp.where(r == c, 1.0, 0.0).astype(BF16)


def _head_sum(z, ones_bd):
    hi = z.astype(BF16)
    lo = (z - hi.astype(F32)).astype(BF16)
    return (jnp.dot(hi, ones_bd, preferred_element_type=F32)
            + jnp.dot(lo, ones_bd, preferred_element_type=F32))


def _mla_in_kernel(x_ref, g_ref, w_ref, gq_ref, gkv_ref, cos_ref, sin_ref, cq_ref, ckv_ref, kpe_ref):
    xn = _rms(x_ref[...], g_ref[...]).astype(BF16)
    h = jnp.dot(xn, w_ref[...], preferred_element_type=F32)
    cq_ref[...] = _rms(h[:, :Q_LORA], gq_ref[...]).astype(BF16)
    ckv_ref[...] = _rms(h[:, Q_LORA:Q_LORA + KV_LORA], gkv_ref[...]).astype(BF16)
    kpe_ref[...] = _rope_upper(h[:, Q_LORA + KV_LORA:], cos_ref[...], sin_ref[...]).astype(BF16)


def _mla_in(x, g, w_in_pad, g_q, g_kv, cosw, sinw, seq):
    m, d = x.shape
    tm = min(512, seq)
    nseq = seq // tm
    n = w_in_pad.shape[1]
    row = lambda i: (i, 0)
    fix = lambda i: (0, 0)
    tab = lambda i: (i % nseq, 0)
    return pl.pallas_call(
        _mla_in_kernel,
        grid=(m // tm,),
        in_specs=[pl.BlockSpec((tm, d), row), pl.BlockSpec((1, d), fix), pl.BlockSpec((d, n), fix),
                  pl.BlockSpec((1, Q_LORA), fix), pl.BlockSpec((1, KV_LORA), fix),
                  pl.BlockSpec((tm, LANES), tab), pl.BlockSpec((tm, LANES), tab)],
        out_specs=[pl.BlockSpec((tm, Q_LORA), row), pl.BlockSpec((tm, KV_LORA), row),
                   pl.BlockSpec((tm, LANES), row)],
        out_shape=[jax.ShapeDtypeStruct((m, Q_LORA), BF16), jax.ShapeDtypeStruct((m, KV_LORA), BF16),
                   jax.ShapeDtypeStruct((m, LANES), BF16)],
        compiler_params=_params("parallel"),
        name="mla_in",
    )(x, g, w_in_pad, g_q, g_kv, cosw, sinw)


def _mla_q_kernel(cq_ref, w_ref, cos_ref, sin_ref, q_ref, *, scale):
    res = jnp.dot(cq_ref[...], w_ref[...], preferred_element_type=F32)
    cosw = cos_ref[...]
    sinw = sin_ref[...]
    for h in range(MLA_HEADS):
        lo = res[:, h * QK_PAD:h * QK_PAD + NOPE_DIM]
        up = _rope_upper(res[:, h * QK_PAD + NOPE_DIM:(h + 1) * QK_PAD], cosw, sinw)
        q_ref[h * QK_PAD:h * QK_PAD + NOPE_DIM, :] = (lo * scale).T.astype(BF16)
        q_ref[h * QK_PAD + NOPE_DIM:(h + 1) * QK_PAD, :] = (up * scale).T.astype(BF16)


def _mla_q(cq, w_uq_pad, cosw, sinw, seq):
    m, c = cq.shape
    n = w_uq_pad.shape[1]
    tm = min(256, seq)
    nseq = seq // tm
    scale = (NOPE_DIM + ROPE_DIM) ** -0.5 * math.log2(math.e)
    return pl.pallas_call(
        functools.partial(_mla_q_kernel, scale=scale),
        grid=(m // tm,),
        in_specs=[pl.BlockSpec((tm, c), lambda i: (i, 0)), pl.BlockSpec((c, n), lambda i: (0, 0)),
                  pl.BlockSpec((tm, LANES), lambda i: (i % nseq, 0)),
                  pl.BlockSpec((tm, LANES), lambda i: (i % nseq, 0))],
        out_specs=pl.BlockSpec((n, tm), lambda i: (0, i)),
        out_shape=jax.ShapeDtypeStruct((n, m), BF16),
        compiler_params=_params("parallel"),
        name="mla_q",
    )(cq, w_uq_pad, cosw, sinw)


def _mla_kv_kernel(ckv_ref, kpe_ref, w_ref, k_ref, vt_ref):
    res = jnp.dot(ckv_ref[...], w_ref[...], preferred_element_type=F32)
    kpe = kpe_ref[...]
    for h in range(MLA_HEADS):
        k_ref[h, :, :NOPE_DIM] = res[:, 2 * h * NOPE_DIM:(2 * h + 1) * NOPE_DIM].astype(BF16)
        k_ref[h, :, NOPE_DIM:] = kpe
        vt_ref[h * V_DIM:(h + 1) * V_DIM, :] = res[:, (2 * h + 1) * V_DIM:(2 * h + 2) * V_DIM].T.astype(BF16)


def _mla_kv(ckv, kpe, w_ukv):
    m, c = ckv.shape
    n = w_ukv.shape[1]
    tm = min(256, m)
    return pl.pallas_call(
        _mla_kv_kernel,
        grid=(m // tm,),
        in_specs=[pl.BlockSpec((tm, c), lambda i: (i, 0)), pl.BlockSpec((tm, LANES), lambda i: (i, 0)),
                  pl.BlockSpec((c, n), lambda i: (0, 0))],
        out_specs=[pl.BlockSpec((MLA_HEADS, tm, QK_PAD), lambda i: (0, i, 0)),
                   pl.BlockSpec((n // 2, tm), lambda i: (0, i))],
        out_shape=[jax.ShapeDtypeStruct((MLA_HEADS, m, QK_PAD), BF16), jax.ShapeDtypeStruct((n // 2, m), BF16)],
        compiler_params=_params("parallel"),
        name="mla_kv",
    )(ckv, kpe, w_ukv)


def _attn_kernel(qt_ref, k0_ref, k_ref, vt_ref, o_ref, sa_sc, sb_sc, xa_sc, xb_sc, m_sc, acc_sc, *, n_split, n_kv):
    j = pl.program_id(3)
    width = qt_ref.shape[1] // n_split

    def score(k, dst_sc, dst_mx, cols):
        s = jnp.dot(k, qt_ref[:, cols], preferred_element_type=F32)
        dst_sc[:, cols] = s
        dst_mx[:, cols] = jnp.max(s, axis=0, keepdims=True)

    @pl.when(j == 0)
    def _():
        m_sc[...] = jnp.full(m_sc.shape, -jnp.inf, F32)
        acc_sc[...] = jnp.zeros(acc_sc.shape, F32)
        k0 = k0_ref[0]
        for c in range(n_split):
            score(k0, sa_sc, xa_sc, slice(c * width, (c + 1) * width))

    def step(cur, nxt):
        k = k_ref[0]
        vt = vt_ref[...]
        vt1 = jnp.concatenate([vt, jnp.ones((ONES_ROWS, vt.shape[1]), BF16)], axis=0)
        for c in range(n_split):
            cols = slice(c * width, (c + 1) * width)
            if nxt is not None:
                score(k, nxt[0], nxt[1], cols)
            m_prev = m_sc[:, cols]
            m_new = jnp.maximum(m_prev, cur[1][:, cols])
            alpha = jnp.exp2(m_prev - m_new)
            pt = jnp.exp2((cur[0][:, cols] - m_new).astype(BF16))
            acc_sc[:, cols] = alpha * acc_sc[:, cols] + jnp.dot(vt1, pt, preferred_element_type=F32)
            m_sc[:, cols] = m_new

    last = n_kv - 1
    buf_a = (sa_sc, xa_sc)
    buf_b = (sb_sc, xb_sc)

    @pl.when((j % 2 == 0) & (j < last))
    def _():
        step(buf_a, buf_b)

    @pl.when((j % 2 == 1) & (j < last))
    def _():
        step(buf_b, buf_a)

    @pl.when(j == last)
    def _():
        step(buf_a if last % 2 == 0 else buf_b, None)
        o_ref[...] = (acc_sc[:V_DIM, :] / acc_sc[V_DIM:V_DIM + 1, :]).T.astype(o_ref.dtype)


def _attention(qt, k, vt, batch, seq):
    m = k.shape[1]
    tq = min(2048, seq)
    tk = min(1024, seq)
    nq = seq // tq
    nk = seq // tk
    return pl.pallas_call(
        functools.partial(_attn_kernel, n_split=max(tq // 512, 1), n_kv=nk),
        grid=(batch, MLA_HEADS, nq, nk),
        in_specs=[pl.BlockSpec((QK_PAD, tq), lambda b, h, i, j: (h, b * nq + i)),
                  pl.BlockSpec((1, tk, QK_PAD), lambda b, h, i, j: (h, b * nk, 0)),
                  pl.BlockSpec((1, tk, QK_PAD), lambda b, h, i, j: (h, b * nk + jnp.minimum(j + 1, nk - 1), 0)),
                  pl.BlockSpec((V_DIM, tk), lambda b, h, i, j: (h, b * nk + j))],
        out_specs=pl.BlockSpec((tq, V_DIM), lambda b, h, i, j: (b * nq + i, h)),
        out_shape=jax.ShapeDtypeStruct((m, MLA_HEADS * V_DIM), BF16),
        scratch_shapes=[pltpu.VMEM((tk, tq), F32), pltpu.VMEM((tk, tq), F32),
                        pltpu.VMEM((1, tq), F32), pltpu.VMEM((1, tq), F32),
                        pltpu.VMEM((1, tq), F32), pltpu.VMEM((V_DIM + ONES_ROWS, tq), F32)],
        compiler_params=_params("parallel", "parallel", "parallel", "arbitrary"),
        name="mla_attention",
    )(qt, k, k, vt)


def _proj_res_kernel(a_ref, w_ref, x_ref, g_ref, o_ref):
    h = jnp.dot(a_ref[...], w_ref[...], preferred_element_type=F32)
    o_ref[...] = x_ref[...] + _rms(h, g_ref[...])


def _proj_res(a, w, x, g, name):
    m, kdim = a.shape
    n = w.shape[1]
    tm = min(512, m)
    return pl.pallas_call(
        _proj_res_kernel,
        grid=(m // tm,),
        in_specs=[pl.BlockSpec((tm, kdim), lambda i: (i, 0)), pl.BlockSpec((kdim, n), lambda i: (0, 0)),
                  pl.BlockSpec((tm, n), lambda i: (i, 0)), pl.BlockSpec((1, n), lambda i: (0, 0))],
        out_specs=pl.BlockSpec((tm, n), lambda i: (i, 0)),
        out_shape=jax.ShapeDtypeStruct((m, n), F32),
        compiler_params=_params("parallel"),
        name=name,
    )(a, w, x, g)


def _ffn_kernel(x_ref, g_pre_ref, wg_ref, wu_ref, wd_ref, g_post_ref, o_ref, xn_sc, acc_sc):
    f = pl.program_id(1)

    @pl.when(f == 0)
    def _():
        xn_sc[...] = _rms(x_ref[...], g_pre_ref[...]).astype(BF16)
        acc_sc[...] = jnp.zeros(acc_sc.shape, F32)

    xn = xn_sc[...]
    half = wg_ref.shape[1] // 2
    acts = []
    for c in range(2):
        cols = slice(c * half, (c + 1) * half)
        gate = jnp.dot(xn, wg_ref[:, cols], preferred_element_type=F32)
        up = jnp.dot(xn, wu_ref[:, cols], preferred_element_type=F32)
        acts.append((gate * _sigmoid(gate) * up).astype(BF16))
    acc_sc[...] += jnp.dot(jnp.concatenate(acts, axis=1), wd_ref[...], preferred_element_type=F32)

    @pl.when(f == pl.num_programs(1) - 1)
    def _():
        o_ref[...] = x_ref[...] + _rms(acc_sc[...], g_post_ref[...])


def _ffn(x, g_pre, w_gu, w_down, g_post):
    m, d = x.shape
    d_ff = w_down.shape[0]
    tm = min(512, m)
    tf = 512
    nf = d_ff // tf
    return pl.pallas_call(
        _ffn_kernel,
        grid=(m // tm, nf),
        in_specs=[pl.BlockSpec((tm, d), lambda i, f: (i, 0)), pl.BlockSpec((1, d), lambda i, f: (0, 0)),
                  pl.BlockSpec((d, tf), lambda i, f: (0, f)), pl.BlockSpec((d, tf), lambda i, f: (0, f + nf)),
                  pl.BlockSpec((tf, d), lambda i, f: (f, 0)), pl.BlockSpec((1, d), lambda i, f: (0, 0))],
        out_specs=pl.BlockSpec((tm, d), lambda i, f: (i, 0)),
        out_shape=jax.ShapeDtypeStruct((m, d), F32),
        scratch_shapes=[pltpu.VMEM((tm, d), BF16), pltpu.VMEM((tm, d), F32)],
        compiler_params=_params("parallel", "arbitrary"),
        name="ffn",
    )(x, g_pre, w_gu, w_gu, w_down, g_post)


def _rwkv_mix_kernel(x_ref, xp_ref, xn_ref, g_ref, mu_ref, wr_ref, wk_ref, wv_ref, w1_ref, a1_ref, g1_ref,
                     r_ref, k_ref, v_ref, tw_ref, ta_ref, sg_ref, *, tiles_per_seq):
    i = pl.program_id(0)
    g = g_ref[...]
    h = _rms(x_ref[...], g)
    tm = h.shape[0]
    first = (i % tiles_per_seq) == 0
    last = (i % tiles_per_seq) == tiles_per_seq - 1
    hp = jnp.where(first, 0.0, _rms(xp_ref[7:8, :], g))
    hn = jnp.where(last, 0.0, _rms(xn_ref[0:1, :], g))
    row = lax.broadcasted_iota(jnp.int32, h.shape, 0)
    h_prev = jnp.where(row == 0, hp, pltpu.roll(h, 1, 0))
    h_next = jnp.where(row == tm - 1, hn, pltpu.roll(h, tm - 1, 0))
    xx = 0.5 * (h_prev + h_next) - h

    def lerp(idx):
        return (h + xx * mu_ref[idx:idx + 1, :]).astype(BF16)

    r_ref[...] = jnp.dot(lerp(0), wr_ref[...], preferred_element_type=F32)
    k_ref[...] = jnp.dot(lerp(2), wk_ref[...], preferred_element_type=F32)
    v_ref[...] = jnp.dot(lerp(3), wv_ref[...], preferred_element_type=F32)
    tw_ref[...] = jnp.tanh(jnp.dot(lerp(1), w1_ref[...], preferred_element_type=F32)).astype(BF16)
    ta_ref[...] = jnp.dot(lerp(4), a1_ref[...], preferred_element_type=F32).astype(BF16)
    sg_ref[...] = _sigmoid(jnp.dot(lerp(5), g1_ref[...], preferred_element_type=F32)).astype(BF16)


def _rwkv_mix(x, g, mu, w_r, w_k, w_v, w1_cat, a1_cat, g1, seq):
    m, d = x.shape
    lr = w1_cat.shape[1]
    tm = min(256, seq)
    tps = seq // tm
    nb8 = m // 8
    r8 = tm // 8
    row = lambda i: (i, 0)
    fix = lambda i: (0, 0)
    weight = pl.BlockSpec((d, d), fix, pipeline_mode=pl.Buffered(1))
    weight_lr = pl.BlockSpec((d, lr), fix, pipeline_mode=pl.Buffered(1))
    return pl.pallas_call(
        functools.partial(_rwkv_mix_kernel, tiles_per_seq=tps),
        grid=(m // tm,),
        in_specs=[pl.BlockSpec((tm, d), row),
                  pl.BlockSpec((8, d), lambda i: (jnp.maximum(i * r8 - 1, 0), 0)),
                  pl.BlockSpec((8, d), lambda i: (jnp.minimum((i + 1) * r8, nb8 - 1), 0)),
                  pl.BlockSpec((1, d), fix), pl.BlockSpec((8, d), fix), weight, weight, weight,
                  weight_lr, weight_lr, weight_lr],
        out_specs=[pl.BlockSpec((tm, d), row)] * 3 + [pl.BlockSpec((tm, lr), row)] * 3,
        out_shape=[jax.ShapeDtypeStruct((m, d), F32)] * 3 + [jax.ShapeDtypeStruct((m, lr), BF16)] * 3,
        compiler_params=_params("parallel"),
        name="rwkv_mix",
    )(x, x, x, g, mu, w_r, w_k, w_v, w1_cat, a1_cat, g1)


def _wkv_pre(r_ref, k_ref, v_ref, a_ref, lw_ref, k_k, k_a, ones_bd, blk, head_mask, gl, *, reverse):
    t_len = WKV_CHUNK
    ng = gl // RWKV_HEAD
    gt = ng * t_len
    n_chunks = r_ref.shape[0] // t_len

    wrow = lax.broadcasted_iota(jnp.int32, (t_len, gt), 0)
    wcol = lax.broadcasted_iota(jnp.int32, (t_len, gt), 1) % t_len
    strict = (wcol > wrow) if reverse else (wcol < wrow)
    incl = (wcol >= wrow) if reverse else (wcol <= wrow)
    eye_w = jnp.where(wcol == wrow, 1.0, 0.0)
    trow = lax.broadcasted_iota(jnp.int32, (t_len, t_len), 0)
    tcol = lax.broadcasted_iota(jnp.int32, (t_len, t_len), 1)
    tri = jnp.where((tcol >= trow) if reverse else (tcol <= trow), 1.0, 0.0).astype(BF16)
    bd_mask = jnp.where(lax.broadcasted_iota(jnp.int32, (gt, gt), 0) // t_len
                        == lax.broadcasted_iota(jnp.int32, (gt, gt), 1) // t_len, 1.0, 0.0).astype(BF16)

    def bdiag(w):
        return jnp.concatenate([w.astype(BF16)] * ng, axis=0) * bd_mask

    def mmb(a, b_bf16):
        return jnp.dot(a.astype(BF16), b_bf16, preferred_element_type=F32)

    items = []
    for ci in range(n_chunks):
        c = (n_chunks - 1 - ci) if reverse else ci
        items.append({"rows": slice(c * t_len, (c + 1) * t_len)})

    for it in items:
        lw = lw_ref[it["rows"], :]
        lw_hi = lw.astype(BF16)
        lw_lo = (lw - lw_hi.astype(F32)).astype(BF16)
        it["lw"] = lw
        cum2 = jnp.dot(tri, jnp.concatenate([lw_hi, lw_lo], axis=1), preferred_element_type=F32)
        it["cum"] = cum2[:, :gl] + cum2[:, gl:]
    yield

    kkr_all = k_ref[...] * k_k
    nrm_all = jnp.maximum(jnp.sqrt(jnp.dot((kkr_all * kkr_all).astype(BF16), ones_bd,
                                           preferred_element_type=F32)), 1e-12)
    kk_all = kkr_all / nrm_all

    for it in items:
        rows, cum, lw = it["rows"], it["cum"], it["lw"]
        a = a_ref[rows, :]
        kk = kk_all[rows]
        kd = k_ref[rows, :] * (1.0 + (a - 1.0) * k_a)
        b = kk * a
        tot = cum[0:1] if reverse else cum[t_len - 1:t_len]
        inv_p = jnp.exp(-cum)
        rt = r_ref[rows, :] * jnp.exp(cum)
        kkt = kk * jnp.exp(cum - lw)
        to_end = jnp.exp(tot - cum)
        it["decay"] = jnp.exp(tot)
        it["b_end"] = (b * to_end).astype(BF16)
        it["k_end"] = (kd * to_end).astype(BF16)
        it["rt"] = rt.astype(BF16)
        it["kkt_blk"] = blk(kkt)
        lhs = jnp.concatenate([kkt, rt], axis=0).astype(BF16)
        rhs = jnp.concatenate([blk(b * inv_p), blk(kd * inv_p)], axis=0)
        aa = lax.dot_general(lhs, rhs, NT, preferred_element_type=F32)
        it["a_ab"] = jnp.where(strict, aa[:t_len, :gt], 0.0)
        it["a_ak"] = jnp.where(strict, aa[:t_len, gt:], 0.0).astype(BF16)
        it["a_r"] = jnp.concatenate([jnp.where(incl, aa[t_len:, :gt], 0.0),
                                     jnp.where(incl, aa[t_len:, gt:], 0.0)], axis=1).astype(BF16)
    yield

    for it in items:
        it["x"] = eye_w - it["a_ab"]
        it["p"] = mmb(it["a_ab"], bdiag(it["a_ab"]))
    yield
    rounds = int(math.log2(t_len)) - 1
    for rnd in range(rounds):
        for it in items:
            p_bd = bdiag(it["p"])
            if rnd < rounds - 1:
                xp = mmb(jnp.concatenate([it["x"], it["p"]], axis=0), p_bd)
                it["x"] = it["x"] + xp[:t_len]
                it["p"] = xp[t_len:]
            else:
                it["x"] = it["x"] + mmb(it["x"], p_bd)
        yield

    for it in items:
        v = v_ref[it["rows"], :]
        it["v"] = v
        it["v_blk"] = blk(v)
        it["akv"] = jnp.dot(it["a_ak"], it["v_blk"], preferred_element_type=F32)
        it["x_b"] = it["x"].astype(BF16)
        it["w_k"] = jnp.dot(it["x_b"], it["kkt_blk"], preferred_element_type=F32)
    yield
    for it in items:
        it["u"] = jnp.dot(it["x_b"], blk(it["akv"]), preferred_element_type=F32)
    yield
    for it in items:
        m_full = lax.dot_general(it["w_k"].astype(BF16), it["b_end"], TN, preferred_element_type=F32)
        it["m"] = jnp.where(head_mask, m_full, 0.0).astype(BF16)
    yield
    for it in items:
        c_full = lax.dot_general(jnp.concatenate([-it["u"], it["v"]], axis=0).astype(BF16),
                                 jnp.concatenate([it["b_end"], it["k_end"]], axis=0),
                                 TN, preferred_element_type=F32)
        it["c"] = jnp.where(head_mask, c_full, 0.0)
        it["wr"] = jnp.concatenate([it["w_k"].astype(BF16), it["rt"]], axis=0)
    return items


def _staggered(first, second):
    results = [None, None]
    live = [first, second]
    next(first)
    while any(g is not None for g in live):
        for idx in (1, 0):
            if live[idx] is not None:
                try:
                    next(live[idx])
                except StopIteration as stop:
                    results[idx] = stop.value
                    live[idx] = None
    return results


def _wkv2_kernel(rf_ref, kf_ref, vf_ref, twf_ref, taf_ref, rb_ref, kb_ref, vb_ref, twb_ref, tab_ref,
                 w2_ref, a2_ref, par_ref, yf_ref, yb_ref, bonus_ref, stf_ref, stb_ref):
    @pl.when(pl.program_id(2) == 0)
    def _():
        stf_ref[...] = jnp.zeros(stf_ref.shape, F32)
        stb_ref[...] = jnp.zeros(stb_ref.shape, F32)

    gl = stf_ref.shape[0]
    gt = (gl // RWKV_HEAD) * WKV_CHUNK
    lr = w2_ref.shape[0] // 2
    ones_bd = _head_ones(gl)
    k_k, k_a, r_k = par_ref[0:1, :], par_ref[1:2, :], par_ref[2:3, :]

    def second_stage(t_ref, w_ref, di, bias_row):
        lo = di * lr
        return (jnp.dot(t_ref[:, lo:lo + lr], w_ref[lo:lo + lr, :], preferred_element_type=F32)
                + par_ref[bias_row + di:bias_row + di + 1, :])

    lw_f = -math.exp(-0.5) * _sigmoid(second_stage(twf_ref, w2_ref, 0, 3))
    lw_b = -math.exp(-0.5) * _sigmoid(second_stage(twb_ref, w2_ref, 1, 3))
    a0_f = _sigmoid(second_stage(taf_ref, a2_ref, 0, 5))
    a1_f = _sigmoid(second_stage(taf_ref, a2_ref, 1, 5))
    a1_b = _sigmoid(second_stage(tab_ref, a2_ref, 1, 5))

    kd_sum = kf_ref[...] * (2.0 + (a0_f + a1_f - 2.0) * k_a)
    bonus_ref[...] = jnp.dot((rf_ref[...] * kd_sum * r_k).astype(BF16), ones_bd,
                             preferred_element_type=F32) * vf_ref[...]

    blk_mask = jnp.where(lax.broadcasted_iota(jnp.int32, (gt, gl), 0) // WKV_CHUNK
                         == lax.broadcasted_iota(jnp.int32, (gt, gl), 1) // RWKV_HEAD, 1.0, 0.0).astype(BF16)
    head_mask = (lax.broadcasted_iota(jnp.int32, (gl, gl), 0) // RWKV_HEAD
                 == lax.broadcasted_iota(jnp.int32, (gl, gl), 1) // RWKV_HEAD)

    def blk(x):
        return jnp.concatenate([x.astype(BF16)] * (gl // RWKV_HEAD), axis=0) * blk_mask

    t_len = WKV_CHUNK
    items_f, items_b = _staggered(
        _wkv_pre(rf_ref, kf_ref, vf_ref, a0_f, lw_f, k_k, k_a, ones_bd, blk, head_mask, gl, reverse=False),
        _wkv_pre(rb_ref, kb_ref, vb_ref, a1_b, lw_b, k_k, k_a, ones_bd, blk, head_mask, gl, reverse=True))
    chains = [(items_f, yf_ref, stf_ref), (items_b, yb_ref, stb_ref)]
    states = [st_ref[...] for _, _, st_ref in chains]

    def emit_y(y_ref, it, sa, rs):
        y_ref[it["rows"], :] = rs + jnp.dot(it["a_r"], jnp.concatenate([blk(sa), it["v_blk"]], axis=0),
                                            preferred_element_type=F32)

    pending = []
    for ci in range(len(chains[0][0])):
        st_bf = [st.astype(BF16) for st in states]
        its = [items[ci] for items, _, _ in chains]
        st_m = [jnp.dot(sb, it["m"], preferred_element_type=F32) for sb, it in zip(st_bf, its)]
        ws = [lax.dot_general(it["wr"], sb, NT, preferred_element_type=F32) for sb, it in zip(st_bf, its)]
        for args in pending:
            emit_y(*args)
        pending = [(y_ref, it, -(w[:t_len] + it["u"]), w[t_len:])
                   for (_, y_ref, _), it, w in zip(chains, its, ws)]
        states = [st * it["decay"] - sm + it["c"] for st, it, sm in zip(states, its, st_m)]
    for args in pending:
        emit_y(*args)
    for (_, _, st_ref), st in zip(chains, states):
        st_ref[...] = st


def _wkv2(r, k, v, tw, ta, w2_cat, a2_cat, par, batch, seq):
    m, d = r.shape
    gl = WKV_GROUP * RWKV_HEAD
    lr2 = tw.shape[1]
    tb = min(WKV_BLOCK, seq)
    nb = seq // tb
    fwd = pl.BlockSpec((tb, gl), lambda bi, g, j: (bi * nb + j, g))
    bwd = pl.BlockSpec((tb, gl), lambda bi, g, j: (bi * nb + nb - 1 - j, g))
    fwd_lr = pl.BlockSpec((tb, lr2), lambda bi, g, j: (bi * nb + j, 0))
    bwd_lr = pl.BlockSpec((tb, lr2), lambda bi, g, j: (bi * nb + nb - 1 - j, 0))
    col = lambda rows: pl.BlockSpec((rows, gl), lambda bi, g, j: (0, g))
    out = jax.ShapeDtypeStruct((m, d), F32)
    return pl.pallas_call(
        _wkv2_kernel,
        grid=(batch, d // gl, nb),
        in_specs=[fwd] * 3 + [fwd_lr] * 2 + [bwd] * 3 + [bwd_lr] * 2 + [col(lr2), col(lr2), col(par.shape[0])],
        out_specs=[fwd, bwd, fwd],
        out_shape=[out, out, out],
        scratch_shapes=[pltpu.VMEM((gl, gl), F32), pltpu.VMEM((gl, gl), F32)],
        compiler_params=_params("parallel", "parallel", "arbitrary"),
        name="wkv",
    )(r, k, v, tw, ta, r, k, v, tw, ta, w2_cat, a2_cat, par)


def _wkv_post_kernel(y0_ref, y1_ref, bonus_ref, sg_ref, g2_ref, lg_ref, lb_ref, o_ref):
    ones_bd = _head_ones(LANES)
    d = y0_ref.shape[1]
    inv_n = 1.0 / RWKV_HEAD
    gate = jnp.dot(sg_ref[...], g2_ref[...], preferred_element_type=F32)
    for c in range(d // LANES):
        sl = slice(c * LANES, (c + 1) * LANES)
        y = y0_ref[:, sl] + y1_ref[:, sl]
        mean = _head_sum(y, ones_bd) * inv_n
        yc = y - mean
        var = _head_sum(yc * yc, ones_bd) * inv_n
        yn = yc * lax.rsqrt(var + LNX_EPS) * lg_ref[:, sl] + lb_ref[:, sl]
        o_ref[:, sl] = ((yn + bonus_ref[:, sl]) * gate[:, sl]).astype(o_ref.dtype)


def _wkv_post(y0, y1, bonus, sg, g2, lnx_g, lnx_b):
    m, d = y0.shape
    lr = sg.shape[1]
    tm = min(256, m)
    row = pl.BlockSpec((tm, d), lambda i: (i, 0))
    par = pl.BlockSpec((1, d), lambda i: (0, 0))
    return pl.pallas_call(
        _wkv_post_kernel,
        grid=(m // tm,),
        in_specs=[row] * 3 + [pl.BlockSpec((tm, lr), lambda i: (i, 0)), pl.BlockSpec((lr, d), lambda i: (0, 0))]
        + [par] * 2,
        out_specs=row,
        out_shape=jax.ShapeDtypeStruct((m, d), BF16),
        compiler_params=_params("parallel"),
        name="wkv_post",
    )(y0, y1, bonus, sg, g2, lnx_g, lnx_b)


def _rope_tables(seq):
    half = ROPE_DIM // 2
    inv = 1.0 / (ROPE_THETA ** (jnp.arange(half, dtype=F32) * (2.0 / ROPE_DIM)))
    ang = jnp.arange(seq, dtype=F32)[:, None] * inv[None, :]
    cos, sin = jnp.cos(ang), jnp.sin(ang)
    zero = jnp.zeros((seq, LANES - ROPE_DIM), F32)
    return (jnp.concatenate([cos, cos, zero], axis=1), jnp.concatenate([-sin, sin, zero], axis=1))


def _row(v):
    return v.reshape(1, -1).astype(F32)


def _pad_cols(w, n):
    return jnp.pad(w, ((0, 0), (0, n - w.shape[1])))


def _pad_rows(w, n):
    return jnp.pad(w, ((0, n - w.shape[0]), (0, 0)))


def _mla_layer(x, batch, seq, g_pre, g_post, w_in, g_q, g_kv, w_uq, w_ukv, w_o):
    cosw, sinw = _rope_tables(seq)
    w_in_pad = _pad_cols(w_in, Q_LORA + KV_LORA + LANES).astype(BF16)
    w_uq_pad = jnp.pad(w_uq.reshape(Q_LORA, MLA_HEADS, NOPE_DIM + ROPE_DIM),
                       ((0, 0), (0, 0), (0, QK_PAD - NOPE_DIM - ROPE_DIM))).reshape(Q_LORA, MLA_HEADS * QK_PAD)
    cq, ckv, kpe = _mla_in(x, _row(g_pre), w_in_pad, _row(g_q), _row(g_kv), cosw, sinw, seq)
    qt = _mla_q(cq, w_uq_pad.astype(BF16), cosw, sinw, seq)
    k, vt = _mla_kv(ckv, kpe, w_ukv.astype(BF16))
    o = _attention(qt, k, vt, batch, seq)
    return _proj_res(o, w_o.astype(BF16), x, _row(g_post), "mla_out")


def _rwkv_layer(x, batch, seq, g_pre, g_post, mu, w_r, w_k, w_v, w_o, w0, w1, w2, a0, a1, a2, g1, g2,
                k_k, k_a, r_k, lnx_g, lnx_b):
    mu8 = jnp.pad(mu, ((0, 8 - mu.shape[0]), (0, 0)))
    w1_cat = jnp.concatenate([_pad_cols(w1[di], LANES) for di in range(2)], axis=1).astype(BF16)
    a1_cat = jnp.concatenate([_pad_cols(a1[di], LANES) for di in range(2)], axis=1).astype(BF16)
    w2_cat = jnp.concatenate([_pad_rows(w2[di], LANES) for di in range(2)], axis=0).astype(BF16)
    a2_cat = jnp.concatenate([_pad_rows(a2[di], LANES) for di in range(2)], axis=0).astype(BF16)
    par = jnp.stack([k_k, k_a, r_k, w0[0], w0[1], a0[0], a0[1], jnp.zeros_like(k_k)]).astype(F32)
    r, k, v, tw, ta, sg = _rwkv_mix(x, _row(g_pre), mu8, w_r.astype(BF16), w_k.astype(BF16), w_v.astype(BF16),
                                    w1_cat, a1_cat, g1.astype(BF16), seq)
    y0, y1, bonus = _wkv2(r, k, v, tw, ta, w2_cat, a2_cat, par, batch, seq)
    yg = _wkv_post(y0, y1, bonus, sg, g2.astype(BF16), _row(lnx_g), _row(lnx_b))
    return _proj_res(yg, w_o.astype(BF16), x, _row(g_post), "rwkv_out")


def _trunk(x3, norm_g, mla_w_in, mla_g_q, mla_g_kv, mla_w_uq, mla_w_ukv, mla_w_o,
           rwkv_mu, rwkv_w_r, rwkv_w_k, rwkv_w_v, rwkv_w_o, rwkv_w0, rwkv_w1, rwkv_w2,
           rwkv_a0, rwkv_a1, rwkv_a2, rwkv_g1, rwkv_g2, rwkv_k_k, rwkv_k_a, rwkv_r_k,
           rwkv_lnx_g, rwkv_lnx_b, ffn_w_gu, ffn_w_down):
    batch, seq, d = x3.shape
    x = x3.reshape(batch * seq, d)
    depth = norm_g.shape[0]
    for i in range(depth):
        j = i // 2
        if i % 2 == 0:
            x = _mla_layer(x, batch, seq, norm_g[i, 0], norm_g[i, 1], mla_w_in[j], mla_g_q[j], mla_g_kv[j],
                           mla_w_uq[j], mla_w_ukv[j], mla_w_o[j])
        else:
            x = _rwkv_layer(x, batch, seq, norm_g[i, 0], norm_g[i, 1], rwkv_mu[j], rwkv_w_r[j], rwkv_w_k[j],
                            rwkv_w_v[j], rwkv_w_o[j], rwkv_w0[j], rwkv_w1[j], rwkv_w2[j], rwkv_a0[j],
                            rwkv_a1[j], rwkv_a2[j], rwkv_g1[j], rwkv_g2[j], rwkv_k_k[j], rwkv_k_a[j],
                            rwkv_r_k[j].reshape(-1), rwkv_lnx_g[j], rwkv_lnx_b[j])
        x = _ffn(x, _row(norm_g[i, 2]), ffn_w_gu[i].astype(BF16), ffn_w_down[i].astype(BF16), _row(norm_g[i, 3]))
    return x.reshape(batch, seq, d)


def kernel(x_prompt, x_sample, norm_g, mla_w_in, mla_g_q, mla_g_kv, mla_w_uq, mla_w_ukv, mla_w_o, rwkv_mu, rwkv_w_r, rwkv_w_k, rwkv_w_v, rwkv_w_o, rwkv_w0, rwkv_w1, rwkv_w2, rwkv_a0, rwkv_a1, rwkv_a2, rwkv_g1, rwkv_g2, rwkv_k_k, rwkv_k_a, rwkv_r_k, rwkv_lnx_g, rwkv_lnx_b, ffn_w_gu, ffn_w_down):
    params = (norm_g, mla_w_in, mla_g_q, mla_g_kv, mla_w_uq, mla_w_ukv, mla_w_o,
              rwkv_mu, rwkv_w_r, rwkv_w_k, rwkv_w_v, rwkv_w_o, rwkv_w0, rwkv_w1, rwkv_w2,
              rwkv_a0, rwkv_a1, rwkv_a2, rwkv_g1, rwkv_g2, rwkv_k_k, rwkv_k_a, rwkv_r_k,
              rwkv_lnx_g, rwkv_lnx_b, ffn_w_gu, ffn_w_down)
    return (_trunk(x_prompt, *params), _trunk(x_sample, *params))
```

```python
import functools
import math

import jax
import jax.numpy as jnp
from jax import lax
from jax.experimental import pallas as pl
from jax.experimental.pallas import tpu as pltpu

F32 = jnp.float32
BF16 = jnp.bfloat16

NORM_EPS = 1e-6
LNX_EPS = 64e-5
ROPE_THETA = 10000.0

MLA_HEADS = 16
Q_LORA = 512
KV_LORA = 512
NOPE_DIM = 128
ROPE_DIM = 64
V_DIM = 128
QK_PAD = 256
ONES_ROWS = 16
RWKV_HEAD = 64

LANES = 128
VMEM_LIMIT = 56 * 1024 * 1024

WKV_CHUNK = 64
WKV_GROUP = 4
WKV_BLOCK = 512

NT = (((1,), (1,)), ((), ()))
TN = (((0,), (0,)), ((), ()))


def _params(*sem):
    return pltpu.CompilerParams(dimension_semantics=sem, vmem_limit_bytes=VMEM_LIMIT)


def _rms(x, g):
    return x * lax.rsqrt(jnp.mean(x * x, axis=-1, keepdims=True) + NORM_EPS) * g


def _sigmoid(z):
    return 1.0 / (1.0 + jnp.exp(-z))


def _rope_upper(up, cosw, sinw):
    lane = lax.broadcasted_iota(jnp.int32, up.shape, 1)
    swapped = jnp.where(lane < ROPE_DIM // 2, pltpu.roll(up, LANES - ROPE_DIM // 2, 1),
                        pltpu.roll(up, ROPE_DIM // 2, 1))
    return up * cosw + swapped * sinw


def _head_ones(n):
    r = lax.broadcasted_iota(jnp.int32, (n, n), 0) // RWKV_HEAD
    c = lax.broadcasted_iota(jnp.int32, (n, n), 1) // RWKV_HEAD
    return jnp.where(r == c, 1.0, 0.0).astype(BF16)


def _head_sum(z, ones_bd):
    hi = z.astype(BF16)
    lo = (z - hi.astype(F32)).astype(BF16)
    return (jnp.dot(hi, ones_bd, preferred_element_type=F32)
            + jnp.dot(lo, ones_bd, preferred_element_type=F32))


def _mla_in_kernel(x_ref, g_ref, w_ref, gq_ref, gkv_ref, cos_ref, sin_ref, cq_ref, ckv_ref, kpe_ref):
    xn = _rms(x_ref[...], g_ref[...]).astype(BF16)
    h = jnp.dot(xn, w_ref[...], preferred_element_type=F32)
    cq_ref[...] = _rms(h[:, :Q_LORA], gq_ref[...]).astype(BF16)
    ckv_ref[...] = _rms(h[:, Q_LORA:Q_LORA + KV_LORA], gkv_ref[...]).astype(BF16)
    kpe_ref[...] = _rope_upper(h[:, Q_LORA + KV_LORA:], cos_ref[...], sin_ref[...]).astype(BF16)


def _mla_in(x, g, w_in_pad, g_q, g_kv, cosw, sinw, seq):
    m, d = x.shape
    tm = min(512, seq)
    nseq = seq // tm
    n = w_in_pad.shape[1]
    row = lambda i: (i, 0)
    fix = lambda i: (0, 0)
    tab = lambda i: (i % nseq, 0)
    return pl.pallas_call(
        _mla_in_kernel,
        grid=(m // tm,),
        in_specs=[pl.BlockSpec((tm, d), row), pl.BlockSpec((1, d), fix), pl.BlockSpec((d, n), fix),
                  pl.BlockSpec((1, Q_LORA), fix), pl.BlockSpec((1, KV_LORA), fix),
                  pl.BlockSpec((tm, LANES), tab), pl.BlockSpec((tm, LANES), tab)],
        out_specs=[pl.BlockSpec((tm, Q_LORA), row), pl.BlockSpec((tm, KV_LORA), row),
                   pl.BlockSpec((tm, LANES), row)],
        out_shape=[jax.ShapeDtypeStruct((m, Q_LORA), BF16), jax.ShapeDtypeStruct((m, KV_LORA), BF16),
                   jax.ShapeDtypeStruct((m, LANES), BF16)],
        compiler_params=_params("parallel"),
        name="mla_in",
    )(x, g, w_in_pad, g_q, g_kv, cosw, sinw)


def _mla_q_kernel(cq_ref, w_ref, cos_ref, sin_ref, q_ref, *, scale):
    res = jnp.dot(cq_ref[...], w_ref[...], preferred_element_type=F32)
    cosw = cos_ref[...]
    sinw = sin_ref[...]
    for h in range(MLA_HEADS):
        lo = res[:, h * QK_PAD:h * QK_PAD + NOPE_DIM]
        up = _rope_upper(res[:, h * QK_PAD + NOPE_DIM:(h + 1) * QK_PAD], cosw, sinw)
        q_ref[h * QK_PAD:h * QK_PAD + NOPE_DIM, :] = (lo * scale).T.astype(BF16)
        q_ref[h * QK_PAD + NOPE_DIM:(h + 1) * QK_PAD, :] = (up * scale).T.astype(BF16)


def _mla_q(cq, w_uq_pad, cosw, sinw, seq):
    m, c = cq.shape
    n = w_uq_pad.shape[1]
    tm = min(256, seq)
    nseq = seq // tm
    scale = (NOPE_DIM + ROPE_DIM) ** -0.5 * math.log2(math.e)
    return pl.pallas_call(
        functools.partial(_mla_q_kernel, scale=scale),
        grid=(m // tm,),
        in_specs=[pl.BlockSpec((tm, c), lambda i: (i, 0)), pl.BlockSpec((c, n), lambda i: (0, 0)),
                  pl.BlockSpec((tm, LANES), lambda i: (i % nseq, 0)),
                  pl.BlockSpec((tm, LANES), lambda i: (i % nseq, 0))],
        out_specs=pl.BlockSpec((n, tm), lambda i: (0, i)),
        out_shape=jax.ShapeDtypeStruct((n, m), BF16),
        compiler_params=_params("parallel"),
        name="mla_q",
    )(cq, w_uq_pad, cosw, sinw)


def _mla_kv_kernel(ckv_ref, kpe_ref, w_ref, k_ref, vt_ref):
    res = jnp.dot(ckv_ref[...], w_ref[...], preferred_element_type=F32)
    kpe = kpe_ref[...]
    for h in range(MLA_HEADS):
        k_ref[h, :, :NOPE_DIM] = res[:, 2 * h * NOPE_DIM:(2 * h + 1) * NOPE_DIM].astype(BF16)
        k_ref[h, :, NOPE_DIM:] = kpe
        vt_ref[h * V_DIM:(h + 1) * V_DIM, :] = res[:, (2 * h + 1) * V_DIM:(2 * h + 2) * V_DIM].T.astype(BF16)


def _mla_kv(ckv, kpe, w_ukv):
    m, c = ckv.shape
    n = w_ukv.shape[1]
    tm = min(256, m)
    return pl.pallas_call(
        _mla_kv_kernel,
        grid=(m // tm,),
        in_specs=[pl.BlockSpec((tm, c), lambda i: (i, 0)), pl.BlockSpec((tm, LANES), lambda i: (i, 0)),
                  pl.BlockSpec((c, n), lambda i: (0, 0))],
        out_specs=[pl.BlockSpec((MLA_HEADS, tm, QK_PAD), lambda i: (0, i, 0)),
                   pl.BlockSpec((n // 2, tm), lambda i: (0, i))],
        out_shape=[jax.ShapeDtypeStruct((MLA_HEADS, m, QK_PAD), BF16), jax.ShapeDtypeStruct((n // 2, m), BF16)],
        compiler_params=_params("parallel"),
        name="mla_kv",
    )(ckv, kpe, w_ukv)


def _attn_kernel(qt_ref, qtn_ref, k0_ref, k_ref, vt_ref, o_ref, sa_sc, sb_sc, xa_sc, xb_sc, m_sc, acc_sc,
                 *, n_split, n_kv):
    i = pl.program_id(2)
    j = pl.program_id(3)
    width = qt_ref.shape[1] // n_split
    last = n_kv - 1
    carry = n_kv % 2 == 0
    buf_a = (sa_sc, xa_sc)
    buf_b = (sb_sc, xb_sc)

    def score(k, q_ref, dst, cols):
        s = jnp.dot(k, q_ref[:, cols], preferred_element_type=F32)
        dst[0][:, cols] = s
        dst[1][:, cols] = jnp.max(s, axis=0, keepdims=True)

    @pl.when(j == 0)
    def _():
        m_sc[...] = jnp.full(m_sc.shape, -jnp.inf, F32)
        acc_sc[...] = jnp.zeros(acc_sc.shape, F32)

    @pl.when((j == 0) & (i == 0) if carry else (j == 0))
    def _():
        k0 = k0_ref[0]
        for c in range(n_split):
            score(k0, qt_ref, buf_a, slice(c * width, (c + 1) * width))

    def step(cur, nxt, k_next_ref, q_next_ref):
        k = k_next_ref[0]
        vt = vt_ref[...]
        vt1 = jnp.concatenate([vt, jnp.ones((ONES_ROWS, vt.shape[1]), BF16)], axis=0)
        for c in range(n_split):
            cols = slice(c * width, (c + 1) * width)
            if nxt is not None:
                score(k, q_next_ref, nxt, cols)
            m_prev = m_sc[:, cols]
            m_new = jnp.maximum(m_prev, cur[1][:, cols])
            alpha = jnp.exp2(m_prev - m_new)
            pt = jnp.exp2((cur[0][:, cols] - m_new).astype(BF16))
            acc_sc[:, cols] = alpha * acc_sc[:, cols] + jnp.dot(vt1, pt, preferred_element_type=F32)
            m_sc[:, cols] = m_new

    @pl.when((j % 2 == 0) & (j < last))
    def _():
        step(buf_a, buf_b, k_ref, qt_ref)

    @pl.when((j % 2 == 1) & (j < last))
    def _():
        step(buf_b, buf_a, k_ref, qt_ref)

    @pl.when(j == last)
    def _():
        if carry:
            step(buf_b, buf_a, k0_ref, qtn_ref)
        else:
            step(buf_a, None, k_ref, qt_ref)
        o_ref[...] = (acc_sc[:V_DIM, :] / acc_sc[V_DIM:V_DIM + 1, :]).T.astype(o_ref.dtype)


def _attention(qt, k, vt, batch, seq):
    m = k.shape[1]
    tq = min(2048, seq)
    tk = min(1024, seq)
    nq = seq // tq
    nk = seq // tk
    return pl.pallas_call(
        functools.partial(_attn_kernel, n_split=max(tq // 256, 1), n_kv=nk),
        grid=(batch, MLA_HEADS, nq, nk),
        in_specs=[pl.BlockSpec((QK_PAD, tq), lambda b, h, i, j: (h, b * nq + i)),
                  pl.BlockSpec((QK_PAD, tq), lambda b, h, i, j: (h, b * nq + jnp.minimum(i + 1, nq - 1))),
                  pl.BlockSpec((1, tk, QK_PAD), lambda b, h, i, j: (h, b * nk, 0)),
                  pl.BlockSpec((1, tk, QK_PAD), lambda b, h, i, j: (h, b * nk + jnp.minimum(j + 1, nk - 1), 0)),
                  pl.BlockSpec((V_DIM, tk), lambda b, h, i, j: (h, b * nk + j))],
        out_specs=pl.BlockSpec((tq, V_DIM), lambda b, h, i, j: (b * nq + i, h)),
        out_shape=jax.ShapeDtypeStruct((m, MLA_HEADS * V_DIM), BF16),
        scratch_shapes=[pltpu.VMEM((tk, tq), F32), pltpu.VMEM((tk, tq), F32),
                        pltpu.VMEM((1, tq), F32), pltpu.VMEM((1, tq), F32),
                        pltpu.VMEM((1, tq), F32), pltpu.VMEM((V_DIM + ONES_ROWS, tq), F32)],
        compiler_params=_params("parallel", "parallel", "arbitrary", "arbitrary"),
        name="mla_attention",
    )(qt, qt, k, k, vt)


def _proj_res_kernel(a_ref, w_ref, x_ref, g_ref, o_ref):
    h = jnp.dot(a_ref[...], w_ref[...], preferred_element_type=F32)
    o_ref[...] = x_ref[...] + _rms(h, g_ref[...])


def _proj_res(a, w, x, g, name):
    m, kdim = a.shape
    n = w.shape[1]
    tm = min(512, m)
    return pl.pallas_call(
        _proj_res_kernel,
        grid=(m // tm,),
        in_specs=[pl.BlockSpec((tm, kdim), lambda i: (i, 0)), pl.BlockSpec((kdim, n), lambda i: (0, 0)),
                  pl.BlockSpec((tm, n), lambda i: (i, 0)), pl.BlockSpec((1, n), lambda i: (0, 0))],
        out_specs=pl.BlockSpec((tm, n), lambda i: (i, 0)),
        out_shape=jax.ShapeDtypeStruct((m, n), F32),
        compiler_params=_params("parallel"),
        name=name,
    )(a, w, x, g)


def _ffn_kernel(x_ref, g_pre_ref, wg_ref, wu_ref, wd_ref, g_post_ref, o_ref, xn_sc, acc_sc):
    f = pl.program_id(1)

    @pl.when(f == 0)
    def _():
        xn_sc[...] = _rms(x_ref[...], g_pre_ref[...]).astype(BF16)
        acc_sc[...] = jnp.zeros(acc_sc.shape, F32)

    xn = xn_sc[...]
    half = wg_ref.shape[1] // 2
    acts = []
    for c in range(2):
        cols = slice(c * half, (c + 1) * half)
        gate = jnp.dot(xn, wg_ref[:, cols], preferred_element_type=F32)
        up = jnp.dot(xn, wu_ref[:, cols], preferred_element_type=F32)
        acts.append((gate * _sigmoid(gate) * up).astype(BF16))
    acc_sc[...] += jnp.dot(jnp.concatenate(acts, axis=1), wd_ref[...], preferred_element_type=F32)

    @pl.when(f == pl.num_programs(1) - 1)
    def _():
        o_ref[...] = x_ref[...] + _rms(acc_sc[...], g_post_ref[...])


def _ffn(x, g_pre, w_gu, w_down, g_post):
    m, d = x.shape
    d_ff = w_down.shape[0]
    tm = min(512, m)
    tf = 512
    nf = d_ff // tf
    return pl.pallas_call(
        _ffn_kernel,
        grid=(m // tm, nf),
        in_specs=[pl.BlockSpec((tm, d), lambda i, f: (i, 0)), pl.BlockSpec((1, d), lambda i, f: (0, 0)),
                  pl.BlockSpec((d, tf), lambda i, f: (0, f)), pl.BlockSpec((d, tf), lambda i, f: (0, f + nf)),
                  pl.BlockSpec((tf, d), lambda i, f: (f, 0)), pl.BlockSpec((1, d), lambda i, f: (0, 0))],
        out_specs=pl.BlockSpec((tm, d), lambda i, f: (i, 0)),
        out_shape=jax.ShapeDtypeStruct((m, d), F32),
        scratch_shapes=[pltpu.VMEM((tm, d), BF16), pltpu.VMEM((tm, d), F32)],
        compiler_params=_params("parallel", "arbitrary"),
        name="ffn",
    )(x, g_pre, w_gu, w_gu, w_down, g_post)


def _rwkv_mix_kernel(x_ref, xp_ref, xn_ref, g_ref, mu_ref, wr_ref, wk_ref, wv_ref, w1_ref, a1_ref, g1_ref,
                     r_ref, k_ref, v_ref, tw_ref, ta_ref, sg_ref, *, tiles_per_seq):
    i = pl.program_id(0)
    g = g_ref[...]
    h = _rms(x_ref[...], g)
    tm = h.shape[0]
    first = (i % tiles_per_seq) == 0
    last = (i % tiles_per_seq) == tiles_per_seq - 1
    hp = jnp.where(first, 0.0, _rms(xp_ref[7:8, :], g))
    hn = jnp.where(last, 0.0, _rms(xn_ref[0:1, :], g))
    row = lax.broadcasted_iota(jnp.int32, h.shape, 0)
    h_prev = jnp.where(row == 0, hp, pltpu.roll(h, 1, 0))
    h_next = jnp.where(row == tm - 1, hn, pltpu.roll(h, tm - 1, 0))
    xx = 0.5 * (h_prev + h_next) - h

    def lerp(idx):
        return (h + xx * mu_ref[idx:idx + 1, :]).astype(BF16)

    r_ref[...] = jnp.dot(lerp(0), wr_ref[...], preferred_element_type=F32)
    k_ref[...] = jnp.dot(lerp(2), wk_ref[...], preferred_element_type=F32)
    v_ref[...] = jnp.dot(lerp(3), wv_ref[...], preferred_element_type=F32)
    tw_ref[...] = jnp.tanh(jnp.dot(lerp(1), w1_ref[...], preferred_element_type=F32)).astype(BF16)
    ta_ref[...] = jnp.dot(lerp(4), a1_ref[...], preferred_element_type=F32).astype(BF16)
    sg_ref[...] = _sigmoid(jnp.dot(lerp(5), g1_ref[...], preferred_element_type=F32)).astype(BF16)


def _rwkv_mix(x, g, mu, w_r, w_k, w_v, w1_cat, a1_cat, g1, seq):
    m, d = x.shape
    lr = w1_cat.shape[1]
    tm = min(256, seq)
    tps = seq // tm
    nb8 = m // 8
    r8 = tm // 8
    row = lambda i: (i, 0)
    fix = lambda i: (0, 0)
    weight = pl.BlockSpec((d, d), fix, pipeline_mode=pl.Buffered(1))
    weight_lr = pl.BlockSpec((d, lr), fix, pipeline_mode=pl.Buffered(1))
    return pl.pallas_call(
        functools.partial(_rwkv_mix_kernel, tiles_per_seq=tps),
        grid=(m // tm,),
        in_specs=[pl.BlockSpec((tm, d), row),
                  pl.BlockSpec((8, d), lambda i: (jnp.maximum(i * r8 - 1, 0), 0)),
                  pl.BlockSpec((8, d), lambda i: (jnp.minimum((i + 1) * r8, nb8 - 1), 0)),
                  pl.BlockSpec((1, d), fix), pl.BlockSpec((8, d), fix), weight, weight, weight,
                  weight_lr, weight_lr, weight_lr],
        out_specs=[pl.BlockSpec((tm, d), row)] * 3 + [pl.BlockSpec((tm, lr), row)] * 3,
        out_shape=[jax.ShapeDtypeStruct((m, d), F32)] * 3 + [jax.ShapeDtypeStruct((m, lr), BF16)] * 3,
        compiler_params=_params("parallel"),
        name="rwkv_mix",
    )(x, x, x, g, mu, w_r, w_k, w_v, w1_cat, a1_cat, g1)


def _wkv_pre(r_ref, k_ref, v_ref, a_ref, lw_ref, k_k, k_a, ones_bd, blk, head_mask, gl, *, reverse):
    t_len = WKV_CHUNK
    ng = gl // RWKV_HEAD
    gt = ng * t_len
    n_chunks = r_ref.shape[0] // t_len

    wrow = lax.broadcasted_iota(jnp.int32, (t_len, gt), 0)
    wcol = lax.broadcasted_iota(jnp.int32, (t_len, gt), 1) % t_len
    strict = (wcol > wrow) if reverse else (wcol < wrow)
    incl = (wcol >= wrow) if reverse else (wcol <= wrow)
    eye_w = jnp.where(wcol == wrow, 1.0, 0.0)
    trow = lax.broadcasted_iota(jnp.int32, (t_len, t_len), 0)
    tcol = lax.broadcasted_iota(jnp.int32, (t_len, t_len), 1)
    tri = jnp.where((tcol >= trow) if reverse else (tcol <= trow), 1.0, 0.0).astype(BF16)
    bd_mask = jnp.where(lax.broadcasted_iota(jnp.int32, (gt, gt), 0) // t_len
                        == lax.broadcasted_iota(jnp.int32, (gt, gt), 1) // t_len, 1.0, 0.0).astype(BF16)

    def bdiag(w):
        return jnp.concatenate([w.astype(BF16)] * ng, axis=0) * bd_mask

    def mmb(a, b_bf16):
        return jnp.dot(a.astype(BF16), b_bf16, preferred_element_type=F32)

    items = []
    for ci in range(n_chunks):
        c = (n_chunks - 1 - ci) if reverse else ci
        items.append({"rows": slice(c * t_len, (c + 1) * t_len)})

    for it in items:
        lw = lw_ref[it["rows"], :]
        lw_hi = lw.astype(BF16)
        lw_lo = (lw - lw_hi.astype(F32)).astype(BF16)
        it["lw"] = lw
        cum2 = jnp.dot(tri, jnp.concatenate([lw_hi, lw_lo], axis=1), preferred_element_type=F32)
        it["cum"] = cum2[:, :gl] + cum2[:, gl:]
    yield

    kkr_all = k_ref[...] * k_k
    nrm_all = jnp.maximum(jnp.sqrt(jnp.dot((kkr_all * kkr_all).astype(BF16), ones_bd,
                                           preferred_element_type=F32)), 1e-12)
    kk_all = kkr_all / nrm_all

    for it in items:
        rows, cum, lw = it["rows"], it["cum"], it["lw"]
        a = a_ref[rows, :]
        kk = kk_all[rows]
        kd = k_ref[rows, :] * (1.0 + (a - 1.0) * k_a)
        b = kk * a
        tot = cum[0:1] if reverse else cum[t_len - 1:t_len]
        inv_p = jnp.exp(-cum)
        rt = r_ref[rows, :] * jnp.exp(cum)
        kkt = kk * jnp.exp(cum - lw)
        to_end = jnp.exp(tot - cum)
        it["decay"] = jnp.exp(tot)
        it["b_end"] = (b * to_end).astype(BF16)
        it["k_end"] = (kd * to_end).astype(BF16)
        it["rt"] = rt.astype(BF16)
        it["kkt_blk"] = blk(kkt)
        lhs = jnp.concatenate([kkt, rt], axis=0).astype(BF16)
        rhs = jnp.concatenate([blk(b * inv_p), blk(kd * inv_p)], axis=0)
        aa = lax.dot_general(lhs, rhs, NT, preferred_element_type=F32)
        it["a_ab"] = jnp.where(strict, aa[:t_len, :gt], 0.0)
        it["a_ak"] = jnp.where(strict, aa[:t_len, gt:], 0.0).astype(BF16)
        it["a_r"] = jnp.concatenate([jnp.where(incl, aa[t_len:, :gt], 0.0),
                                     jnp.where(incl, aa[t_len:, gt:], 0.0)], axis=1).astype(BF16)
    yield

    for it in items:
        it["x"] = eye_w - it["a_ab"]
        it["p"] = mmb(it["a_ab"], bdiag(it["a_ab"]))
    yield
    rounds = int(math.log2(t_len)) - 1
    for rnd in range(rounds):
        for it in items:
            p_bd = bdiag(it["p"])
            if rnd < rounds - 1:
                xp = mmb(jnp.concatenate([it["x"], it["p"]], axis=0), p_bd)
                it["x"] = it["x"] + xp[:t_len]
                it["p"] = xp[t_len:]
            else:
                it["x"] = it["x"] + mmb(it["x"], p_bd)
        yield

    for it in items:
        v = v_ref[it["rows"], :]
        it["v"] = v
        it["v_blk"] = blk(v)
        it["akv"] = jnp.dot(it["a_ak"], it["v_blk"], preferred_element_type=F32)
        it["x_b"] = it["x"].astype(BF16)
        it["w_k"] = jnp.dot(it["x_b"], it["kkt_blk"], preferred_element_type=F32)
    yield
    for it in items:
        it["u"] = jnp.dot(it["x_b"], blk(it["akv"]), preferred_element_type=F32)
    yield
    for it in items:
        m_full = lax.dot_general(it["w_k"].astype(BF16), it["b_end"], TN, preferred_element_type=F32)
        it["m"] = jnp.where(head_mask, m_full, 0.0).astype(BF16)
    yield
    for it in items:
        c_full = lax.dot_general(jnp.concatenate([-it["u"], it["v"]], axis=0).astype(BF16),
                                 jnp.concatenate([it["b_end"], it["k_end"]], axis=0),
                                 TN, preferred_element_type=F32)
        it["c"] = jnp.where(head_mask, c_full, 0.0)
        it["wr"] = jnp.concatenate([it["w_k"].astype(BF16), it["rt"]], axis=0)
    return items


def _staggered(first, second):
    results = [None, None]
    live = [first, second]
    next(first)
    while any(g is not None for g in live):
        for idx in (1, 0):
            if live[idx] is not None:
                try:
                    next(live[idx])
                except StopIteration as stop:
                    results[idx] = stop.value
                    live[idx] = None
    return results


def _wkv2_kernel(rf_ref, kf_ref, vf_ref, twf_ref, taf_ref, rb_ref, kb_ref, vb_ref, twb_ref, tab_ref,
                 w2_ref, a2_ref, par_ref, yf_ref, yb_ref, bonus_ref, stf_ref, stb_ref):
    @pl.when(pl.program_id(2) == 0)
    def _():
        stf_ref[...] = jnp.zeros(stf_ref.shape, F32)
        stb_ref[...] = jnp.zeros(stb_ref.shape, F32)

    gl = stf_ref.shape[0]
    gt = (gl // RWKV_HEAD) * WKV_CHUNK
    lr = w2_ref.shape[0] // 2
    ones_bd = _head_ones(gl)
    k_k, k_a, r_k = par_ref[0:1, :], par_ref[1:2, :], par_ref[2:3, :]

    def second_stage(t_ref, w_ref, di, bias_row):
        lo = di * lr
        return (jnp.dot(t_ref[:, lo:lo + lr], w_ref[lo:lo + lr, :], preferred_element_type=F32)
                + par_ref[bias_row + di:bias_row + di + 1, :])

    lw_f = -math.exp(-0.5) * _sigmoid(second_stage(twf_ref, w2_ref, 0, 3))
    lw_b = -math.exp(-0.5) * _sigmoid(second_stage(twb_ref, w2_ref, 1, 3))
    a0_f = _sigmoid(second_stage(taf_ref, a2_ref, 0, 5))
    a1_f = _sigmoid(second_stage(taf_ref, a2_ref, 1, 5))
    a1_b = _sigmoid(second_stage(tab_ref, a2_ref, 1, 5))

    kd_sum = kf_ref[...] * (2.0 + (a0_f + a1_f - 2.0) * k_a)
    bonus_ref[...] = jnp.dot((rf_ref[...] * kd_sum * r_k).astype(BF16), ones_bd,
                             preferred_element_type=F32) * vf_ref[...]

    blk_mask = jnp.where(lax.broadcasted_iota(jnp.int32, (gt, gl), 0) // WKV_CHUNK
                         == lax.broadcasted_iota(jnp.int32, (gt, gl), 1) // RWKV_HEAD, 1.0, 0.0).astype(BF16)
    head_mask = (lax.broadcasted_iota(jnp.int32, (gl, gl), 0) // RWKV_HEAD
                 == lax.broadcasted_iota(jnp.int32, (gl, gl), 1) // RWKV_HEAD)

    def blk(x):
        return jnp.concatenate([x.astype(BF16)] * (gl // RWKV_HEAD), axis=0) * blk_mask

    t_len = WKV_CHUNK
    items_f, items_b = _staggered(
        _wkv_pre(rf_ref, kf_ref, vf_ref, a0_f, lw_f, k_k, k_a, ones_bd, blk, head_mask, gl, reverse=False),
        _wkv_pre(rb_ref, kb_ref, vb_ref, a1_b, lw_b, k_k, k_a, ones_bd, blk, head_mask, gl, reverse=True))
    chains = [(items_f, yf_ref, stf_ref), (items_b, yb_ref, stb_ref)]
    states = [st_ref[...] for _, _, st_ref in chains]

    def emit_y(y_ref, it, sa, rs):
        y_ref[it["rows"], :] = rs + jnp.dot(it["a_r"], jnp.concatenate([blk(sa), it["v_blk"]], axis=0),
                                            preferred_element_type=F32)

    pending = []
    for ci in range(len(chains[0][0])):
        st_bf = [st.astype(BF16) for st in states]
        its = [items[ci] for items, _, _ in chains]
        st_m = [jnp.dot(sb, it["m"], preferred_element_type=F32) for sb, it in zip(st_bf, its)]
        ws = [lax.dot_general(it["wr"], sb, NT, preferred_element_type=F32) for sb, it in zip(st_bf, its)]
        for args in pending:
            emit_y(*args)
        pending = [(y_ref, it, -(w[:t_len] + it["u"]), w[t_len:])
                   for (_, y_ref, _), it, w in zip(chains, its, ws)]
        states = [st * it["decay"] - sm + it["c"] for st, it, sm in zip(states, its, st_m)]
    for args in pending:
        emit_y(*args)
    for (_, _, st_ref), st in zip(chains, states):
        st_ref[...] = st


def _wkv2(r, k, v, tw, ta, w2_cat, a2_cat, par, batch, seq):
    m, d = r.shape
    gl = WKV_GROUP * RWKV_HEAD
    lr2 = tw.shape[1]
    tb = min(WKV_BLOCK, seq)
    nb = seq // tb
    fwd = pl.BlockSpec((tb, gl), lambda bi, g, j: (bi * nb + j, g))
    bwd = pl.BlockSpec((tb, gl), lambda bi, g, j: (bi * nb + nb - 1 - j, g))
    fwd_lr = pl.BlockSpec((tb, lr2), lambda bi, g, j: (bi * nb + j, 0))
    bwd_lr = pl.BlockSpec((tb, lr2), lambda bi, g, j: (bi * nb + nb - 1 - j, 0))
    col = lambda rows: pl.BlockSpec((rows, gl), lambda bi, g, j: (0, g))
    out = jax.ShapeDtypeStruct((m, d), F32)
    return pl.pallas_call(
        _wkv2_kernel,
        grid=(batch, d // gl, nb),
        in_specs=[fwd] * 3 + [fwd_lr] * 2 + [bwd] * 3 + [bwd_lr] * 2 + [col(lr2), col(lr2), col(par.shape[0])],
        out_specs=[fwd, bwd, fwd],
        out_shape=[out, out, out],
        scratch_shapes=[pltpu.VMEM((gl, gl), F32), pltpu.VMEM((gl, gl), F32)],
        compiler_params=_params("parallel", "parallel", "arbitrary"),
        name="wkv",
    )(r, k, v, tw, ta, r, k, v, tw, ta, w2_cat, a2_cat, par)


def _wkv_post_kernel(y0_ref, y1_ref, bonus_ref, sg_ref, g2_ref, lg_ref, lb_ref, o_ref):
    ones_bd = _head_ones(LANES)
    d = y0_ref.shape[1]
    inv_n = 1.0 / RWKV_HEAD
    gate = jnp.dot(sg_ref[...], g2_ref[...], preferred_element_type=F32)
    for c in range(d // LANES):
        sl = slice(c * LANES, (c + 1) * LANES)
        y = y0_ref[:, sl] + y1_ref[:, sl]
        mean = _head_sum(y, ones_bd) * inv_n
        yc = y - mean
        var = _head_sum(yc * yc, ones_bd) * inv_n
        yn = yc * lax.rsqrt(var + LNX_EPS) * lg_ref[:, sl] + lb_ref[:, sl]
        o_ref[:, sl] = ((yn + bonus_ref[:, sl]) * gate[:, sl]).astype(o_ref.dtype)


def _wkv_post(y0, y1, bonus, sg, g2, lnx_g, lnx_b):
    m, d = y0.shape
    lr = sg.shape[1]
    tm = min(256, m)
    row = pl.BlockSpec((tm, d), lambda i: (i, 0))
    par = pl.BlockSpec((1, d), lambda i: (0, 0))
    return pl.pallas_call(
        _wkv_post_kernel,
        grid=(m // tm,),
        in_specs=[row] * 3 + [pl.BlockSpec((tm, lr), lambda i: (i, 0)), pl.BlockSpec((lr, d), lambda i: (0, 0))]
        + [par] * 2,
        out_specs=row,
        out_shape=jax.ShapeDtypeStruct((m, d), BF16),
        compiler_params=_params("parallel"),
        name="wkv_post",
    )(y0, y1, bonus, sg, g2, lnx_g, lnx_b)


def _rope_tables(seq):
    half = ROPE_DIM // 2
    inv = 1.0 / (ROPE_THETA ** (jnp.arange(half, dtype=F32) * (2.0 / ROPE_DIM)))
    ang = jnp.arange(seq, dtype=F32)[:, None] * inv[None, :]
    cos, sin = jnp.cos(ang), jnp.sin(ang)
    zero = jnp.zeros((seq, LANES - ROPE_DIM), F32)
    return (jnp.concatenate([cos, cos, zero], axis=1), jnp.concatenate([-sin, sin, zero], axis=1))


def _row(v):
    return v.reshape(1, -1).astype(F32)


def _pad_cols(w, n):
    return jnp.pad(w, ((0, 0), (0, n - w.shape[1])))


def _pad_rows(w, n):
    return jnp.pad(w, ((0, n - w.shape[0]), (0, 0)))


def _mla_layer(x, batch, seq, g_pre, g_post, w_in, g_q, g_kv, w_uq, w_ukv, w_o):
    cosw, sinw = _rope_tables(seq)
    w_in_pad = _pad_cols(w_in, Q_LORA + KV_LORA + LANES).astype(BF16)
    w_uq_pad = jnp.pad(w_uq.reshape(Q_LORA, MLA_HEADS, NOPE_DIM + ROPE_DIM),
                       ((0, 0), (0, 0), (0, QK_PAD - NOPE_DIM - ROPE_DIM))).reshape(Q_LORA, MLA_HEADS * QK_PAD)
    cq, ckv, kpe = _mla_in(x, _row(g_pre), w_in_pad, _row(g_q), _row(g_kv), cosw, sinw, seq)
    qt = _mla_q(cq, w_uq_pad.astype(BF16), cosw, sinw, seq)
    k, vt = _mla_kv(ckv, kpe, w_ukv.astype(BF16))
    o = _attention(qt, k, vt, batch, seq)
    return _proj_res(o, w_o.astype(BF16), x, _row(g_post), "mla_out")


def _rwkv_layer(x, batch, seq, g_pre, g_post, mu, w_r, w_k, w_v, w_o, w0, w1, w2, a0, a1, a2, g1, g2,
                k_k, k_a, r_k, lnx_g, lnx_b):
    mu8 = jnp.pad(mu, ((0, 8 - mu.shape[0]), (0, 0)))
    w1_cat = jnp.concatenate([_pad_cols(w1[di], LANES) for di in range(2)], axis=1).astype(BF16)
    a1_cat = jnp.concatenate([_pad_cols(a1[di], LANES) for di in range(2)], axis=1).astype(BF16)
    w2_cat = jnp.concatenate([_pad_rows(w2[di], LANES) for di in range(2)], axis=0).astype(BF16)
    a2_cat = jnp.concatenate([_pad_rows(a2[di], LANES) for di in range(2)], axis=0).astype(BF16)
    par = jnp.stack([k_k, k_a, r_k, w0[0], w0[1], a0[0], a0[1], jnp.zeros_like(k_k)]).astype(F32)
    r, k, v, tw, ta, sg = _rwkv_mix(x, _row(g_pre), mu8, w_r.astype(BF16), w_k.astype(BF16), w_v.astype(BF16),
                                    w1_cat, a1_cat, g1.astype(BF16), seq)
    y0, y1, bonus = _wkv2(r, k, v, tw, ta, w2_cat, a2_cat, par, batch, seq)
    yg = _wkv_post(y0, y1, bonus, sg, g2.astype(BF16), _row(lnx_g), _row(lnx_b))
    return _proj_res(yg, w_o.astype(BF16), x, _row(g_post), "rwkv_out")


def _trunk(x3, norm_g, mla_w_in, mla_g_q, mla_g_kv, mla_w_uq, mla_w_ukv, mla_w_o,
           rwkv_mu, rwkv_w_r, rwkv_w_k, rwkv_w_v, rwkv_w_o, rwkv_w0, rwkv_w1, rwkv_w2,
           rwkv_a0, rwkv_a1, rwkv_a2, rwkv_g1, rwkv_g2, rwkv_k_k, rwkv_k_a, rwkv_r_k,
           rwkv_lnx_g, rwkv_lnx_b, ffn_w_gu, ffn_w_down):
    batch, seq, d = x3.shape
    x = x3.reshape(batch * seq, d)
    depth = norm_g.shape[0]
    for i in range(depth):
        j = i // 2
        if i % 2 == 0:
            x = _mla_layer(x, batch, seq, norm_g[i, 0], norm_g[i, 1], mla_w_in[j], mla_g_q[j], mla_g_kv[j],
                           mla_w_uq[j], mla_w_ukv[j], mla_w_o[j])
        else:
            x = _rwkv_layer(x, batch, seq, norm_g[i, 0], norm_g[i, 1], rwkv_mu[j], rwkv_w_r[j], rwkv_w_k[j],
                            rwkv_w_v[j], rwkv_w_o[j], rwkv_w0[j], rwkv_w1[j], rwkv_w2[j], rwkv_a0[j],
                            rwkv_a1[j], rwkv_a2[j], rwkv_g1[j], rwkv_g2[j], rwkv_k_k[j], rwkv_k_a[j],
                            rwkv_r_k[j].reshape(-1), rwkv_lnx_g[j], rwkv_lnx_b[j])
        x = _ffn(x, _row(norm_g[i, 2]), ffn_w_gu[i].astype(BF16), ffn_w_down[i].astype(BF16), _row(norm_g[i, 3]))
    return x.reshape(batch, seq, d)


def kernel(x_prompt, x_sample, norm_g, mla_w_in, mla_g_q, mla_g_kv, mla_w_uq, mla_w_ukv, mla_w_o, rwkv_mu, rwkv_w_r, rwkv_w_k, rwkv_w_v, rwkv_w_o, rwkv_w0, rwkv_w1, rwkv_w2, rwkv_a0, rwkv_a1, rwkv_a2, rwkv_g1, rwkv_g2, rwkv_k_k, rwkv_k_a, rwkv_r_k, rwkv_lnx_g, rwkv_lnx_b, ffn_w_gu, ffn_w_down):
    params = (norm_g, mla_w_in, mla_g_q, mla_g_kv, mla_w_uq, mla_w_ukv, mla_w_o,
              rwkv_mu, rwkv_w_r, rwkv_w_k, rwkv_w_v, rwkv_w_o, rwkv_w0, rwkv_w1, rwkv_w2,
              rwkv_a0, rwkv_a1, rwkv_a2, rwkv_g1, rwkv_g2, rwkv_k_k, rwkv_k_a, rwkv_r_k,
              rwkv_lnx_g, rwkv_lnx_b, ffn_w_gu, ffn_w_down)
    return (_trunk(x_prompt, *params), _trunk(x_sample, *params))
```

```python
import functools
import math

import jax
import jax.numpy as jnp
from jax import lax
from jax.experimental import pallas as pl
from jax.experimental.pallas import tpu as pltpu

F32 = jnp.float32
BF16 = jnp.bfloat16

NORM_EPS = 1e-6
LNX_EPS = 64e-5
ROPE_THETA = 10000.0

MLA_HEADS = 16
Q_LORA = 512
KV_LORA = 512
NOPE_DIM = 128
ROPE_DIM = 64
V_DIM = 128
QK_PAD = 256
ONES_ROWS = 16
RWKV_HEAD = 64

LANES = 128
VMEM_LIMIT = 56 * 1024 * 1024

WKV_CHUNK = 64
WKV_GROUP = 4
WKV_BLOCK = 512

NT = (((1,), (1,)), ((), ()))
TN = (((0,), (0,)), ((), ()))


def _params(*sem):
    return pltpu.CompilerParams(dimension_semantics=sem, vmem_limit_bytes=VMEM_LIMIT)


def _rms(x, g):
    return x * lax.rsqrt(jnp.mean(x * x, axis=-1, keepdims=True) + NORM_EPS) * g


def _sigmoid(z):
    return 1.0 / (1.0 + jnp.exp(-z))


def _rope_upper(up, cosw, sinw):
    lane = lax.broadcasted_iota(jnp.int32, up.shape, 1)
    swapped = jnp.where(lane < ROPE_DIM // 2, pltpu.roll(up, LANES - ROPE_DIM // 2, 1),
                        pltpu.roll(up, ROPE_DIM // 2, 1))
    return up * cosw + swapped * sinw


def _head_ones(n):
    r = lax.broadcasted_iota(jnp.int32, (n, n), 0) // RWKV_HEAD
    c = lax.broadcasted_iota(jnp.int32, (n, n), 1) // RWKV_HEAD
    return jnp.where(r == c, 1.0, 0.0).astype(BF16)


def _mla_in_kernel(x_ref, g_ref, w_ref, gq_ref, gkv_ref, cos_ref, sin_ref, cq_ref, ckv_ref, kpe_ref):
    xn = _rms(x_ref[...], g_ref[...]).astype(BF16)
    h = jnp.dot(xn, w_ref[...], preferred_element_type=F32)
    cq_ref[...] = _rms(h[:, :Q_LORA], gq_ref[...]).astype(BF16)
    ckv_ref[...] = _rms(h[:, Q_LORA:Q_LORA + KV_LORA], gkv_ref[...]).astype(BF16)
    kpe_ref[...] = _rope_upper(h[:, Q_LORA + KV_LORA:], cos_ref[...], sin_ref[...]).astype(BF16)


def _mla_in(x, g, w_in_pad, g_q, g_kv, cosw, sinw, seq):
    m, d = x.shape
    tm = min(512, seq)
    nseq = seq // tm
    n = w_in_pad.shape[1]
    row = lambda i: (i, 0)
    fix = lambda i: (0, 0)
    tab = lambda i: (i % nseq, 0)
    return pl.pallas_call(
        _mla_in_kernel,
        grid=(m // tm,),
        in_specs=[pl.BlockSpec((tm, d), row), pl.BlockSpec((1, d), fix), pl.BlockSpec((d, n), fix),
                  pl.BlockSpec((1, Q_LORA), fix), pl.BlockSpec((1, KV_LORA), fix),
                  pl.BlockSpec((tm, LANES), tab), pl.BlockSpec((tm, LANES), tab)],
        out_specs=[pl.BlockSpec((tm, Q_LORA), row), pl.BlockSpec((tm, KV_LORA), row),
                   pl.BlockSpec((tm, LANES), row)],
        out_shape=[jax.ShapeDtypeStruct((m, Q_LORA), BF16), jax.ShapeDtypeStruct((m, KV_LORA), BF16),
                   jax.ShapeDtypeStruct((m, LANES), BF16)],
        compiler_params=_params("parallel"),
        name="mla_in",
    )(x, g, w_in_pad, g_q, g_kv, cosw, sinw)


def _mla_q_kernel(cq_ref, w_ref, cos_ref, sin_ref, q_ref, *, scale):
    res = jnp.dot(cq_ref[...], w_ref[...], preferred_element_type=F32)
    cosw = cos_ref[...]
    sinw = sin_ref[...]
    for h in range(MLA_HEADS):
        lo = res[:, h * QK_PAD:h * QK_PAD + NOPE_DIM]
        up = _rope_upper(res[:, h * QK_PAD + NOPE_DIM:(h + 1) * QK_PAD], cosw, sinw)
        q_ref[h * QK_PAD:h * QK_PAD + NOPE_DIM, :] = (lo * scale).T.astype(BF16)
        q_ref[h * QK_PAD + NOPE_DIM:(h + 1) * QK_PAD, :] = (up * scale).T.astype(BF16)


def _mla_q(cq, w_uq_pad, cosw, sinw, seq):
    m, c = cq.shape
    n = w_uq_pad.shape[1]
    tm = min(256, seq)
    nseq = seq // tm
    scale = (NOPE_DIM + ROPE_DIM) ** -0.5 * math.log2(math.e)
    return pl.pallas_call(
        functools.partial(_mla_q_kernel, scale=scale),
        grid=(m // tm,),
        in_specs=[pl.BlockSpec((tm, c), lambda i: (i, 0)), pl.BlockSpec((c, n), lambda i: (0, 0)),
                  pl.BlockSpec((tm, LANES), lambda i: (i % nseq, 0)),
                  pl.BlockSpec((tm, LANES), lambda i: (i % nseq, 0))],
        out_specs=pl.BlockSpec((n, tm), lambda i: (0, i)),
        out_shape=jax.ShapeDtypeStruct((n, m), BF16),
        compiler_params=_params("parallel"),
        name="mla_q",
    )(cq, w_uq_pad, cosw, sinw)


def _mla_kv_kernel(ckv_ref, kpe_ref, w_ref, k_ref, vt_ref):
    res = jnp.dot(ckv_ref[...], w_ref[...], preferred_element_type=F32)
    kpe = kpe_ref[...]
    for h in range(MLA_HEADS):
        k_ref[h, :, :NOPE_DIM] = res[:, 2 * h * NOPE_DIM:(2 * h + 1) * NOPE_DIM].astype(BF16)
        k_ref[h, :, NOPE_DIM:] = kpe
        vt_ref[h * V_DIM:(h + 1) * V_DIM, :] = res[:, (2 * h + 1) * V_DIM:(2 * h + 2) * V_DIM].T.astype(BF16)


def _mla_kv(ckv, kpe, w_ukv):
    m, c = ckv.shape
    n = w_ukv.shape[1]
    tm = min(256, m)
    return pl.pallas_call(
        _mla_kv_kernel,
        grid=(m // tm,),
        in_specs=[pl.BlockSpec((tm, c), lambda i: (i, 0)), pl.BlockSpec((tm, LANES), lambda i: (i, 0)),
                  pl.BlockSpec((c, n), lambda i: (0, 0))],
        out_specs=[pl.BlockSpec((MLA_HEADS, tm, QK_PAD), lambda i: (0, i, 0)),
                   pl.BlockSpec((n // 2, tm), lambda i: (0, i))],
        out_shape=[jax.ShapeDtypeStruct((MLA_HEADS, m, QK_PAD), BF16), jax.ShapeDtypeStruct((n // 2, m), BF16)],
        compiler_params=_params("parallel"),
        name="mla_kv",
    )(ckv, kpe, w_ukv)


def _attn_kernel(qt_ref, qtn_ref, k0_ref, k_ref, vt_ref, o_ref, sa_sc, sb_sc, xa_sc, xb_sc, m_sc, acc_sc,
                 *, n_split, n_kv):
    i = pl.program_id(2)
    j = pl.program_id(3)
    width = qt_ref.shape[1] // n_split
    last = n_kv - 1
    carry = n_kv % 2 == 0
    buf_a = (sa_sc, xa_sc)
    buf_b = (sb_sc, xb_sc)

    def score(k, q_ref, dst, cols):
        s = jnp.dot(k, q_ref[:, cols], preferred_element_type=F32)
        dst[0][:, cols] = s
        dst[1][:, cols] = jnp.max(s, axis=0, keepdims=True)

    @pl.when(j == 0)
    def _():
        m_sc[...] = jnp.full(m_sc.shape, -jnp.inf, F32)
        acc_sc[...] = jnp.zeros(acc_sc.shape, F32)

    @pl.when((j == 0) & (i == 0) if carry else (j == 0))
    def _():
        k0 = k0_ref[0]
        for c in range(n_split):
            score(k0, qt_ref, buf_a, slice(c * width, (c + 1) * width))

    def step(cur, nxt, k_next_ref, q_next_ref):
        k = k_next_ref[0]
        vt = vt_ref[...]
        vt1 = jnp.concatenate([vt, jnp.ones((ONES_ROWS, vt.shape[1]), BF16)], axis=0)
        for c in range(n_split):
            cols = slice(c * width, (c + 1) * width)
            if nxt is not None:
                score(k, q_next_ref, nxt, cols)
            m_prev = m_sc[:, cols]
            m_new = jnp.maximum(m_prev, cur[1][:, cols])
            alpha = jnp.exp2(m_prev - m_new)
            pt = jnp.exp2((cur[0][:, cols] - m_new).astype(BF16))
            acc_sc[:, cols] = alpha * acc_sc[:, cols] + jnp.dot(vt1, pt, preferred_element_type=F32)
            m_sc[:, cols] = m_new

    @pl.when((j % 2 == 0) & (j < last))
    def _():
        step(buf_a, buf_b, k_ref, qt_ref)

    @pl.when((j % 2 == 1) & (j < last))
    def _():
        step(buf_b, buf_a, k_ref, qt_ref)

    @pl.when(j == last)
    def _():
        if carry:
            step(buf_b, buf_a, k0_ref, qtn_ref)
        else:
            step(buf_a, None, k_ref, qt_ref)
        o_ref[...] = (acc_sc[:V_DIM, :] / acc_sc[V_DIM:V_DIM + 1, :]).T.astype(o_ref.dtype)


def _attention(qt, k, vt, batch, seq):
    m = k.shape[1]
    tq = min(2048, seq)
    tk = min(1024, seq)
    nq = seq // tq
    nk = seq // tk
    return pl.pallas_call(
        functools.partial(_attn_kernel, n_split=max(tq // 256, 1), n_kv=nk),
        grid=(batch, MLA_HEADS, nq, nk),
        in_specs=[pl.BlockSpec((QK_PAD, tq), lambda b, h, i, j: (h, b * nq + i)),
                  pl.BlockSpec((QK_PAD, tq), lambda b, h, i, j: (h, b * nq + jnp.minimum(i + 1, nq - 1))),
                  pl.BlockSpec((1, tk, QK_PAD), lambda b, h, i, j: (h, b * nk, 0)),
                  pl.BlockSpec((1, tk, QK_PAD), lambda b, h, i, j: (h, b * nk + jnp.minimum(j + 1, nk - 1), 0)),
                  pl.BlockSpec((V_DIM, tk), lambda b, h, i, j: (h, b * nk + j))],
        out_specs=pl.BlockSpec((tq, V_DIM), lambda b, h, i, j: (b * nq + i, h)),
        out_shape=jax.ShapeDtypeStruct((m, MLA_HEADS * V_DIM), BF16),
        scratch_shapes=[pltpu.VMEM((tk, tq), F32), pltpu.VMEM((tk, tq), F32),
                        pltpu.VMEM((1, tq), F32), pltpu.VMEM((1, tq), F32),
                        pltpu.VMEM((1, tq), F32), pltpu.VMEM((V_DIM + ONES_ROWS, tq), F32)],
        compiler_params=_params("parallel", "parallel", "arbitrary", "arbitrary"),
        name="mla_attention",
    )(qt, qt, k, k, vt)


def _proj_res_kernel(a_ref, w_ref, x_ref, g_ref, o_ref):
    h = jnp.dot(a_ref[...], w_ref[...], preferred_element_type=F32)
    o_ref[...] = x_ref[...] + _rms(h, g_ref[...])


def _proj_res(a, w, x, g, name):
    m, kdim = a.shape
    n = w.shape[1]
    tm = min(512, m)
    return pl.pallas_call(
        _proj_res_kernel,
        grid=(m // tm,),
        in_specs=[pl.BlockSpec((tm, kdim), lambda i: (i, 0)), pl.BlockSpec((kdim, n), lambda i: (0, 0)),
                  pl.BlockSpec((tm, n), lambda i: (i, 0)), pl.BlockSpec((1, n), lambda i: (0, 0))],
        out_specs=pl.BlockSpec((tm, n), lambda i: (i, 0)),
        out_shape=jax.ShapeDtypeStruct((m, n), F32),
        compiler_params=_params("parallel"),
        name=name,
    )(a, w, x, g)


def _ffn_kernel(x_ref, g_pre_ref, wg_ref, wu_ref, wd_ref, g_post_ref, o_ref, xn_sc, acc_sc):
    f = pl.program_id(1)

    @pl.when(f == 0)
    def _():
        xn_sc[...] = _rms(x_ref[...], g_pre_ref[...]).astype(BF16)
        acc_sc[...] = jnp.zeros(acc_sc.shape, F32)

    xn = xn_sc[...]
    half = wg_ref.shape[1] // 2
    acts = []
    for c in range(2):
        cols = slice(c * half, (c + 1) * half)
        gate = jnp.dot(xn, wg_ref[:, cols], preferred_element_type=F32)
        up = jnp.dot(xn, wu_ref[:, cols], preferred_element_type=F32)
        acts.append((gate * _sigmoid(gate) * up).astype(BF16))
    acc_sc[...] += jnp.dot(jnp.concatenate(acts, axis=1), wd_ref[...], preferred_element_type=F32)

    @pl.when(f == pl.num_programs(1) - 1)
    def _():
        o_ref[...] = x_ref[...] + _rms(acc_sc[...], g_post_ref[...])


def _ffn(x, g_pre, w_gu, w_down, g_post):
    m, d = x.shape
    d_ff = w_down.shape[0]
    tm = min(512, m)
    tf = 512
    nf = d_ff // tf
    return pl.pallas_call(
        _ffn_kernel,
        grid=(m // tm, nf),
        in_specs=[pl.BlockSpec((tm, d), lambda i, f: (i, 0)), pl.BlockSpec((1, d), lambda i, f: (0, 0)),
                  pl.BlockSpec((d, tf), lambda i, f: (0, f)), pl.BlockSpec((d, tf), lambda i, f: (0, f + nf)),
                  pl.BlockSpec((tf, d), lambda i, f: (f, 0)), pl.BlockSpec((1, d), lambda i, f: (0, 0))],
        out_specs=pl.BlockSpec((tm, d), lambda i, f: (i, 0)),
        out_shape=jax.ShapeDtypeStruct((m, d), F32),
        scratch_shapes=[pltpu.VMEM((tm, d), BF16), pltpu.VMEM((tm, d), F32)],
        compiler_params=_params("parallel", "arbitrary"),
        name="ffn",
    )(x, g_pre, w_gu, w_gu, w_down, g_post)


def _rwkv_mix_kernel(x_ref, xp_ref, xn_ref, g_ref, mu_ref, wr_ref, wk_ref, wv_ref, w1_ref, a1_ref, g1_ref,
                     r_ref, k_ref, v_ref, tw_ref, ta_ref, sg_ref, *, tiles_per_seq):
    i = pl.program_id(0)
    g = g_ref[...]
    h = _rms(x_ref[...], g)
    tm = h.shape[0]
    first = (i % tiles_per_seq) == 0
    last = (i % tiles_per_seq) == tiles_per_seq - 1
    hp = jnp.where(first, 0.0, _rms(xp_ref[7:8, :], g))
    hn = jnp.where(last, 0.0, _rms(xn_ref[0:1, :], g))
    row = lax.broadcasted_iota(jnp.int32, h.shape, 0)
    h_prev = jnp.where(row == 0, hp, pltpu.roll(h, 1, 0))
    h_next = jnp.where(row == tm - 1, hn, pltpu.roll(h, tm - 1, 0))
    xx = 0.5 * (h_prev + h_next) - h

    def lerp(idx):
        return (h + xx * mu_ref[idx:idx + 1, :]).astype(BF16)

    r_ref[...] = jnp.dot(lerp(0), wr_ref[...], preferred_element_type=F32)
    k_ref[...] = jnp.dot(lerp(2), wk_ref[...], preferred_element_type=F32)
    v_ref[...] = jnp.dot(lerp(3), wv_ref[...], preferred_element_type=F32)
    tw_ref[...] = jnp.tanh(jnp.dot(lerp(1), w1_ref[...], preferred_element_type=F32)).astype(BF16)
    ta_ref[...] = jnp.dot(lerp(4), a1_ref[...], preferred_element_type=F32).astype(BF16)
    sg_ref[...] = _sigmoid(jnp.dot(lerp(5), g1_ref[...], preferred_element_type=F32)).astype(BF16)


def _rwkv_mix(x, g, mu, w_r, w_k, w_v, w1_cat, a1_cat, g1, seq):
    m, d = x.shape
    lr = w1_cat.shape[1]
    tm = min(256, seq)
    tps = seq // tm
    nb8 = m // 8
    r8 = tm // 8
    row = lambda i: (i, 0)
    fix = lambda i: (0, 0)
    weight = pl.BlockSpec((d, d), fix, pipeline_mode=pl.Buffered(1))
    weight_lr = pl.BlockSpec((d, lr), fix, pipeline_mode=pl.Buffered(1))
    return pl.pallas_call(
        functools.partial(_rwkv_mix_kernel, tiles_per_seq=tps),
        grid=(m // tm,),
        in_specs=[pl.BlockSpec((tm, d), row),
                  pl.BlockSpec((8, d), lambda i: (jnp.maximum(i * r8 - 1, 0), 0)),
                  pl.BlockSpec((8, d), lambda i: (jnp.minimum((i + 1) * r8, nb8 - 1), 0)),
                  pl.BlockSpec((1, d), fix), pl.BlockSpec((8, d), fix), weight, weight, weight,
                  weight_lr, weight_lr, weight_lr],
        out_specs=[pl.BlockSpec((tm, d), row)] * 3 + [pl.BlockSpec((tm, lr), row)] * 3,
        out_shape=[jax.ShapeDtypeStruct((m, d), F32)] * 3 + [jax.ShapeDtypeStruct((m, lr), BF16)] * 3,
        compiler_params=_params("parallel"),
        name="rwkv_mix",
    )(x, x, x, g, mu, w_r, w_k, w_v, w1_cat, a1_cat, g1)


def _wkv_pre(r_ref, k_ref, v_ref, a_ref, lw_ref, k_k, k_a, ones_bd, blk, head_mask, gl, *, reverse):
    t_len = WKV_CHUNK
    ng = gl // RWKV_HEAD
    gt = ng * t_len
    n_chunks = r_ref.shape[0] // t_len

    wrow = lax.broadcasted_iota(jnp.int32, (t_len, gt), 0)
    wcol = lax.broadcasted_iota(jnp.int32, (t_len, gt), 1) % t_len
    strict = (wcol > wrow) if reverse else (wcol < wrow)
    incl = (wcol >= wrow) if reverse else (wcol <= wrow)
    eye_w = jnp.where(wcol == wrow, 1.0, 0.0)
    trow = lax.broadcasted_iota(jnp.int32, (t_len, t_len), 0)
    tcol = lax.broadcasted_iota(jnp.int32, (t_len, t_len), 1)
    tri = jnp.where((tcol >= trow) if reverse else (tcol <= trow), 1.0, 0.0).astype(BF16)
    bd_mask = jnp.where(lax.broadcasted_iota(jnp.int32, (gt, gt), 0) // t_len
                        == lax.broadcasted_iota(jnp.int32, (gt, gt), 1) // t_len, 1.0, 0.0).astype(BF16)

    def bdiag(w):
        return jnp.concatenate([w.astype(BF16)] * ng, axis=0) * bd_mask

    def mmb(a, b_bf16):
        return jnp.dot(a.astype(BF16), b_bf16, preferred_element_type=F32)

    items = []
    for ci in range(n_chunks):
        c = (n_chunks - 1 - ci) if reverse else ci
        items.append({"rows": slice(c * t_len, (c + 1) * t_len)})

    for it in items:
        lw = lw_ref[it["rows"], :]
        lw_hi = lw.astype(BF16)
        lw_lo = (lw - lw_hi.astype(F32)).astype(BF16)
        it["lw"] = lw
        cum2 = jnp.dot(tri, jnp.concatenate([lw_hi, lw_lo], axis=1), preferred_element_type=F32)
        it["cum"] = cum2[:, :gl] + cum2[:, gl:]
    yield

    kkr_all = k_ref[...] * k_k
    nrm_all = jnp.maximum(jnp.sqrt(jnp.dot((kkr_all * kkr_all).astype(BF16), ones_bd,
                                           preferred_element_type=F32)), 1e-12)
    kk_all = kkr_all / nrm_all

    for it in items:
        rows, cum, lw = it["rows"], it["cum"], it["lw"]
        a = a_ref[rows, :]
        kk = kk_all[rows]
        kd = k_ref[rows, :] * (1.0 + (a - 1.0) * k_a)
        b = kk * a
        tot = cum[0:1] if reverse else cum[t_len - 1:t_len]
        inv_p = jnp.exp(-cum)
        rt = r_ref[rows, :] * jnp.exp(cum)
        kkt = kk * jnp.exp(cum - lw)
        to_end = jnp.exp(tot - cum)
        it["decay"] = jnp.exp(tot)
        it["b_end"] = (b * to_end).astype(BF16)
        it["k_end"] = (kd * to_end).astype(BF16)
        it["rt"] = rt.astype(BF16)
        it["kkt_blk"] = blk(kkt)
        lhs = jnp.concatenate([kkt, rt], axis=0).astype(BF16)
        rhs = jnp.concatenate([blk(b * inv_p), blk(kd * inv_p)], axis=0)
        aa = lax.dot_general(lhs, rhs, NT, preferred_element_type=F32)
        it["a_ab"] = jnp.where(strict, aa[:t_len, :gt], 0.0)
        it["a_ak"] = jnp.where(strict, aa[:t_len, gt:], 0.0).astype(BF16)
        it["a_r"] = jnp.concatenate([jnp.where(incl, aa[t_len:, :gt], 0.0),
                                     jnp.where(incl, aa[t_len:, gt:], 0.0)], axis=1).astype(BF16)
    yield

    for it in items:
        it["x"] = eye_w - it["a_ab"]
        it["p"] = mmb(it["a_ab"], bdiag(it["a_ab"]))
    yield
    rounds = int(math.log2(t_len)) - 1
    for rnd in range(rounds):
        for it in items:
            p_bd = bdiag(it["p"])
            if rnd < rounds - 1:
                xp = mmb(jnp.concatenate([it["x"], it["p"]], axis=0), p_bd)
                it["x"] = it["x"] + xp[:t_len]
                it["p"] = xp[t_len:]
            else:
                it["x"] = it["x"] + mmb(it["x"], p_bd)
        yield

    for it in items:
        v = v_ref[it["rows"], :]
        it["v"] = v
        it["v_blk"] = blk(v)
        it["akv"] = jnp.dot(it["a_ak"], it["v_blk"], preferred_element_type=F32)
        it["x_b"] = it["x"].astype(BF16)
        it["w_k"] = jnp.dot(it["x_b"], it["kkt_blk"], preferred_element_type=F32)
    yield
    for it in items:
        it["u"] = jnp.dot(it["x_b"], blk(it["akv"]), preferred_element_type=F32)
    yield
    for it in items:
        m_full = lax.dot_general(it["w_k"].astype(BF16), it["b_end"], TN, preferred_element_type=F32)
        it["m"] = jnp.where(head_mask, m_full, 0.0).astype(BF16)
    yield
    for it in items:
        c_full = lax.dot_general(jnp.concatenate([-it["u"], it["v"]], axis=0).astype(BF16),
                                 jnp.concatenate([it["b_end"], it["k_end"]], axis=0),
                                 TN, preferred_element_type=F32)
        it["c"] = jnp.where(head_mask, c_full, 0.0)
        it["wr"] = jnp.concatenate([it["w_k"].astype(BF16), it["rt"]], axis=0)
    return items


def _staggered(first, second):
    results = [None, None]
    live = [first, second]
    next(first)
    while any(g is not None for g in live):
        for idx in (1, 0):
            if live[idx] is not None:
                try:
                    next(live[idx])
                except StopIteration as stop:
                    results[idx] = stop.value
                    live[idx] = None
    return results


def _wkv2_kernel(rf_ref, kf_ref, vf_ref, twf_ref, taf_ref, rb_ref, kb_ref, vb_ref, twb_ref, tab_ref,
                 w2_ref, a2_ref, par_ref, yf_ref, yb_ref, bonus_ref, stf_ref, stb_ref):
    @pl.when(pl.program_id(2) == 0)
    def _():
        stf_ref[...] = jnp.zeros(stf_ref.shape, F32)
        stb_ref[...] = jnp.zeros(stb_ref.shape, F32)

    gl = stf_ref.shape[0]
    gt = (gl // RWKV_HEAD) * WKV_CHUNK
    lr = w2_ref.shape[0] // 2
    ones_bd = _head_ones(gl)
    k_k, k_a, r_k = par_ref[0:1, :], par_ref[1:2, :], par_ref[2:3, :]

    def second_stage(t_ref, w_ref, di, bias_row):
        lo = di * lr
        return (jnp.dot(t_ref[:, lo:lo + lr], w_ref[lo:lo + lr, :], preferred_element_type=F32)
                + par_ref[bias_row + di:bias_row + di + 1, :])

    lw_f = -math.exp(-0.5) * _sigmoid(second_stage(twf_ref, w2_ref, 0, 3))
    lw_b = -math.exp(-0.5) * _sigmoid(second_stage(twb_ref, w2_ref, 1, 3))
    a0_f = _sigmoid(second_stage(taf_ref, a2_ref, 0, 5))
    a1_f = _sigmoid(second_stage(taf_ref, a2_ref, 1, 5))
    a1_b = _sigmoid(second_stage(tab_ref, a2_ref, 1, 5))

    kd_sum = kf_ref[...] * (2.0 + (a0_f + a1_f - 2.0) * k_a)
    bonus_ref[...] = jnp.dot((rf_ref[...] * kd_sum * r_k).astype(BF16), ones_bd,
                             preferred_element_type=F32) * vf_ref[...]

    blk_mask = jnp.where(lax.broadcasted_iota(jnp.int32, (gt, gl), 0) // WKV_CHUNK
                         == lax.broadcasted_iota(jnp.int32, (gt, gl), 1) // RWKV_HEAD, 1.0, 0.0).astype(BF16)
    head_mask = (lax.broadcasted_iota(jnp.int32, (gl, gl), 0) // RWKV_HEAD
                 == lax.broadcasted_iota(jnp.int32, (gl, gl), 1) // RWKV_HEAD)

    def blk(x):
        return jnp.concatenate([x.astype(BF16)] * (gl // RWKV_HEAD), axis=0) * blk_mask

    t_len = WKV_CHUNK
    items_f, items_b = _staggered(
        _wkv_pre(rf_ref, kf_ref, vf_ref, a0_f, lw_f, k_k, k_a, ones_bd, blk, head_mask, gl, reverse=False),
        _wkv_pre(rb_ref, kb_ref, vb_ref, a1_b, lw_b, k_k, k_a, ones_bd, blk, head_mask, gl, reverse=True))
    chains = [(items_f, yf_ref, stf_ref), (items_b, yb_ref, stb_ref)]
    states = [st_ref[...] for _, _, st_ref in chains]

    def emit_y(y_ref, it, sa, rs):
        y_ref[it["rows"], :] = rs + jnp.dot(it["a_r"], jnp.concatenate([blk(sa), it["v_blk"]], axis=0),
                                            preferred_element_type=F32)

    pending = []
    for ci in range(len(chains[0][0])):
        st_bf = [st.astype(BF16) for st in states]
        its = [items[ci] for items, _, _ in chains]
        st_m = [jnp.dot(sb, it["m"], preferred_element_type=F32) for sb, it in zip(st_bf, its)]
        ws = [lax.dot_general(it["wr"], sb, NT, preferred_element_type=F32) for sb, it in zip(st_bf, its)]
        for args in pending:
            emit_y(*args)
        pending = [(y_ref, it, -(w[:t_len] + it["u"]), w[t_len:])
                   for (_, y_ref, _), it, w in zip(chains, its, ws)]
        states = [st * it["decay"] - sm + it["c"] for st, it, sm in zip(states, its, st_m)]
    for args in pending:
        emit_y(*args)
    for (_, _, st_ref), st in zip(chains, states):
        st_ref[...] = st


def _wkv2(r, k, v, tw, ta, w2_cat, a2_cat, par, batch, seq):
    m, d = r.shape
    gl = WKV_GROUP * RWKV_HEAD
    lr2 = tw.shape[1]
    tb = min(WKV_BLOCK, seq)
    nb = seq // tb
    fwd = pl.BlockSpec((tb, gl), lambda bi, g, j: (bi * nb + j, g))
    bwd = pl.BlockSpec((tb, gl), lambda bi, g, j: (bi * nb + nb - 1 - j, g))
    fwd_lr = pl.BlockSpec((tb, lr2), lambda bi, g, j: (bi * nb + j, 0))
    bwd_lr = pl.BlockSpec((tb, lr2), lambda bi, g, j: (bi * nb + nb - 1 - j, 0))
    col = lambda rows: pl.BlockSpec((rows, gl), lambda bi, g, j: (0, g))
    out = jax.ShapeDtypeStruct((m, d), F32)
    return pl.pallas_call(
        _wkv2_kernel,
        grid=(batch, d // gl, nb),
        in_specs=[fwd] * 3 + [fwd_lr] * 2 + [bwd] * 3 + [bwd_lr] * 2 + [col(lr2), col(lr2), col(par.shape[0])],
        out_specs=[fwd, bwd, fwd],
        out_shape=[out, out, out],
        scratch_shapes=[pltpu.VMEM((gl, gl), F32), pltpu.VMEM((gl, gl), F32)],
        compiler_params=_params("parallel", "parallel", "arbitrary"),
        name="wkv",
    )(r, k, v, tw, ta, r, k, v, tw, ta, w2_cat, a2_cat, par)


def _rwkv_out_kernel(y0_ref, y1_ref, bonus_ref, sg_ref, g2_ref, lg_ref, lb_ref, wo_ref, x_ref, g_ref, o_ref):
    width = 2 * LANES
    ones_bd = _head_ones(width)
    d = y0_ref.shape[1]
    inv_n = 1.0 / RWKV_HEAD

    def head_sum(z):
        return jnp.dot(z.astype(BF16), ones_bd, preferred_element_type=F32)

    gate = jnp.dot(sg_ref[...], g2_ref[...], preferred_element_type=F32)
    gated = []
    for c in range(d // width):
        sl = slice(c * width, (c + 1) * width)
        y = y0_ref[:, sl] + y1_ref[:, sl]
        mean = head_sum(y) * inv_n
        yc = y - mean
        var = head_sum(yc * yc) * inv_n
        yn = yc * lax.rsqrt(var + LNX_EPS) * lg_ref[:, sl] + lb_ref[:, sl]
        gated.append(((yn + bonus_ref[:, sl]) * gate[:, sl]).astype(BF16))
    h = jnp.dot(jnp.concatenate(gated, axis=1), wo_ref[...], preferred_element_type=F32)
    o_ref[...] = x_ref[...] + _rms(h, g_ref[...])


def _rwkv_out(y0, y1, bonus, sg, g2, lnx_g, lnx_b, w_o, x, g_post):
    m, d = y0.shape
    lr = sg.shape[1]
    tm = min(256, m)
    row = pl.BlockSpec((tm, d), lambda i: (i, 0))
    par = pl.BlockSpec((1, d), lambda i: (0, 0))
    fix = lambda i: (0, 0)
    return pl.pallas_call(
        _rwkv_out_kernel,
        grid=(m // tm,),
        in_specs=[row] * 3 + [pl.BlockSpec((tm, lr), lambda i: (i, 0)),
                              pl.BlockSpec((lr, d), fix, pipeline_mode=pl.Buffered(1)), par, par,
                              pl.BlockSpec((d, d), fix, pipeline_mode=pl.Buffered(1)), row, par],
        out_specs=row,
        out_shape=jax.ShapeDtypeStruct((m, d), F32),
        compiler_params=_params("parallel"),
        name="rwkv_out",
    )(y0, y1, bonus, sg, g2, lnx_g, lnx_b, w_o, x, g_post)


def _rope_tables(seq):
    half = ROPE_DIM // 2
    inv = 1.0 / (ROPE_THETA ** (jnp.arange(half, dtype=F32) * (2.0 / ROPE_DIM)))
    ang = jnp.arange(seq, dtype=F32)[:, None] * inv[None, :]
    cos, sin = jnp.cos(ang), jnp.sin(ang)
    zero = jnp.zeros((seq, LANES - ROPE_DIM), F32)
    return (jnp.concatenate([cos, cos, zero], axis=1), jnp.concatenate([-sin, sin, zero], axis=1))


def _row(v):
    return v.reshape(1, -1).astype(F32)


def _pad_cols(w, n):
    return jnp.pad(w, ((0, 0), (0, n - w.shape[1])))


def _pad_rows(w, n):
    return jnp.pad(w, ((0, n - w.shape[0]), (0, 0)))


def _mla_layer(x, batch, seq, g_pre, g_post, w_in, g_q, g_kv, w_uq, w_ukv, w_o):
    cosw, sinw = _rope_tables(seq)
    w_in_pad = _pad_cols(w_in, Q_LORA + KV_LORA + LANES).astype(BF16)
    w_uq_pad = jnp.pad(w_uq.reshape(Q_LORA, MLA_HEADS, NOPE_DIM + ROPE_DIM),
                       ((0, 0), (0, 0), (0, QK_PAD - NOPE_DIM - ROPE_DIM))).reshape(Q_LORA, MLA_HEADS * QK_PAD)
    cq, ckv, kpe = _mla_in(x, _row(g_pre), w_in_pad, _row(g_q), _row(g_kv), cosw, sinw, seq)
    qt = _mla_q(cq, w_uq_pad.astype(BF16), cosw, sinw, seq)
    k, vt = _mla_kv(ckv, kpe, w_ukv.astype(BF16))
    o = _attention(qt, k, vt, batch, seq)
    return _proj_res(o, w_o.astype(BF16), x, _row(g_post), "mla_out")


def _rwkv_layer(x, batch, seq, g_pre, g_post, mu, w_r, w_k, w_v, w_o, w0, w1, w2, a0, a1, a2, g1, g2,
                k_k, k_a, r_k, lnx_g, lnx_b):
    mu8 = jnp.pad(mu, ((0, 8 - mu.shape[0]), (0, 0)))
    w1_cat = jnp.concatenate([_pad_cols(w1[di], LANES) for di in range(2)], axis=1).astype(BF16)
    a1_cat = jnp.concatenate([_pad_cols(a1[di], LANES) for di in range(2)], axis=1).astype(BF16)
    w2_cat = jnp.concatenate([_pad_rows(w2[di], LANES) for di in range(2)], axis=0).astype(BF16)
    a2_cat = jnp.concatenate([_pad_rows(a2[di], LANES) for di in range(2)], axis=0).astype(BF16)
    par = jnp.stack([k_k, k_a, r_k, w0[0], w0[1], a0[0], a0[1], jnp.zeros_like(k_k)]).astype(F32)
    r, k, v, tw, ta, sg = _rwkv_mix(x, _row(g_pre), mu8, w_r.astype(BF16), w_k.astype(BF16), w_v.astype(BF16),
                                    w1_cat, a1_cat, g1.astype(BF16), seq)
    y0, y1, bonus = _wkv2(r, k, v, tw, ta, w2_cat, a2_cat, par, batch, seq)
    return _rwkv_out(y0, y1, bonus, sg, g2.astype(BF16), _row(lnx_g), _row(lnx_b), w_o.astype(BF16), x,
                     _row(g_post))


def _trunk(x3, norm_g, mla_w_in, mla_g_q, mla_g_kv, mla_w_uq, mla_w_ukv, mla_w_o,
           rwkv_mu, rwkv_w_r, rwkv_w_k, rwkv_w_v, rwkv_w_o, rwkv_w0, rwkv_w1, rwkv_w2,
           rwkv_a0, rwkv_a1, rwkv_a2, rwkv_g1, rwkv_g2, rwkv_k_k, rwkv_k_a, rwkv_r_k,
           rwkv_lnx_g, rwkv_lnx_b, ffn_w_gu, ffn_w_down):
    batch, seq, d = x3.shape
    x = x3.reshape(batch * seq, d)
    depth = norm_g.shape[0]
    for i in range(depth):
        j = i // 2
        if i % 2 == 0:
            x = _mla_layer(x, batch, seq, norm_g[i, 0], norm_g[i, 1], mla_w_in[j], mla_g_q[j], mla_g_kv[j],
                           mla_w_uq[j], mla_w_ukv[j], mla_w_o[j])
        else:
            x = _rwkv_layer(x, batch, seq, norm_g[i, 0], norm_g[i, 1], rwkv_mu[j], rwkv_w_r[j], rwkv_w_k[j],
                            rwkv_w_v[j], rwkv_w_o[j], rwkv_w0[j], rwkv_w1[j], rwkv_w2[j], rwkv_a0[j],
                            rwkv_a1[j], rwkv_a2[j], rwkv_g1[j], rwkv_g2[j], rwkv_k_k[j], rwkv_k_a[j],
                            rwkv_r_k[j].reshape(-1), rwkv_lnx_g[j], rwkv_lnx_b[j])
        x = _ffn(x, _row(norm_g[i, 2]), ffn_w_gu[i].astype(BF16), ffn_w_down[i].astype(BF16), _row(norm_g[i, 3]))
    return x.reshape(batch, seq, d)


def kernel(x_prompt, x_sample, norm_g, mla_w_in, mla_g_q, mla_g_kv, mla_w_uq, mla_w_ukv, mla_w_o, rwkv_mu, rwkv_w_r, rwkv_w_k, rwkv_w_v, rwkv_w_o, rwkv_w0, rwkv_w1, rwkv_w2, rwkv_a0, rwkv_a1, rwkv_a2, rwkv_g1, rwkv_g2, rwkv_k_k, rwkv_k_a, rwkv_r_k, rwkv_lnx_g, rwkv_lnx_b, ffn_w_gu, ffn_w_down):
    params = (norm_g, mla_w_in, mla_g_q, mla_g_kv, mla_w_uq, mla_w_ukv, mla_w_o,
              rwkv_mu, rwkv_w_r, rwkv_w_k, rwkv_w_v, rwkv_w_o, rwkv_w0, rwkv_w1, rwkv_w2,
              rwkv_a0, rwkv_a1, rwkv_a2, rwkv_g1, rwkv_g2, rwkv_k_k, rwkv_k_a, rwkv_r_k,
              rwkv_lnx_g, rwkv_lnx_b, ffn_w_gu, ffn_w_down)
    return (_trunk(x_prompt, *params), _trunk(x_sample, *params))
```

```python
import functools
import math

import jax
import jax.numpy as jnp
from jax import lax
from jax.experimental import pallas as pl
from jax.experimental.pallas import tpu as pltpu

F32 = jnp.float32
BF16 = jnp.bfloat16

NORM_EPS = 1e-6
LNX_EPS = 64e-5
ROPE_THETA = 10000.0

MLA_HEADS = 16
Q_LORA = 512
KV_LORA = 512
NOPE_DIM = 128
ROPE_DIM = 64
V_DIM = 128
QK_PAD = 256
ONES_ROWS = 16
RWKV_HEAD = 64

LANES = 128
VMEM_LIMIT = 56 * 1024 * 1024

WKV_CHUNK = 64
WKV_GROUP = 4
WKV_BLOCK = 512

NT = (((1,), (1,)), ((), ()))
TN = (((0,), (0,)), ((), ()))


def _params(*sem):
    return pltpu.CompilerParams(dimension_semantics=sem, vmem_limit_bytes=VMEM_LIMIT)


def _rms(x, g):
    return x * lax.rsqrt(jnp.mean(x * x, axis=-1, keepdims=True) + NORM_EPS) * g


def _sigmoid(z):
    return 1.0 / (1.0 + jnp.exp(-z))


def _rope_upper(up, cosw, sinw):
    lane = lax.broadcasted_iota(jnp.int32, up.shape, 1)
    swapped = jnp.where(lane < ROPE_DIM // 2, pltpu.roll(up, LANES - ROPE_DIM // 2, 1),
                        pltpu.roll(up, ROPE_DIM // 2, 1))
    return up * cosw + swapped * sinw


def _head_ones(n):
    r = lax.broadcasted_iota(jnp.int32, (n, n), 0) // RWKV_HEAD
    c = lax.broadcasted_iota(jnp.int32, (n, n), 1) // RWKV_HEAD
    return jnp.where(r == c, 1.0, 0.0).astype(BF16)


def _mla_in_kernel(x_ref, g_ref, w_ref, gq_ref, gkv_ref, cos_ref, sin_ref, cq_ref, ckv_ref, kpe_ref):
    xn = _rms(x_ref[...], g_ref[...]).astype(BF16)
    h = jnp.dot(xn, w_ref[...], preferred_element_type=F32)
    cq_ref[...] = _rms(h[:, :Q_LORA], gq_ref[...]).astype(BF16)
    ckv_ref[...] = _rms(h[:, Q_LORA:Q_LORA + KV_LORA], gkv_ref[...]).astype(BF16)
    kpe_ref[...] = _rope_upper(h[:, Q_LORA + KV_LORA:], cos_ref[...], sin_ref[...]).astype(BF16)


def _mla_in(x, g, w_in_pad, g_q, g_kv, cosw, sinw, seq):
    m, d = x.shape
    tm = min(512, seq)
    nseq = seq // tm
    n = w_in_pad.shape[1]
    row = lambda i: (i, 0)
    fix = lambda i: (0, 0)
    tab = lambda i: (i % nseq, 0)
    return pl.pallas_call(
        _mla_in_kernel,
        grid=(m // tm,),
        in_specs=[pl.BlockSpec((tm, d), row), pl.BlockSpec((1, d), fix), pl.BlockSpec((d, n), fix),
                  pl.BlockSpec((1, Q_LORA), fix), pl.BlockSpec((1, KV_LORA), fix),
                  pl.BlockSpec((tm, LANES), tab), pl.BlockSpec((tm, LANES), tab)],
        out_specs=[pl.BlockSpec((tm, Q_LORA), row), pl.BlockSpec((tm, KV_LORA), row),
                   pl.BlockSpec((tm, LANES), row)],
        out_shape=[jax.ShapeDtypeStruct((m, Q_LORA), BF16), jax.ShapeDtypeStruct((m, KV_LORA), BF16),
                   jax.ShapeDtypeStruct((m, LANES), BF16)],
        compiler_params=_params("parallel"),
        name="mla_in",
    )(x, g, w_in_pad, g_q, g_kv, cosw, sinw)


def _mla_q_kernel(cq_ref, w_ref, cos_ref, sin_ref, q_ref, *, scale):
    res = jnp.dot(cq_ref[...], w_ref[...], preferred_element_type=F32)
    cosw = cos_ref[...]
    sinw = sin_ref[...]
    for h in range(MLA_HEADS):
        lo = res[:, h * QK_PAD:h * QK_PAD + NOPE_DIM]
        up = _rope_upper(res[:, h * QK_PAD + NOPE_DIM:(h + 1) * QK_PAD], cosw, sinw)
        q_ref[h * QK_PAD:h * QK_PAD + NOPE_DIM, :] = (lo * scale).T.astype(BF16)
        q_ref[h * QK_PAD + NOPE_DIM:(h + 1) * QK_PAD, :] = (up * scale).T.astype(BF16)


def _mla_q(cq, w_uq_pad, cosw, sinw, seq):
    m, c = cq.shape
    n = w_uq_pad.shape[1]
    tm = min(256, seq)
    nseq = seq // tm
    scale = (NOPE_DIM + ROPE_DIM) ** -0.5 * math.log2(math.e)
    return pl.pallas_call(
        functools.partial(_mla_q_kernel, scale=scale),
        grid=(m // tm,),
        in_specs=[pl.BlockSpec((tm, c), lambda i: (i, 0)), pl.BlockSpec((c, n), lambda i: (0, 0)),
                  pl.BlockSpec((tm, LANES), lambda i: (i % nseq, 0)),
                  pl.BlockSpec((tm, LANES), lambda i: (i % nseq, 0))],
        out_specs=pl.BlockSpec((n, tm), lambda i: (0, i)),
        out_shape=jax.ShapeDtypeStruct((n, m), BF16),
        compiler_params=_params("parallel"),
        name="mla_q",
    )(cq, w_uq_pad, cosw, sinw)


def _mla_kv_kernel(ckv_ref, kpe_ref, w_ref, k_ref, vt_ref):
    res = jnp.dot(ckv_ref[...], w_ref[...], preferred_element_type=F32)
    kpe = kpe_ref[...]
    for h in range(MLA_HEADS):
        k_ref[h, :, :NOPE_DIM] = res[:, 2 * h * NOPE_DIM:(2 * h + 1) * NOPE_DIM].astype(BF16)
        k_ref[h, :, NOPE_DIM:] = kpe
        vt_ref[h * V_DIM:(h + 1) * V_DIM, :] = res[:, (2 * h + 1) * V_DIM:(2 * h + 2) * V_DIM].T.astype(BF16)


def _mla_kv(ckv, kpe, w_ukv):
    m, c = ckv.shape
    n = w_ukv.shape[1]
    tm = min(256, m)
    return pl.pallas_call(
        _mla_kv_kernel,
        grid=(m // tm,),
        in_specs=[pl.BlockSpec((tm, c), lambda i: (i, 0)), pl.BlockSpec((tm, LANES), lambda i: (i, 0)),
                  pl.BlockSpec((c, n), lambda i: (0, 0))],
        out_specs=[pl.BlockSpec((MLA_HEADS, tm, QK_PAD), lambda i: (0, i, 0)),
                   pl.BlockSpec((n // 2, tm), lambda i: (0, i))],
        out_shape=[jax.ShapeDtypeStruct((MLA_HEADS, m, QK_PAD), BF16), jax.ShapeDtypeStruct((n // 2, m), BF16)],
        compiler_params=_params("parallel"),
        name="mla_kv",
    )(ckv, kpe, w_ukv)


def _attn_kernel(qt_ref, qtn_ref, k0_ref, k_ref, vt_ref, o_ref, sa_sc, sb_sc, xa_sc, xb_sc, m_sc, acc_sc,
                 *, n_split, n_kv):
    i = pl.program_id(2)
    j = pl.program_id(3)
    width = qt_ref.shape[1] // n_split
    last = n_kv - 1
    carry = n_kv % 2 == 0
    buf_a = (sa_sc, xa_sc)
    buf_b = (sb_sc, xb_sc)

    def score(k, q_ref, dst, cols):
        s = jnp.dot(k, q_ref[:, cols], preferred_element_type=F32)
        dst[0][:, cols] = s
        dst[1][:, cols] = jnp.max(s, axis=0, keepdims=True)

    @pl.when(j == 0)
    def _():
        m_sc[...] = jnp.full(m_sc.shape, -jnp.inf, F32)
        acc_sc[...] = jnp.zeros(acc_sc.shape, F32)

    @pl.when((j == 0) & (i == 0) if carry else (j == 0))
    def _():
        k0 = k0_ref[0]
        for c in range(n_split):
            score(k0, qt_ref, buf_a, slice(c * width, (c + 1) * width))

    def step(cur, nxt, k_next_ref, q_next_ref):
        k = k_next_ref[0]
        vt = vt_ref[...]
        vt1 = jnp.concatenate([vt, jnp.ones((ONES_ROWS, vt.shape[1]), BF16)], axis=0)
        for c in range(n_split):
            cols = slice(c * width, (c + 1) * width)
            if nxt is not None:
                score(k, q_next_ref, nxt, cols)
            m_prev = m_sc[:, cols]
            m_new = jnp.maximum(m_prev, cur[1][:, cols])
            alpha = jnp.exp2(m_prev - m_new)
            pt = jnp.exp2((cur[0][:, cols] - m_new).astype(BF16))
            acc_sc[:, cols] = alpha * acc_sc[:, cols] + jnp.dot(vt1, pt, preferred_element_type=F32)
            m_sc[:, cols] = m_new

    @pl.when((j % 2 == 0) & (j < last))
    def _():
        step(buf_a, buf_b, k_ref, qt_ref)

    @pl.when((j % 2 == 1) & (j < last))
    def _():
        step(buf_b, buf_a, k_ref, qt_ref)

    @pl.when(j == last)
    def _():
        if carry:
            step(buf_b, buf_a, k0_ref, qtn_ref)
        else:
            step(buf_a, None, k_ref, qt_ref)
        o_ref[...] = (acc_sc[:V_DIM, :] / acc_sc[V_DIM:V_DIM + 1, :]).T.astype(o_ref.dtype)


def _attention(qt, k, vt, batch, seq):
    m = k.shape[1]
    tq = min(2048, seq)
    tk = min(2048, seq)
    nq = seq // tq
    nk = seq // tk
    return pl.pallas_call(
        functools.partial(_attn_kernel, n_split=max(tq // 256, 1), n_kv=nk),
        grid=(batch, MLA_HEADS, nq, nk),
        in_specs=[pl.BlockSpec((QK_PAD, tq), lambda b, h, i, j: (h, b * nq + i)),
                  pl.BlockSpec((QK_PAD, tq), lambda b, h, i, j: (h, b * nq + jnp.minimum(i + 1, nq - 1))),
                  pl.BlockSpec((1, tk, QK_PAD), lambda b, h, i, j: (h, b * nk, 0)),
                  pl.BlockSpec((1, tk, QK_PAD), lambda b, h, i, j: (h, b * nk + jnp.minimum(j + 1, nk - 1), 0)),
                  pl.BlockSpec((V_DIM, tk), lambda b, h, i, j: (h, b * nk + j))],
        out_specs=pl.BlockSpec((tq, V_DIM), lambda b, h, i, j: (b * nq + i, h)),
        out_shape=jax.ShapeDtypeStruct((m, MLA_HEADS * V_DIM), BF16),
        scratch_shapes=[pltpu.VMEM((tk, tq), F32), pltpu.VMEM((tk, tq), F32),
                        pltpu.VMEM((1, tq), F32), pltpu.VMEM((1, tq), F32),
                        pltpu.VMEM((1, tq), F32), pltpu.VMEM((V_DIM + ONES_ROWS, tq), F32)],
        compiler_params=_params("parallel", "parallel", "arbitrary", "arbitrary"),
        name="mla_attention",
    )(qt, qt, k, k, vt)


def _proj_res_kernel(a_ref, w_ref, x_ref, g_ref, o_ref):
    h = jnp.dot(a_ref[...], w_ref[...], preferred_element_type=F32)
    o_ref[...] = x_ref[...] + _rms(h, g_ref[...])


def _proj_res(a, w, x, g, name):
    m, kdim = a.shape
    n = w.shape[1]
    tm = min(512, m)
    return pl.pallas_call(
        _proj_res_kernel,
        grid=(m // tm,),
        in_specs=[pl.BlockSpec((tm, kdim), lambda i: (i, 0)), pl.BlockSpec((kdim, n), lambda i: (0, 0)),
                  pl.BlockSpec((tm, n), lambda i: (i, 0)), pl.BlockSpec((1, n), lambda i: (0, 0))],
        out_specs=pl.BlockSpec((tm, n), lambda i: (i, 0)),
        out_shape=jax.ShapeDtypeStruct((m, n), F32),
        compiler_params=_params("parallel"),
        name=name,
    )(a, w, x, g)


def _ffn_kernel(x_ref, g_pre_ref, wg_ref, wu_ref, wd_ref, g_post_ref, o_ref, xn_sc, acc_sc):
    f = pl.program_id(1)

    @pl.when(f == 0)
    def _():
        xn_sc[...] = _rms(x_ref[...], g_pre_ref[...]).astype(BF16)
        acc_sc[...] = jnp.zeros(acc_sc.shape, F32)

    xn = xn_sc[...]
    half = wg_ref.shape[1] // 2
    acts = []
    for c in range(2):
        cols = slice(c * half, (c + 1) * half)
        gate = jnp.dot(xn, wg_ref[:, cols], preferred_element_type=F32)
        up = jnp.dot(xn, wu_ref[:, cols], preferred_element_type=F32)
        acts.append((gate * _sigmoid(gate) * up).astype(BF16))
    acc_sc[...] += jnp.dot(jnp.concatenate(acts, axis=1), wd_ref[...], preferred_element_type=F32)

    @pl.when(f == pl.num_programs(1) - 1)
    def _():
        o_ref[...] = x_ref[...] + _rms(acc_sc[...], g_post_ref[...])


def _ffn(x, g_pre, w_gu, w_down, g_post):
    m, d = x.shape
    d_ff = w_down.shape[0]
    tm = min(512, m)
    tf = 512
    nf = d_ff // tf
    return pl.pallas_call(
        _ffn_kernel,
        grid=(m // tm, nf),
        in_specs=[pl.BlockSpec((tm, d), lambda i, f: (i, 0)), pl.BlockSpec((1, d), lambda i, f: (0, 0)),
                  pl.BlockSpec((d, tf), lambda i, f: (0, f)), pl.BlockSpec((d, tf), lambda i, f: (0, f + nf)),
                  pl.BlockSpec((tf, d), lambda i, f: (f, 0)), pl.BlockSpec((1, d), lambda i, f: (0, 0))],
        out_specs=pl.BlockSpec((tm, d), lambda i, f: (i, 0)),
        out_shape=jax.ShapeDtypeStruct((m, d), F32),
        scratch_shapes=[pltpu.VMEM((tm, d), BF16), pltpu.VMEM((tm, d), F32)],
        compiler_params=_params("parallel", "arbitrary"),
        name="ffn",
    )(x, g_pre, w_gu, w_gu, w_down, g_post)


def _rwkv_mix_kernel(x_ref, xp_ref, xn_ref, g_ref, mu_ref, wr_ref, wk_ref, wv_ref, w1_ref, a1_ref, g1_ref,
                     r_ref, k_ref, v_ref, tw_ref, ta_ref, sg_ref, *, tiles_per_seq):
    i = pl.program_id(0)
    g = g_ref[...]
    h = _rms(x_ref[...], g)
    tm = h.shape[0]
    first = (i % tiles_per_seq) == 0
    last = (i % tiles_per_seq) == tiles_per_seq - 1
    hp = jnp.where(first, 0.0, _rms(xp_ref[7:8, :], g))
    hn = jnp.where(last, 0.0, _rms(xn_ref[0:1, :], g))
    row = lax.broadcasted_iota(jnp.int32, h.shape, 0)
    h_prev = jnp.where(row == 0, hp, pltpu.roll(h, 1, 0))
    h_next = jnp.where(row == tm - 1, hn, pltpu.roll(h, tm - 1, 0))
    xx = 0.5 * (h_prev + h_next) - h

    def lerp(idx):
        return (h + xx * mu_ref[idx:idx + 1, :]).astype(BF16)

    r_ref[...] = jnp.dot(lerp(0), wr_ref[...], preferred_element_type=F32)
    k_ref[...] = jnp.dot(lerp(2), wk_ref[...], preferred_element_type=F32)
    v_ref[...] = jnp.dot(lerp(3), wv_ref[...], preferred_element_type=F32)
    tw_ref[...] = jnp.tanh(jnp.dot(lerp(1), w1_ref[...], preferred_element_type=F32)).astype(BF16)
    ta_ref[...] = jnp.dot(lerp(4), a1_ref[...], preferred_element_type=F32).astype(BF16)
    sg_ref[...] = _sigmoid(jnp.dot(lerp(5), g1_ref[...], preferred_element_type=F32)).astype(BF16)


def _rwkv_mix(x, g, mu, w_r, w_k, w_v, w1_cat, a1_cat, g1, seq):
    m, d = x.shape
    lr = w1_cat.shape[1]
    tm = min(256, seq)
    tps = seq // tm
    nb8 = m // 8
    r8 = tm // 8
    row = lambda i: (i, 0)
    fix = lambda i: (0, 0)
    weight = pl.BlockSpec((d, d), fix, pipeline_mode=pl.Buffered(1))
    weight_lr = pl.BlockSpec((d, lr), fix, pipeline_mode=pl.Buffered(1))
    return pl.pallas_call(
        functools.partial(_rwkv_mix_kernel, tiles_per_seq=tps),
        grid=(m // tm,),
        in_specs=[pl.BlockSpec((tm, d), row),
                  pl.BlockSpec((8, d), lambda i: (jnp.maximum(i * r8 - 1, 0), 0)),
                  pl.BlockSpec((8, d), lambda i: (jnp.minimum((i + 1) * r8, nb8 - 1), 0)),
                  pl.BlockSpec((1, d), fix), pl.BlockSpec((8, d), fix), weight, weight, weight,
                  weight_lr, weight_lr, weight_lr],
        out_specs=[pl.BlockSpec((tm, d), row)] * 3 + [pl.BlockSpec((tm, lr), row)] * 3,
        out_shape=[jax.ShapeDtypeStruct((m, d), F32)] * 3 + [jax.ShapeDtypeStruct((m, lr), BF16)] * 3,
        compiler_params=_params("parallel"),
        name="rwkv_mix",
    )(x, x, x, g, mu, w_r, w_k, w_v, w1_cat, a1_cat, g1)


def _wkv_pre(r_ref, k_ref, v_ref, a_ref, lw_ref, k_k, k_a, ones_bd, blk, head_mask, gl, *, reverse):
    t_len = WKV_CHUNK
    ng = gl // RWKV_HEAD
    gt = ng * t_len
    n_chunks = r_ref.shape[0] // t_len

    wrow = lax.broadcasted_iota(jnp.int32, (t_len, gt), 0)
    wcol = lax.broadcasted_iota(jnp.int32, (t_len, gt), 1) % t_len
    strict = (wcol > wrow) if reverse else (wcol < wrow)
    incl = (wcol >= wrow) if reverse else (wcol <= wrow)
    eye_w = jnp.where(wcol == wrow, 1.0, 0.0)
    trow = lax.broadcasted_iota(jnp.int32, (t_len, t_len), 0)
    tcol = lax.broadcasted_iota(jnp.int32, (t_len, t_len), 1)
    tri = jnp.where((tcol >= trow) if reverse else (tcol <= trow), 1.0, 0.0).astype(BF16)
    bd_mask = jnp.where(lax.broadcasted_iota(jnp.int32, (gt, gt), 0) // t_len
                        == lax.broadcasted_iota(jnp.int32, (gt, gt), 1) // t_len, 1.0, 0.0).astype(BF16)

    def bdiag(w):
        return jnp.concatenate([w.astype(BF16)] * ng, axis=0) * bd_mask

    def mmb(a, b_bf16):
        return jnp.dot(a.astype(BF16), b_bf16, preferred_element_type=F32)

    items = []
    for ci in range(n_chunks):
        c = (n_chunks - 1 - ci) if reverse else ci
        items.append({"rows": slice(c * t_len, (c + 1) * t_len)})

    for it in items:
        lw = lw_ref[it["rows"], :]
        lw_hi = lw.astype(BF16)
        lw_lo = (lw - lw_hi.astype(F32)).astype(BF16)
        it["lw"] = lw
        cum2 = jnp.dot(tri, jnp.concatenate([lw_hi, lw_lo], axis=1), preferred_element_type=F32)
        it["cum"] = cum2[:, :gl] + cum2[:, gl:]
    yield

    kkr_all = k_ref[...] * k_k
    nrm_all = jnp.maximum(jnp.sqrt(jnp.dot((kkr_all * kkr_all).astype(BF16), ones_bd,
                                           preferred_element_type=F32)), 1e-12)
    kk_all = kkr_all / nrm_all

    for it in items:
        rows, cum, lw = it["rows"], it["cum"], it["lw"]
        a = a_ref[rows, :]
        kk = kk_all[rows]
        kd = k_ref[rows, :] * (1.0 + (a - 1.0) * k_a)
        b = kk * a
        tot = cum[0:1] if reverse else cum[t_len - 1:t_len]
        inv_p = jnp.exp(-cum)
        rt = r_ref[rows, :] * jnp.exp(cum)
        kkt = kk * jnp.exp(cum - lw)
        to_end = jnp.exp(tot - cum)
        it["decay"] = jnp.exp(tot)
        it["b_end"] = (b * to_end).astype(BF16)
        it["k_end"] = (kd * to_end).astype(BF16)
        it["rt"] = rt.astype(BF16)
        it["kkt_blk"] = blk(kkt)
        lhs = jnp.concatenate([kkt, rt], axis=0).astype(BF16)
        rhs = jnp.concatenate([blk(b * inv_p), blk(kd * inv_p)], axis=0)
        aa = lax.dot_general(lhs, rhs, NT, preferred_element_type=F32)
        it["a_ab"] = jnp.where(strict, aa[:t_len, :gt], 0.0)
        it["a_ak"] = jnp.where(strict, aa[:t_len, gt:], 0.0).astype(BF16)
        it["a_r"] = jnp.concatenate([jnp.where(incl, aa[t_len:, :gt], 0.0),
                                     jnp.where(incl, aa[t_len:, gt:], 0.0)], axis=1).astype(BF16)
    yield

    for it in items:
        it["x"] = eye_w - it["a_ab"]
        it["p"] = mmb(it["a_ab"], bdiag(it["a_ab"]))
    yield
    rounds = int(math.log2(t_len)) - 1
    for rnd in range(rounds):
        for it in items:
            p_bd = bdiag(it["p"])
            if rnd < rounds - 1:
                xp = mmb(jnp.concatenate([it["x"], it["p"]], axis=0), p_bd)
                it["x"] = it["x"] + xp[:t_len]
                it["p"] = xp[t_len:]
            else:
                it["x"] = it["x"] + mmb(it["x"], p_bd)
        yield

    for it in items:
        v = v_ref[it["rows"], :]
        it["v"] = v
        it["v_blk"] = blk(v)
        it["akv"] = jnp.dot(it["a_ak"], it["v_blk"], preferred_element_type=F32)
        it["x_b"] = it["x"].astype(BF16)
        it["w_k"] = jnp.dot(it["x_b"], it["kkt_blk"], preferred_element_type=F32)
    yield
    for it in items:
        it["u"] = jnp.dot(it["x_b"], blk(it["akv"]), preferred_element_type=F32)
    yield
    for it in items:
        m_full = lax.dot_general(it["w_k"].astype(BF16), it["b_end"], TN, preferred_element_type=F32)
        it["m"] = jnp.where(head_mask, m_full, 0.0).astype(BF16)
    yield
    for it in items:
        c_full = lax.dot_general(jnp.concatenate([-it["u"], it["v"]], axis=0).astype(BF16),
                                 jnp.concatenate([it["b_end"], it["k_end"]], axis=0),
                                 TN, preferred_element_type=F32)
        it["c"] = jnp.where(head_mask, c_full, 0.0)
        it["wr"] = jnp.concatenate([it["w_k"].astype(BF16), it["rt"]], axis=0)
    return items


def _staggered(first, second):
    results = [None, None]
    live = [first, second]
    next(first)
    while any(g is not None for g in live):
        for idx in (1, 0):
            if live[idx] is not None:
                try:
                    next(live[idx])
                except StopIteration as stop:
                    results[idx] = stop.value
                    live[idx] = None
    return results


def _wkv2_kernel(rf_ref, kf_ref, vf_ref, twf_ref, taf_ref, rb_ref, kb_ref, vb_ref, twb_ref, tab_ref,
                 w2_ref, a2_ref, par_ref, yf_ref, yb_ref, bonus_ref, stf_ref, stb_ref):
    @pl.when(pl.program_id(2) == 0)
    def _():
        stf_ref[...] = jnp.zeros(stf_ref.shape, F32)
        stb_ref[...] = jnp.zeros(stb_ref.shape, F32)

    gl = stf_ref.shape[0]
    gt = (gl // RWKV_HEAD) * WKV_CHUNK
    lr = w2_ref.shape[0] // 2
    ones_bd = _head_ones(gl)
    k_k, k_a, r_k = par_ref[0:1, :], par_ref[1:2, :], par_ref[2:3, :]

    def second_stage(t_ref, w_ref, di, bias_row):
        lo = di * lr
        return (jnp.dot(t_ref[:, lo:lo + lr], w_ref[lo:lo + lr, :], preferred_element_type=F32)
                + par_ref[bias_row + di:bias_row + di + 1, :])

    lw_f = -math.exp(-0.5) * _sigmoid(second_stage(twf_ref, w2_ref, 0, 3))
    lw_b = -math.exp(-0.5) * _sigmoid(second_stage(twb_ref, w2_ref, 1, 3))
    a0_f = _sigmoid(second_stage(taf_ref, a2_ref, 0, 5))
    a1_f = _sigmoid(second_stage(taf_ref, a2_ref, 1, 5))
    a1_b = _sigmoid(second_stage(tab_ref, a2_ref, 1, 5))

    kd_sum = kf_ref[...] * (2.0 + (a0_f + a1_f - 2.0) * k_a)
    bonus_ref[...] = jnp.dot((rf_ref[...] * kd_sum * r_k).astype(BF16), ones_bd,
                             preferred_element_type=F32) * vf_ref[...]

    blk_mask = jnp.where(lax.broadcasted_iota(jnp.int32, (gt, gl), 0) // WKV_CHUNK
                         == lax.broadcasted_iota(jnp.int32, (gt, gl), 1) // RWKV_HEAD, 1.0, 0.0).astype(BF16)
    head_mask = (lax.broadcasted_iota(jnp.int32, (gl, gl), 0) // RWKV_HEAD
                 == lax.broadcasted_iota(jnp.int32, (gl, gl), 1) // RWKV_HEAD)

    def blk(x):
        return jnp.concatenate([x.astype(BF16)] * (gl // RWKV_HEAD), axis=0) * blk_mask

    t_len = WKV_CHUNK
    items_f, items_b = _staggered(
        _wkv_pre(rf_ref, kf_ref, vf_ref, a0_f, lw_f, k_k, k_a, ones_bd, blk, head_mask, gl, reverse=False),
        _wkv_pre(rb_ref, kb_ref, vb_ref, a1_b, lw_b, k_k, k_a, ones_bd, blk, head_mask, gl, reverse=True))
    chains = [(items_f, yf_ref, stf_ref), (items_b, yb_ref, stb_ref)]
    states = [st_ref[...] for _, _, st_ref in chains]

    def emit_y(y_ref, it, sa, rs):
        y_ref[it["rows"], :] = rs + jnp.dot(it["a_r"], jnp.concatenate([blk(sa), it["v_blk"]], axis=0),
                                            preferred_element_type=F32)

    pending = []
    for ci in range(len(chains[0][0])):
        st_bf = [st.astype(BF16) for st in states]
        its = [items[ci] for items, _, _ in chains]
        st_m = [jnp.dot(sb, it["m"], preferred_element_type=F32) for sb, it in zip(st_bf, its)]
        ws = [lax.dot_general(it["wr"], sb, NT, preferred_element_type=F32) for sb, it in zip(st_bf, its)]
        for args in pending:
            emit_y(*args)
        pending = [(y_ref, it, -(w[:t_len] + it["u"]), w[t_len:])
                   for (_, y_ref, _), it, w in zip(chains, its, ws)]
        states = [st * it["decay"] - sm + it["c"] for st, it, sm in zip(states, its, st_m)]
    for args in pending:
        emit_y(*args)
    for (_, _, st_ref), st in zip(chains, states):
        st_ref[...] = st


def _wkv2(r, k, v, tw, ta, w2_cat, a2_cat, par, batch, seq):
    m, d = r.shape
    gl = WKV_GROUP * RWKV_HEAD
    lr2 = tw.shape[1]
    tb = min(WKV_BLOCK, seq)
    nb = seq // tb
    fwd = pl.BlockSpec((tb, gl), lambda bi, g, j: (bi * nb + j, g))
    bwd = pl.BlockSpec((tb, gl), lambda bi, g, j: (bi * nb + nb - 1 - j, g))
    fwd_lr = pl.BlockSpec((tb, lr2), lambda bi, g, j: (bi * nb + j, 0))
    bwd_lr = pl.BlockSpec((tb, lr2), lambda bi, g, j: (bi * nb + nb - 1 - j, 0))
    col = lambda rows: pl.BlockSpec((rows, gl), lambda bi, g, j: (0, g))
    out = jax.ShapeDtypeStruct((m, d), F32)
    return pl.pallas_call(
        _wkv2_kernel,
        grid=(batch, d // gl, nb),
        in_specs=[fwd] * 3 + [fwd_lr] * 2 + [bwd] * 3 + [bwd_lr] * 2 + [col(lr2), col(lr2), col(par.shape[0])],
        out_specs=[fwd, bwd, fwd],
        out_shape=[out, out, out],
        scratch_shapes=[pltpu.VMEM((gl, gl), F32), pltpu.VMEM((gl, gl), F32)],
        compiler_params=_params("parallel", "parallel", "arbitrary"),
        name="wkv",
    )(r, k, v, tw, ta, r, k, v, tw, ta, w2_cat, a2_cat, par)


def _rwkv_out_kernel(y0_ref, y1_ref, bonus_ref, sg_ref, g2_ref, lg_ref, lb_ref, wo_ref, x_ref, g_ref, o_ref):
    width = 2 * LANES
    ones_bd = _head_ones(width)
    d = y0_ref.shape[1]
    inv_n = 1.0 / RWKV_HEAD

    def head_sum(z):
        return jnp.dot(z.astype(BF16), ones_bd, preferred_element_type=F32)

    gate = jnp.dot(sg_ref[...], g2_ref[...], preferred_element_type=F32)
    gated = []
    for c in range(d // width):
        sl = slice(c * width, (c + 1) * width)
        y = y0_ref[:, sl] + y1_ref[:, sl]
        mean = head_sum(y) * inv_n
        yc = y - mean
        var = head_sum(yc * yc) * inv_n
        yn = yc * lax.rsqrt(var + LNX_EPS) * lg_ref[:, sl] + lb_ref[:, sl]
        gated.append(((yn + bonus_ref[:, sl]) * gate[:, sl]).astype(BF16))
    h = jnp.dot(jnp.concatenate(gated, axis=1), wo_ref[...], preferred_element_type=F32)
    o_ref[...] = x_ref[...] + _rms(h, g_ref[...])


def _rwkv_out(y0, y1, bonus, sg, g2, lnx_g, lnx_b, w_o, x, g_post):
    m, d = y0.shape
    lr = sg.shape[1]
    tm = min(256, m)
    row = pl.BlockSpec((tm, d), lambda i: (i, 0))
    par = pl.BlockSpec((1, d), lambda i: (0, 0))
    fix = lambda i: (0, 0)
    return pl.pallas_call(
        _rwkv_out_kernel,
        grid=(m // tm,),
        in_specs=[row] * 3 + [pl.BlockSpec((tm, lr), lambda i: (i, 0)),
                              pl.BlockSpec((lr, d), fix, pipeline_mode=pl.Buffered(1)), par, par,
                              pl.BlockSpec((d, d), fix, pipeline_mode=pl.Buffered(1)), row, par],
        out_specs=row,
        out_shape=jax.ShapeDtypeStruct((m, d), F32),
        compiler_params=_params("parallel"),
        name="rwkv_out",
    )(y0, y1, bonus, sg, g2, lnx_g, lnx_b, w_o, x, g_post)


def _rope_tables(seq):
    half = ROPE_DIM // 2
    inv = 1.0 / (ROPE_THETA ** (jnp.arange(half, dtype=F32) * (2.0 / ROPE_DIM)))
    ang = jnp.arange(seq, dtype=F32)[:, None] * inv[None, :]
    cos, sin = jnp.cos(ang), jnp.sin(ang)
    zero = jnp.zeros((seq, LANES - ROPE_DIM), F32)
    return (jnp.concatenate([cos, cos, zero], axis=1), jnp.concatenate([-sin, sin, zero], axis=1))


def _row(v):
    return v.reshape(1, -1).astype(F32)


def _pad_cols(w, n):
    return jnp.pad(w, ((0, 0), (0, n - w.shape[1])))


def _pad_rows(w, n):
    return jnp.pad(w, ((0, n - w.shape[0]), (0, 0)))


def _mla_layer(x, batch, seq, g_pre, g_post, w_in, g_q, g_kv, w_uq, w_ukv, w_o):
    cosw, sinw = _rope_tables(seq)
    w_in_pad = _pad_cols(w_in, Q_LORA + KV_LORA + LANES).astype(BF16)
    w_uq_pad = jnp.pad(w_uq.reshape(Q_LORA, MLA_HEADS, NOPE_DIM + ROPE_DIM),
                       ((0, 0), (0, 0), (0, QK_PAD - NOPE_DIM - ROPE_DIM))).reshape(Q_LORA, MLA_HEADS * QK_PAD)
    cq, ckv, kpe = _mla_in(x, _row(g_pre), w_in_pad, _row(g_q), _row(g_kv), cosw, sinw, seq)
    qt = _mla_q(cq, w_uq_pad.astype(BF16), cosw, sinw, seq)
    k, vt = _mla_kv(ckv, kpe, w_ukv.astype(BF16))
    o = _attention(qt, k, vt, batch, seq)
    return _proj_res(o, w_o.astype(BF16), x, _row(g_post), "mla_out")


def _rwkv_layer(x, batch, seq, g_pre, g_post, mu, w_r, w_k, w_v, w_o, w0, w1, w2, a0, a1, a2, g1, g2,
                k_k, k_a, r_k, lnx_g, lnx_b):
    mu8 = jnp.pad(mu, ((0, 8 - mu.shape[0]), (0, 0)))
    w1_cat = jnp.concatenate([_pad_cols(w1[di], LANES) for di in range(2)], axis=1).astype(BF16)
    a1_cat = jnp.concatenate([_pad_cols(a1[di], LANES) for di in range(2)], axis=1).astype(BF16)
    w2_cat = jnp.concatenate([_pad_rows(w2[di], LANES) for di in range(2)], axis=0).astype(BF16)
    a2_cat = jnp.concatenate([_pad_rows(a2[di], LANES) for di in range(2)], axis=0).astype(BF16)
    par = jnp.stack([k_k, k_a, r_k, w0[0], w0[1], a0[0], a0[1], jnp.zeros_like(k_k)]).astype(F32)
    r, k, v, tw, ta, sg = _rwkv_mix(x, _row(g_pre), mu8, w_r.astype(BF16), w_k.astype(BF16), w_v.astype(BF16),
                                    w1_cat, a1_cat, g1.astype(BF16), seq)
    y0, y1, bonus = _wkv2(r, k, v, tw, ta, w2_cat, a2_cat, par, batch, seq)
    return _rwkv_out(y0, y1, bonus, sg, g2.astype(BF16), _row(lnx_g), _row(lnx_b), w_o.astype(BF16), x,
                     _row(g_post))


def _trunk(x3, norm_g, mla_w_in, mla_g_q, mla_g_kv, mla_w_uq, mla_w_ukv, mla_w_o,
           rwkv_mu, rwkv_w_r, rwkv_w_k, rwkv_w_v, rwkv_w_o, rwkv_w0, rwkv_w1, rwkv_w2,
           rwkv_a0, rwkv_a1, rwkv_a2, rwkv_g1, rwkv_g2, rwkv_k_k, rwkv_k_a, rwkv_r_k,
           rwkv_lnx_g, rwkv_lnx_b, ffn_w_gu, ffn_w_down):
    batch, seq, d = x3.shape
    x = x3.reshape(batch * seq, d)
    depth = norm_g.shape[0]
    for i in range(depth):
        j = i // 2
        if i % 2 == 0:
            x = _mla_layer(x, batch, seq, norm_g[i, 0], norm_g[i, 1], mla_w_in[j], mla_g_q[j], mla_g_kv[j],
                           mla_w_uq[j], mla_w_ukv[j], mla_w_o[j])
        else:
            x = _rwkv_layer(x, batch, seq, norm_g[i, 0], norm_g[i, 1], rwkv_mu[j], rwkv_w_r[j], rwkv_w_k[j],
                            rwkv_w_v[j], rwkv_w_o[j], rwkv_w0[j], rwkv_w1[j], rwkv_w2[j], rwkv_a0[j],
                            rwkv_a1[j], rwkv_a2[j], rwkv_g1[j], rwkv_g2[j], rwkv_k_k[j], rwkv_k_a[j],
                            rwkv_r_k[j].reshape(-1), rwkv_lnx_g[j], rwkv_lnx_b[j])
        x = _ffn(x, _row(norm_g[i, 2]), ffn_w_gu[i].astype(BF16), ffn_w_down[i].astype(BF16), _row(norm_g[i, 3]))
    return x.reshape(batch, seq, d)


def kernel(x_prompt, x_sample, norm_g, mla_w_in, mla_g_q, mla_g_kv, mla_w_uq, mla_w_ukv, mla_w_o, rwkv_mu, rwkv_w_r, rwkv_w_k, rwkv_w_v, rwkv_w_o, rwkv_w0, rwkv_w1, rwkv_w2, rwkv_a0, rwkv_a1, rwkv_a2, rwkv_g1, rwkv_g2, rwkv_k_k, rwkv_k_a, rwkv_r_k, rwkv_lnx_g, rwkv_lnx_b, ffn_w_gu, ffn_w_down):
    params = (norm_g, mla_w_in, mla_g_q, mla_g_kv, mla_w_uq, mla_w_ukv, mla_w_o,
              rwkv_mu, rwkv_w_r, rwkv_w_k, rwkv_w_v, rwkv_w_o, rwkv_w0, rwkv_w1, rwkv_w2,
              rwkv_a0, rwkv_a1, rwkv_a2, rwkv_g1, rwkv_g2, rwkv_k_k, rwkv_k_a, rwkv_r_k,
              rwkv_lnx_g, rwkv_lnx_b, ffn_w_gu, ffn_w_down)
    return (_trunk(x_prompt, *params), _trunk(x_sample, *params))
```

```python
import functools
import math

import jax
import jax.numpy as jnp
from jax import lax
from jax.experimental import pallas as pl
from jax.experimental.pallas import tpu as pltpu

F32 = jnp.float32
BF16 = jnp.bfloat16

NORM_EPS = 1e-6
LNX_EPS = 64e-5
ROPE_THETA = 10000.0

MLA_HEADS = 16
Q_LORA = 512
KV_LORA = 512
NOPE_DIM = 128
ROPE_DIM = 64
V_DIM = 128
QK_PAD = 256
ONES_ROWS = 16
RWKV_HEAD = 64

LANES = 128
VMEM_LIMIT = 56 * 1024 * 1024

WKV_CHUNK = 64
WKV_GROUP = 4
WKV_BLOCK = 1024

NT = (((1,), (1,)), ((), ()))
TN = (((0,), (0,)), ((), ()))


def _params(*sem):
    return pltpu.CompilerParams(dimension_semantics=sem, vmem_limit_bytes=VMEM_LIMIT)


def _rms(x, g):
    return x * lax.rsqrt(jnp.mean(x * x, axis=-1, keepdims=True) + NORM_EPS) * g


def _sigmoid(z):
    return 1.0 / (1.0 + jnp.exp(-z))


def _rope_upper(up, cosw, sinw):
    lane = lax.broadcasted_iota(jnp.int32, up.shape, 1)
    swapped = jnp.where(lane < ROPE_DIM // 2, pltpu.roll(up, LANES - ROPE_DIM // 2, 1),
                        pltpu.roll(up, ROPE_DIM // 2, 1))
    return up * cosw + swapped * sinw


def _head_ones(n):
    r = lax.broadcasted_iota(jnp.int32, (n, n), 0) // RWKV_HEAD
    c = lax.broadcasted_iota(jnp.int32, (n, n), 1) // RWKV_HEAD
    return jnp.where(r == c, 1.0, 0.0).astype(BF16)


def _mla_in_kernel(x_ref, g_ref, w_ref, gq_ref, gkv_ref, cos_ref, sin_ref, cq_ref, ckv_ref, kpe_ref):
    xn = _rms(x_ref[...], g_ref[...]).astype(BF16)
    h = jnp.dot(xn, w_ref[...], preferred_element_type=F32)
    cq_ref[...] = _rms(h[:, :Q_LORA], gq_ref[...]).astype(BF16)
    ckv_ref[...] = _rms(h[:, Q_LORA:Q_LORA + KV_LORA], gkv_ref[...]).astype(BF16)
    kpe_ref[...] = _rope_upper(h[:, Q_LORA + KV_LORA:], cos_ref[...], sin_ref[...]).astype(BF16)


def _mla_in(x, g, w_in_pad, g_q, g_kv, cosw, sinw, seq):
    m, d = x.shape
    tm = min(512, seq)
    nseq = seq // tm
    n = w_in_pad.shape[1]
    row = lambda i: (i, 0)
    fix = lambda i: (0, 0)
    tab = lambda i: (i % nseq, 0)
    return pl.pallas_call(
        _mla_in_kernel,
        grid=(m // tm,),
        in_specs=[pl.BlockSpec((tm, d), row), pl.BlockSpec((1, d), fix), pl.BlockSpec((d, n), fix),
                  pl.BlockSpec((1, Q_LORA), fix), pl.BlockSpec((1, KV_LORA), fix),
                  pl.BlockSpec((tm, LANES), tab), pl.BlockSpec((tm, LANES), tab)],
        out_specs=[pl.BlockSpec((tm, Q_LORA), row), pl.BlockSpec((tm, KV_LORA), row),
                   pl.BlockSpec((tm, LANES), row)],
        out_shape=[jax.ShapeDtypeStruct((m, Q_LORA), BF16), jax.ShapeDtypeStruct((m, KV_LORA), BF16),
                   jax.ShapeDtypeStruct((m, LANES), BF16)],
        compiler_params=_params("parallel"),
        name="mla_in",
    )(x, g, w_in_pad, g_q, g_kv, cosw, sinw)


def _mla_q_kernel(cq_ref, w_ref, cos_ref, sin_ref, q_ref, *, scale):
    res = jnp.dot(cq_ref[...], w_ref[...], preferred_element_type=F32)
    cosw = cos_ref[...]
    sinw = sin_ref[...]
    for h in range(MLA_HEADS):
        lo = res[:, h * QK_PAD:h * QK_PAD + NOPE_DIM]
        up = _rope_upper(res[:, h * QK_PAD + NOPE_DIM:(h + 1) * QK_PAD], cosw, sinw)
        q_ref[h * QK_PAD:h * QK_PAD + NOPE_DIM, :] = (lo * scale).T.astype(BF16)
        q_ref[h * QK_PAD + NOPE_DIM:(h + 1) * QK_PAD, :] = (up * scale).T.astype(BF16)


def _mla_q(cq, w_uq_pad, cosw, sinw, seq):
    m, c = cq.shape
    n = w_uq_pad.shape[1]
    tm = min(256, seq)
    nseq = seq // tm
    scale = (NOPE_DIM + ROPE_DIM) ** -0.5 * math.log2(math.e)
    return pl.pallas_call(
        functools.partial(_mla_q_kernel, scale=scale),
        grid=(m // tm,),
        in_specs=[pl.BlockSpec((tm, c), lambda i: (i, 0)), pl.BlockSpec((c, n), lambda i: (0, 0)),
                  pl.BlockSpec((tm, LANES), lambda i: (i % nseq, 0)),
                  pl.BlockSpec((tm, LANES), lambda i: (i % nseq, 0))],
        out_specs=pl.BlockSpec((n, tm), lambda i: (0, i)),
        out_shape=jax.ShapeDtypeStruct((n, m), BF16),
        compiler_params=_params("parallel"),
        name="mla_q",
    )(cq, w_uq_pad, cosw, sinw)


def _mla_kv_kernel(ckv_ref, kpe_ref, w_ref, k_ref, vt_ref):
    res = jnp.dot(ckv_ref[...], w_ref[...], preferred_element_type=F32)
    kpe = kpe_ref[...]
    for h in range(MLA_HEADS):
        k_ref[h, :, :NOPE_DIM] = res[:, 2 * h * NOPE_DIM:(2 * h + 1) * NOPE_DIM].astype(BF16)
        k_ref[h, :, NOPE_DIM:] = kpe
        vt_ref[h * V_DIM:(h + 1) * V_DIM, :] = res[:, (2 * h + 1) * V_DIM:(2 * h + 2) * V_DIM].T.astype(BF16)


def _mla_kv(ckv, kpe, w_ukv):
    m, c = ckv.shape
    n = w_ukv.shape[1]
    tm = min(256, m)
    return pl.pallas_call(
        _mla_kv_kernel,
        grid=(m // tm,),
        in_specs=[pl.BlockSpec((tm, c), lambda i: (i, 0)), pl.BlockSpec((tm, LANES), lambda i: (i, 0)),
                  pl.BlockSpec((c, n), lambda i: (0, 0))],
        out_specs=[pl.BlockSpec((MLA_HEADS, tm, QK_PAD), lambda i: (0, i, 0)),
                   pl.BlockSpec((n // 2, tm), lambda i: (0, i))],
        out_shape=[jax.ShapeDtypeStruct((MLA_HEADS, m, QK_PAD), BF16), jax.ShapeDtypeStruct((n // 2, m), BF16)],
        compiler_params=_params("parallel"),
        name="mla_kv",
    )(ckv, kpe, w_ukv)


def _attn_kernel(qt_ref, qtn_ref, k0_ref, k_ref, vt_ref, o_ref, sa_sc, sb_sc, xa_sc, xb_sc, m_sc, acc_sc,
                 *, n_split, n_kv):
    i = pl.program_id(2)
    j = pl.program_id(3)
    width = qt_ref.shape[1] // n_split
    last = n_kv - 1
    carry = n_kv % 2 == 0
    buf_a = (sa_sc, xa_sc)
    buf_b = (sb_sc, xb_sc)

    def score(k, q_ref, dst, cols):
        s = jnp.dot(k, q_ref[:, cols], preferred_element_type=F32)
        dst[0][:, cols] = s
        dst[1][:, cols] = jnp.max(s, axis=0, keepdims=True)

    @pl.when(j == 0)
    def _():
        m_sc[...] = jnp.full(m_sc.shape, -jnp.inf, F32)
        acc_sc[...] = jnp.zeros(acc_sc.shape, F32)

    @pl.when((j == 0) & (i == 0) if carry else (j == 0))
    def _():
        k0 = k0_ref[0]
        for c in range(n_split):
            score(k0, qt_ref, buf_a, slice(c * width, (c + 1) * width))

    def step(cur, nxt, k_next_ref, q_next_ref):
        k = k_next_ref[0]
        vt = vt_ref[...]
        vt1 = jnp.concatenate([vt, jnp.ones((ONES_ROWS, vt.shape[1]), BF16)], axis=0)
        for c in range(n_split):
            cols = slice(c * width, (c + 1) * width)
            if nxt is not None:
                score(k, q_next_ref, nxt, cols)
            m_prev = m_sc[:, cols]
            m_new = jnp.maximum(m_prev, cur[1][:, cols])
            alpha = jnp.exp2(m_prev - m_new)
            pt = jnp.exp2((cur[0][:, cols] - m_new).astype(BF16))
            acc_sc[:, cols] = alpha * acc_sc[:, cols] + jnp.dot(vt1, pt, preferred_element_type=F32)
            m_sc[:, cols] = m_new

    @pl.when((j % 2 == 0) & (j < last))
    def _():
        step(buf_a, buf_b, k_ref, qt_ref)

    @pl.when((j % 2 == 1) & (j < last))
    def _():
        step(buf_b, buf_a, k_ref, qt_ref)

    @pl.when(j == last)
    def _():
        if carry:
            step(buf_b, buf_a, k0_ref, qtn_ref)
        else:
            step(buf_a, None, k_ref, qt_ref)
        o_ref[...] = (acc_sc[:V_DIM, :] / acc_sc[V_DIM:V_DIM + 1, :]).T.astype(o_ref.dtype)


def _attention(qt, k, vt, batch, seq):
    m = k.shape[1]
    tq = min(2048, seq)
    tk = min(2048, seq)
    nq = seq // tq
    nk = seq // tk
    return pl.pallas_call(
        functools.partial(_attn_kernel, n_split=max(tq // 256, 1), n_kv=nk),
        grid=(batch, MLA_HEADS, nq, nk),
        in_specs=[pl.BlockSpec((QK_PAD, tq), lambda b, h, i, j: (h, b * nq + i)),
                  pl.BlockSpec((QK_PAD, tq), lambda b, h, i, j: (h, b * nq + jnp.minimum(i + 1, nq - 1))),
                  pl.BlockSpec((1, tk, QK_PAD), lambda b, h, i, j: (h, b * nk, 0)),
                  pl.BlockSpec((1, tk, QK_PAD), lambda b, h, i, j: (h, b * nk + jnp.minimum(j + 1, nk - 1), 0)),
                  pl.BlockSpec((V_DIM, tk), lambda b, h, i, j: (h, b * nk + j))],
        out_specs=pl.BlockSpec((tq, V_DIM), lambda b, h, i, j: (b * nq + i, h)),
        out_shape=jax.ShapeDtypeStruct((m, MLA_HEADS * V_DIM), BF16),
        scratch_shapes=[pltpu.VMEM((tk, tq), F32), pltpu.VMEM((tk, tq), F32),
                        pltpu.VMEM((1, tq), F32), pltpu.VMEM((1, tq), F32),
                        pltpu.VMEM((1, tq), F32), pltpu.VMEM((V_DIM + ONES_ROWS, tq), F32)],
        compiler_params=_params("parallel", "parallel", "arbitrary", "arbitrary"),
        name="mla_attention",
    )(qt, qt, k, k, vt)


def _proj_res_kernel(a_ref, w_ref, x_ref, g_ref, o_ref):
    h = jnp.dot(a_ref[...], w_ref[...], preferred_element_type=F32)
    o_ref[...] = x_ref[...] + _rms(h, g_ref[...])


def _proj_res(a, w, x, g, name):
    m, kdim = a.shape
    n = w.shape[1]
    tm = min(512, m)
    return pl.pallas_call(
        _proj_res_kernel,
        grid=(m // tm,),
        in_specs=[pl.BlockSpec((tm, kdim), lambda i: (i, 0)), pl.BlockSpec((kdim, n), lambda i: (0, 0)),
                  pl.BlockSpec((tm, n), lambda i: (i, 0)), pl.BlockSpec((1, n), lambda i: (0, 0))],
        out_specs=pl.BlockSpec((tm, n), lambda i: (i, 0)),
        out_shape=jax.ShapeDtypeStruct((m, n), F32),
        compiler_params=_params("parallel"),
        name=name,
    )(a, w, x, g)


def _ffn_kernel(x_ref, g_pre_ref, wg_ref, wu_ref, wd_ref, g_post_ref, o_ref, xn_sc, acc_sc):
    f = pl.program_id(1)

    @pl.when(f == 0)
    def _():
        xn_sc[...] = _rms(x_ref[...], g_pre_ref[...]).astype(BF16)
        acc_sc[...] = jnp.zeros(acc_sc.shape, F32)

    xn = xn_sc[...]
    half = wg_ref.shape[1] // 2
    acts = []
    for c in range(2):
        cols = slice(c * half, (c + 1) * half)
        gate = jnp.dot(xn, wg_ref[:, cols], preferred_element_type=F32)
        up = jnp.dot(xn, wu_ref[:, cols], preferred_element_type=F32)
        acts.append((gate * _sigmoid(gate) * up).astype(BF16))
    acc_sc[...] += jnp.dot(jnp.concatenate(acts, axis=1), wd_ref[...], preferred_element_type=F32)

    @pl.when(f == pl.num_programs(1) - 1)
    def _():
        o_ref[...] = x_ref[...] + _rms(acc_sc[...], g_post_ref[...])


def _ffn(x, g_pre, w_gu, w_down, g_post):
    m, d = x.shape
    d_ff = w_down.shape[0]
    tm = min(512, m)
    tf = 512
    nf = d_ff // tf
    return pl.pallas_call(
        _ffn_kernel,
        grid=(m // tm, nf),
        in_specs=[pl.BlockSpec((tm, d), lambda i, f: (i, 0)), pl.BlockSpec((1, d), lambda i, f: (0, 0)),
                  pl.BlockSpec((d, tf), lambda i, f: (0, f)), pl.BlockSpec((d, tf), lambda i, f: (0, f + nf)),
                  pl.BlockSpec((tf, d), lambda i, f: (f, 0)), pl.BlockSpec((1, d), lambda i, f: (0, 0))],
        out_specs=pl.BlockSpec((tm, d), lambda i, f: (i, 0)),
        out_shape=jax.ShapeDtypeStruct((m, d), F32),
        scratch_shapes=[pltpu.VMEM((tm, d), BF16), pltpu.VMEM((tm, d), F32)],
        compiler_params=_params("parallel", "arbitrary"),
        name="ffn",
    )(x, g_pre, w_gu, w_gu, w_down, g_post)


def _rwkv_mix_kernel(x_ref, xp_ref, xn_ref, g_ref, mu_ref, wr_ref, wk_ref, wv_ref, w1_ref, a1_ref, g1_ref,
                     r_ref, k_ref, v_ref, tw_ref, ta_ref, sg_ref, *, tiles_per_seq):
    i = pl.program_id(0)
    g = g_ref[...]
    h = _rms(x_ref[...], g)
    tm = h.shape[0]
    first = (i % tiles_per_seq) == 0
    last = (i % tiles_per_seq) == tiles_per_seq - 1
    hp = jnp.where(first, 0.0, _rms(xp_ref[7:8, :], g))
    hn = jnp.where(last, 0.0, _rms(xn_ref[0:1, :], g))
    row = lax.broadcasted_iota(jnp.int32, h.shape, 0)
    h_prev = jnp.where(row == 0, hp, pltpu.roll(h, 1, 0))
    h_next = jnp.where(row == tm - 1, hn, pltpu.roll(h, tm - 1, 0))
    xx = 0.5 * (h_prev + h_next) - h

    def lerp(idx):
        return (h + xx * mu_ref[idx:idx + 1, :]).astype(BF16)

    r_ref[...] = jnp.dot(lerp(0), wr_ref[...], preferred_element_type=F32)
    k_ref[...] = jnp.dot(lerp(2), wk_ref[...], preferred_element_type=F32)
    v_ref[...] = jnp.dot(lerp(3), wv_ref[...], preferred_element_type=F32)
    tw_ref[...] = jnp.tanh(jnp.dot(lerp(1), w1_ref[...], preferred_element_type=F32)).astype(BF16)
    ta_ref[...] = jnp.dot(lerp(4), a1_ref[...], preferred_element_type=F32).astype(BF16)
    sg_ref[...] = _sigmoid(jnp.dot(lerp(5), g1_ref[...], preferred_element_type=F32)).astype(BF16)


def _rwkv_mix(x, g, mu, w_r, w_k, w_v, w1_cat, a1_cat, g1, seq):
    m, d = x.shape
    lr = w1_cat.shape[1]
    tm = min(256, seq)
    tps = seq // tm
    nb8 = m // 8
    r8 = tm // 8
    row = lambda i: (i, 0)
    fix = lambda i: (0, 0)
    weight = pl.BlockSpec((d, d), fix, pipeline_mode=pl.Buffered(1))
    weight_lr = pl.BlockSpec((d, lr), fix, pipeline_mode=pl.Buffered(1))
    return pl.pallas_call(
        functools.partial(_rwkv_mix_kernel, tiles_per_seq=tps),
        grid=(m // tm,),
        in_specs=[pl.BlockSpec((tm, d), row),
                  pl.BlockSpec((8, d), lambda i: (jnp.maximum(i * r8 - 1, 0), 0)),
                  pl.BlockSpec((8, d), lambda i: (jnp.minimum((i + 1) * r8, nb8 - 1), 0)),
                  pl.BlockSpec((1, d), fix), pl.BlockSpec((8, d), fix), weight, weight, weight,
                  weight_lr, weight_lr, weight_lr],
        out_specs=[pl.BlockSpec((tm, d), row)] * 3 + [pl.BlockSpec((tm, lr), row)] * 3,
        out_shape=[jax.ShapeDtypeStruct((m, d), F32)] * 3 + [jax.ShapeDtypeStruct((m, lr), BF16)] * 3,
        compiler_params=_params("parallel"),
        name="rwkv_mix",
    )(x, x, x, g, mu, w_r, w_k, w_v, w1_cat, a1_cat, g1)


def _wkv_pre(r_ref, k_ref, v_ref, a_ref, lw_ref, k_k, k_a, ones_bd, blk, head_mask, gl, *, reverse):
    t_len = WKV_CHUNK
    ng = gl // RWKV_HEAD
    gt = ng * t_len
    n_chunks = r_ref.shape[0] // t_len

    wrow = lax.broadcasted_iota(jnp.int32, (t_len, gt), 0)
    wcol = lax.broadcasted_iota(jnp.int32, (t_len, gt), 1) % t_len
    strict = (wcol > wrow) if reverse else (wcol < wrow)
    incl = (wcol >= wrow) if reverse else (wcol <= wrow)
    eye_w = jnp.where(wcol == wrow, 1.0, 0.0)
    trow = lax.broadcasted_iota(jnp.int32, (t_len, t_len), 0)
    tcol = lax.broadcasted_iota(jnp.int32, (t_len, t_len), 1)
    tri = jnp.where((tcol >= trow) if reverse else (tcol <= trow), 1.0, 0.0).astype(BF16)
    bd_mask = jnp.where(lax.broadcasted_iota(jnp.int32, (gt, gt), 0) // t_len
                        == lax.broadcasted_iota(jnp.int32, (gt, gt), 1) // t_len, 1.0, 0.0).astype(BF16)

    def bdiag(w):
        return jnp.concatenate([w.astype(BF16)] * ng, axis=0) * bd_mask

    def mmb(a, b_bf16):
        return jnp.dot(a.astype(BF16), b_bf16, preferred_element_type=F32)

    items = []
    for ci in range(n_chunks):
        c = (n_chunks - 1 - ci) if reverse else ci
        items.append({"rows": slice(c * t_len, (c + 1) * t_len)})

    for it in items:
        lw = lw_ref[it["rows"], :]
        lw_hi = lw.astype(BF16)
        lw_lo = (lw - lw_hi.astype(F32)).astype(BF16)
        it["lw"] = lw
        cum2 = jnp.dot(tri, jnp.concatenate([lw_hi, lw_lo], axis=1), preferred_element_type=F32)
        it["cum"] = cum2[:, :gl] + cum2[:, gl:]
    yield

    kkr_all = k_ref[...] * k_k
    nrm_all = jnp.maximum(jnp.sqrt(jnp.dot((kkr_all * kkr_all).astype(BF16), ones_bd,
                                           preferred_element_type=F32)), 1e-12)
    kk_all = kkr_all / nrm_all

    for it in items:
        rows, cum, lw = it["rows"], it["cum"], it["lw"]
        a = a_ref[rows, :]
        kk = kk_all[rows]
        kd = k_ref[rows, :] * (1.0 + (a - 1.0) * k_a)
        b = kk * a
        tot = cum[0:1] if reverse else cum[t_len - 1:t_len]
        inv_p = jnp.exp(-cum)
        rt = r_ref[rows, :] * jnp.exp(cum)
        kkt = kk * jnp.exp(cum - lw)
        to_end = jnp.exp(tot - cum)
        it["decay"] = jnp.exp(tot)
        it["b_end"] = (b * to_end).astype(BF16)
        it["k_end"] = (kd * to_end).astype(BF16)
        it["rt"] = rt.astype(BF16)
        it["kkt_blk"] = blk(kkt)
        lhs = jnp.concatenate([kkt, rt], axis=0).astype(BF16)
        rhs = jnp.concatenate([blk(b * inv_p), blk(kd * inv_p)], axis=0)
        aa = lax.dot_general(lhs, rhs, NT, preferred_element_type=F32)
        it["a_ab"] = jnp.where(strict, aa[:t_len, :gt], 0.0)
        it["a_ak"] = jnp.where(strict, aa[:t_len, gt:], 0.0).astype(BF16)
        it["a_r"] = jnp.concatenate([jnp.where(incl, aa[t_len:, :gt], 0.0),
                                     jnp.where(incl, aa[t_len:, gt:], 0.0)], axis=1).astype(BF16)
    yield

    for it in items:
        it["x"] = eye_w - it["a_ab"]
        it["p"] = mmb(it["a_ab"], bdiag(it["a_ab"]))
    yield
    rounds = int(math.log2(t_len)) - 1
    for rnd in range(rounds):
        for it in items:
            p_bd = bdiag(it["p"])
            if rnd < rounds - 1:
                xp = mmb(jnp.concatenate([it["x"], it["p"]], axis=0), p_bd)
                it["x"] = it["x"] + xp[:t_len]
                it["p"] = xp[t_len:]
            else:
                it["x"] = it["x"] + mmb(it["x"], p_bd)
        yield

    for it in items:
        v = v_ref[it["rows"], :]
        it["v"] = v
        it["v_blk"] = blk(v)
        it["akv"] = jnp.dot(it["a_ak"], it["v_blk"], preferred_element_type=F32)
        it["x_b"] = it["x"].astype(BF16)
        it["w_k"] = jnp.dot(it["x_b"], it["kkt_blk"], preferred_element_type=F32)
    yield
    for it in items:
        it["u"] = jnp.dot(it["x_b"], blk(it["akv"]), preferred_element_type=F32)
    yield
    for it in items:
        m_full = lax.dot_general(it["w_k"].astype(BF16), it["b_end"], TN, preferred_element_type=F32)
        it["m"] = jnp.where(head_mask, m_full, 0.0).astype(BF16)
    yield
    for it in items:
        c_full = lax.dot_general(jnp.concatenate([-it["u"], it["v"]], axis=0).astype(BF16),
                                 jnp.concatenate([it["b_end"], it["k_end"]], axis=0),
                                 TN, preferred_element_type=F32)
        it["c"] = jnp.where(head_mask, c_full, 0.0)
        it["wr"] = jnp.concatenate([it["w_k"].astype(BF16), it["rt"]], axis=0)
    return items


def _staggered(first, second):
    results = [None, None]
    live = [first, second]
    next(first)
    while any(g is not None for g in live):
        for idx in (1, 0):
            if live[idx] is not None:
                try:
                    next(live[idx])
                except StopIteration as stop:
                    results[idx] = stop.value
                    live[idx] = None
    return results


def _wkv2_kernel(rf_ref, kf_ref, vf_ref, twf_ref, taf_ref, rb_ref, kb_ref, vb_ref, twb_ref, tab_ref,
                 w2_ref, a2_ref, par_ref, yf_ref, yb_ref, bonus_ref, stf_ref, stb_ref):
    @pl.when(pl.program_id(2) == 0)
    def _():
        stf_ref[...] = jnp.zeros(stf_ref.shape, F32)
        stb_ref[...] = jnp.zeros(stb_ref.shape, F32)

    gl = stf_ref.shape[0]
    gt = (gl // RWKV_HEAD) * WKV_CHUNK
    lr = w2_ref.shape[0] // 2
    ones_bd = _head_ones(gl)
    k_k, k_a, r_k = par_ref[0:1, :], par_ref[1:2, :], par_ref[2:3, :]

    def second_stage(t_ref, w_ref, di, bias_row):
        lo = di * lr
        return (jnp.dot(t_ref[:, lo:lo + lr], w_ref[lo:lo + lr, :], preferred_element_type=F32)
                + par_ref[bias_row + di:bias_row + di + 1, :])

    lw_f = -math.exp(-0.5) * _sigmoid(second_stage(twf_ref, w2_ref, 0, 3))
    lw_b = -math.exp(-0.5) * _sigmoid(second_stage(twb_ref, w2_ref, 1, 3))
    a0_f = _sigmoid(second_stage(taf_ref, a2_ref, 0, 5))
    a1_f = _sigmoid(second_stage(taf_ref, a2_ref, 1, 5))
    a1_b = _sigmoid(second_stage(tab_ref, a2_ref, 1, 5))

    kd_sum = kf_ref[...] * (2.0 + (a0_f + a1_f - 2.0) * k_a)
    bonus_ref[...] = jnp.dot((rf_ref[...] * kd_sum * r_k).astype(BF16), ones_bd,
                             preferred_element_type=F32) * vf_ref[...]

    blk_mask = jnp.where(lax.broadcasted_iota(jnp.int32, (gt, gl), 0) // WKV_CHUNK
                         == lax.broadcasted_iota(jnp.int32, (gt, gl), 1) // RWKV_HEAD, 1.0, 0.0).astype(BF16)
    head_mask = (lax.broadcasted_iota(jnp.int32, (gl, gl), 0) // RWKV_HEAD
                 == lax.broadcasted_iota(jnp.int32, (gl, gl), 1) // RWKV_HEAD)

    def blk(x):
        return jnp.concatenate([x.astype(BF16)] * (gl // RWKV_HEAD), axis=0) * blk_mask

    t_len = WKV_CHUNK
    items_f, items_b = _staggered(
        _wkv_pre(rf_ref, kf_ref, vf_ref, a0_f, lw_f, k_k, k_a, ones_bd, blk, head_mask, gl, reverse=False),
        _wkv_pre(rb_ref, kb_ref, vb_ref, a1_b, lw_b, k_k, k_a, ones_bd, blk, head_mask, gl, reverse=True))
    chains = [(items_f, yf_ref, stf_ref), (items_b, yb_ref, stb_ref)]
    states = [st_ref[...] for _, _, st_ref in chains]

    def emit_y(y_ref, it, sa, rs):
        y_ref[it["rows"], :] = rs + jnp.dot(it["a_r"], jnp.concatenate([blk(sa), it["v_blk"]], axis=0),
                                            preferred_element_type=F32)

    pending = []
    for ci in range(len(chains[0][0])):
        st_bf = [st.astype(BF16) for st in states]
        its = [items[ci] for items, _, _ in chains]
        st_m = [jnp.dot(sb, it["m"], preferred_element_type=F32) for sb, it in zip(st_bf, its)]
        ws = [lax.dot_general(it["wr"], sb, NT, preferred_element_type=F32) for sb, it in zip(st_bf, its)]
        for args in pending:
            emit_y(*args)
        pending = [(y_ref, it, -(w[:t_len] + it["u"]), w[t_len:])
                   for (_, y_ref, _), it, w in zip(chains, its, ws)]
        states = [st * it["decay"] - sm + it["c"] for st, it, sm in zip(states, its, st_m)]
    for args in pending:
        emit_y(*args)
    for (_, _, st_ref), st in zip(chains, states):
        st_ref[...] = st


def _wkv2(r, k, v, tw, ta, w2_cat, a2_cat, par, batch, seq):
    m, d = r.shape
    gl = WKV_GROUP * RWKV_HEAD
    lr2 = tw.shape[1]
    tb = min(WKV_BLOCK, seq)
    nb = seq // tb
    fwd = pl.BlockSpec((tb, gl), lambda bi, g, j: (bi * nb + j, g))
    bwd = pl.BlockSpec((tb, gl), lambda bi, g, j: (bi * nb + nb - 1 - j, g))
    fwd_lr = pl.BlockSpec((tb, lr2), lambda bi, g, j: (bi * nb + j, 0))
    bwd_lr = pl.BlockSpec((tb, lr2), lambda bi, g, j: (bi * nb + nb - 1 - j, 0))
    col = lambda rows: pl.BlockSpec((rows, gl), lambda bi, g, j: (0, g))
    out = jax.ShapeDtypeStruct((m, d), F32)
    return pl.pallas_call(
        _wkv2_kernel,
        grid=(batch, d // gl, nb),
        in_specs=[fwd] * 3 + [fwd_lr] * 2 + [bwd] * 3 + [bwd_lr] * 2 + [col(lr2), col(lr2), col(par.shape[0])],
        out_specs=[fwd, bwd, fwd],
        out_shape=[out, out, out],
        scratch_shapes=[pltpu.VMEM((gl, gl), F32), pltpu.VMEM((gl, gl), F32)],
        compiler_params=_params("parallel", "parallel", "arbitrary"),
        name="wkv",
    )(r, k, v, tw, ta, r, k, v, tw, ta, w2_cat, a2_cat, par)


def _rwkv_out_kernel(y0_ref, y1_ref, bonus_ref, sg_ref, g2_ref, lg_ref, lb_ref, wo_ref, x_ref, g_ref, o_ref):
    width = 2 * LANES
    ones_bd = _head_ones(width)
    d = y0_ref.shape[1]
    inv_n = 1.0 / RWKV_HEAD

    def head_sum(z):
        return jnp.dot(z.astype(BF16), ones_bd, preferred_element_type=F32)

    gate = jnp.dot(sg_ref[...], g2_ref[...], preferred_element_type=F32)
    gated = []
    for c in range(d // width):
        sl = slice(c * width, (c + 1) * width)
        y = y0_ref[:, sl] + y1_ref[:, sl]
        mean = head_sum(y) * inv_n
        yc = y - mean
        var = head_sum(yc * yc) * inv_n
        yn = yc * lax.rsqrt(var + LNX_EPS) * lg_ref[:, sl] + lb_ref[:, sl]
        gated.append(((yn + bonus_ref[:, sl]) * gate[:, sl]).astype(BF16))
    h = jnp.dot(jnp.concatenate(gated, axis=1), wo_ref[...], preferred_element_type=F32)
    o_ref[...] = x_ref[...] + _rms(h, g_ref[...])


def _rwkv_out(y0, y1, bonus, sg, g2, lnx_g, lnx_b, w_o, x, g_post):
    m, d = y0.shape
    lr = sg.shape[1]
    tm = min(256, m)
    row = pl.BlockSpec((tm, d), lambda i: (i, 0))
    par = pl.BlockSpec((1, d), lambda i: (0, 0))
    fix = lambda i: (0, 0)
    return pl.pallas_call(
        _rwkv_out_kernel,
        grid=(m // tm,),
        in_specs=[row] * 3 + [pl.BlockSpec((tm, lr), lambda i: (i, 0)),
                              pl.BlockSpec((lr, d), fix, pipeline_mode=pl.Buffered(1)), par, par,
                              pl.BlockSpec((d, d), fix, pipeline_mode=pl.Buffered(1)), row, par],
        out_specs=row,
        out_shape=jax.ShapeDtypeStruct((m, d), F32),
        compiler_params=_params("parallel"),
        name="rwkv_out",
    )(y0, y1, bonus, sg, g2, lnx_g, lnx_b, w_o, x, g_post)


def _rope_tables(seq):
    half = ROPE_DIM // 2
    inv = 1.0 / (ROPE_THETA ** (jnp.arange(half, dtype=F32) * (2.0 / ROPE_DIM)))
    ang = jnp.arange(seq, dtype=F32)[:, None] * inv[None, :]
    cos, sin = jnp.cos(ang), jnp.sin(ang)
    zero = jnp.zeros((seq, LANES - ROPE_DIM), F32)
    return (jnp.concatenate([cos, cos, zero], axis=1), jnp.concatenate([-sin, sin, zero], axis=1))


def _row(v):
    return v.reshape(1, -1).astype(F32)


def _pad_cols(w, n):
    return jnp.pad(w, ((0, 0), (0, n - w.shape[1])))


def _pad_rows(w, n):
    return jnp.pad(w, ((0, n - w.shape[0]), (0, 0)))


def _mla_layer(x, batch, seq, g_pre, g_post, w_in, g_q, g_kv, w_uq, w_ukv, w_o):
    cosw, sinw = _rope_tables(seq)
    w_in_pad = _pad_cols(w_in, Q_LORA + KV_LORA + LANES).astype(BF16)
    w_uq_pad = jnp.pad(w_uq.reshape(Q_LORA, MLA_HEADS, NOPE_DIM + ROPE_DIM),
                       ((0, 0), (0, 0), (0, QK_PAD - NOPE_DIM - ROPE_DIM))).reshape(Q_LORA, MLA_HEADS * QK_PAD)
    cq, ckv, kpe = _mla_in(x, _row(g_pre), w_in_pad, _row(g_q), _row(g_kv), cosw, sinw, seq)
    qt = _mla_q(cq, w_uq_pad.astype(BF16), cosw, sinw, seq)
    k, vt = _mla_kv(ckv, kpe, w_ukv.astype(BF16))
    o = _attention(qt, k, vt, batch, seq)
    return _proj_res(o, w_o.astype(BF16), x, _row(g_post), "mla_out")


def _rwkv_layer(x, batch, seq, g_pre, g_post, mu, w_r, w_k, w_v, w_o, w0, w1, w2, a0, a1, a2, g1, g2,
                k_k, k_a, r_k, lnx_g, lnx_b):
    mu8 = jnp.pad(mu, ((0, 8 - mu.shape[0]), (0, 0)))
    w1_cat = jnp.concatenate([_pad_cols(w1[di], LANES) for di in range(2)], axis=1).astype(BF16)
    a1_cat = jnp.concatenate([_pad_cols(a1[di], LANES) for di in range(2)], axis=1).astype(BF16)
    w2_cat = jnp.concatenate([_pad_rows(w2[di], LANES) for di in range(2)], axis=0).astype(BF16)
    a2_cat = jnp.concatenate([_pad_rows(a2[di], LANES) for di in range(2)], axis=0).astype(BF16)
    par = jnp.stack([k_k, k_a, r_k, w0[0], w0[1], a0[0], a0[1], jnp.zeros_like(k_k)]).astype(F32)
    r, k, v, tw, ta, sg = _rwkv_mix(x, _row(g_pre), mu8, w_r.astype(BF16), w_k.astype(BF16), w_v.astype(BF16),
                                    w1_cat, a1_cat, g1.astype(BF16), seq)
    y0, y1, bonus = _wkv2(r, k, v, tw, ta, w2_cat, a2_cat, par, batch, seq)
    return _rwkv_out(y0, y1, bonus, sg, g2.astype(BF16), _row(lnx_g), _row(lnx_b), w_o.astype(BF16), x,
                     _row(g_post))


def _trunk(x3, norm_g, mla_w_in, mla_g_q, mla_g_kv, mla_w_uq, mla_w_ukv, mla_w_o,
           rwkv_mu, rwkv_w_r, rwkv_w_k, rwkv_w_v, rwkv_w_o, rwkv_w0, rwkv_w1, rwkv_w2,
           rwkv_a0, rwkv_a1, rwkv_a2, rwkv_g1, rwkv_g2, rwkv_k_k, rwkv_k_a, rwkv_r_k,
           rwkv_lnx_g, rwkv_lnx_b, ffn_w_gu, ffn_w_down):
    batch, seq, d = x3.shape
    x = x3.reshape(batch * seq, d)
    depth = norm_g.shape[0]
    for i in range(depth):
        j = i // 2
        if i % 2 == 0:
            x = _mla_layer(x, batch, seq, norm_g[i, 0], norm_g[i, 1], mla_w_in[j], mla_g_q[j], mla_g_kv[j],
                           mla_w_uq[j], mla_w_ukv[j], mla_w_o[j])
        else:
            x = _rwkv_layer(x, batch, seq, norm_g[i, 0], norm_g[i, 1], rwkv_mu[j], rwkv_w_r[j], rwkv_w_k[j],
                            rwkv_w_v[j], rwkv_w_o[j], rwkv_w0[j], rwkv_w1[j], rwkv_w2[j], rwkv_a0[j],
                            rwkv_a1[j], rwkv_a2[j], rwkv_g1[j], rwkv_g2[j], rwkv_k_k[j], rwkv_k_a[j],
                            rwkv_r_k[j].reshape(-1), rwkv_lnx_g[j], rwkv_lnx_b[j])
        x = _ffn(x, _row(norm_g[i, 2]), ffn_w_gu[i].astype(BF16), ffn_w_down[i].astype(BF16), _row(norm_g[i, 3]))
    return x.reshape(batch, seq, d)


def kernel(x_prompt, x_sample, norm_g, mla_w_in, mla_g_q, mla_g_kv, mla_w_uq, mla_w_ukv, mla_w_o, rwkv_mu, rwkv_w_r, rwkv_w_k, rwkv_w_v, rwkv_w_o, rwkv_w0, rwkv_w1, rwkv_w2, rwkv_a0, rwkv_a1, rwkv_a2, rwkv_g1, rwkv_g2, rwkv_k_k, rwkv_k_a, rwkv_r_k, rwkv_lnx_g, rwkv_lnx_b, ffn_w_gu, ffn_w_down):
    params = (norm_g, mla_w_in, mla_g_q, mla_g_kv, mla_w_uq, mla_w_ukv, mla_w_o,
              rwkv_mu, rwkv_w_r, rwkv_w_k, rwkv_w_v, rwkv_w_o, rwkv_w0, rwkv_w1, rwkv_w2,
              rwkv_a0, rwkv_a1, rwkv_a2, rwkv_g1, rwkv_g2, rwkv_k_k, rwkv_k_a, rwkv_r_k,
              rwkv_lnx_g, rwkv_lnx_b, ffn_w_gu, ffn_w_down)
    return (_trunk(x_prompt, *params), _trunk(x_sample, *params))
```

```python
import functools
import math

import jax
import jax.numpy as jnp
from jax import lax
from jax.experimental import pallas as pl
from jax.experimental.pallas import tpu as pltpu

F32 = jnp.float32
BF16 = jnp.bfloat16

NORM_EPS = 1e-6
LNX_EPS = 64e-5
ROPE_THETA = 10000.0

MLA_HEADS = 16
Q_LORA = 512
KV_LORA = 512
NOPE_DIM = 128
ROPE_DIM = 64
V_DIM = 128
QK_PAD = 256
ONES_ROWS = 16
RWKV_HEAD = 64

LANES = 128
VMEM_LIMIT = 56 * 1024 * 1024

ROWS_WIDE = 512
ROWS_NARROW = 256
ATTN_TQ = 2048
ATTN_TK = 2048
ATTN_GROUP = 256
FFN_TF = 512

WKV_CHUNK = 64
WKV_GROUP = 4
WKV_BLOCK = 1024

NT = (((1,), (1,)), ((), ()))
TN = (((0,), (0,)), ((), ()))


def _params(*sem):
    return pltpu.CompilerParams(dimension_semantics=sem, vmem_limit_bytes=VMEM_LIMIT)


def _rms(x, g):
    return x * lax.rsqrt(jnp.mean(x * x, axis=-1, keepdims=True) + NORM_EPS) * g


def _sigmoid(z):
    return 1.0 / (1.0 + jnp.exp(-z))


def _rope_upper(up, cosw, sinw):
    lane = lax.broadcasted_iota(jnp.int32, up.shape, 1)
    swapped = jnp.where(lane < ROPE_DIM // 2, pltpu.roll(up, LANES - ROPE_DIM // 2, 1),
                        pltpu.roll(up, ROPE_DIM // 2, 1))
    return up * cosw + swapped * sinw


def _head_ones(n):
    r = lax.broadcasted_iota(jnp.int32, (n, n), 0) // RWKV_HEAD
    c = lax.broadcasted_iota(jnp.int32, (n, n), 1) // RWKV_HEAD
    return jnp.where(r == c, 1.0, 0.0).astype(BF16)


def _mla_in_kernel(x_ref, g_ref, w_ref, gq_ref, gkv_ref, cos_ref, sin_ref, cq_ref, ckv_ref, kpe_ref):
    xn = _rms(x_ref[...], g_ref[...]).astype(BF16)
    h = jnp.dot(xn, w_ref[...], preferred_element_type=F32)
    cq_ref[...] = _rms(h[:, :Q_LORA], gq_ref[...]).T.astype(BF16)
    ckv_ref[...] = _rms(h[:, Q_LORA:Q_LORA + KV_LORA], gkv_ref[...]).astype(BF16)
    kpe_ref[...] = _rope_upper(h[:, Q_LORA + KV_LORA:], cos_ref[...], sin_ref[...]).astype(BF16)


def _mla_in(x, g, w_in_pad, g_q, g_kv, cosw, sinw, seq):
    m, d = x.shape
    tm = min(ROWS_WIDE, seq)
    nseq = seq // tm
    n = w_in_pad.shape[1]
    row = lambda i: (i, 0)
    fix = lambda i: (0, 0)
    tab = lambda i: (i % nseq, 0)
    return pl.pallas_call(
        _mla_in_kernel,
        grid=(m // tm,),
        in_specs=[pl.BlockSpec((tm, d), row), pl.BlockSpec((1, d), fix), pl.BlockSpec((d, n), fix),
                  pl.BlockSpec((1, Q_LORA), fix), pl.BlockSpec((1, KV_LORA), fix),
                  pl.BlockSpec((tm, LANES), tab), pl.BlockSpec((tm, LANES), tab)],
        out_specs=[pl.BlockSpec((Q_LORA, tm), lambda i: (0, i)), pl.BlockSpec((tm, KV_LORA), row),
                   pl.BlockSpec((tm, LANES), row)],
        out_shape=[jax.ShapeDtypeStruct((Q_LORA, m), BF16), jax.ShapeDtypeStruct((m, KV_LORA), BF16),
                   jax.ShapeDtypeStruct((m, LANES), BF16)],
        compiler_params=_params("parallel"),
        name="mla_in",
    )(x, g, w_in_pad, g_q, g_kv, cosw, sinw)


def _mla_q_kernel(cqt_ref, wt_ref, cos_ref, sin_ref, q_ref, *, scale):
    res = jnp.dot(wt_ref[...], cqt_ref[...], preferred_element_type=F32)
    cos = cos_ref[...]
    sin = sin_ref[...]
    half = ROPE_DIM // 2
    head = NOPE_DIM + ROPE_DIM
    for h in range(MLA_HEADS):
        src = h * head
        dst = h * QK_PAD
        x1 = res[src + NOPE_DIM:src + NOPE_DIM + half]
        x2 = res[src + NOPE_DIM + half:src + head]
        q_ref[dst:dst + NOPE_DIM, :] = (res[src:src + NOPE_DIM] * scale).astype(BF16)
        q_ref[dst + NOPE_DIM:dst + NOPE_DIM + half, :] = ((x1 * cos - x2 * sin) * scale).astype(BF16)
        q_ref[dst + NOPE_DIM + half:dst + head, :] = ((x1 * sin + x2 * cos) * scale).astype(BF16)
        q_ref[dst + head:dst + QK_PAD, :] = jnp.zeros((QK_PAD - head, res.shape[1]), BF16)


def _mla_q(cqt, w_uq_t, cos_t, sin_t, seq):
    c, m = cqt.shape
    n = w_uq_t.shape[0]
    tm = min(ROWS_NARROW, seq)
    nseq = seq // tm
    half = ROPE_DIM // 2
    scale = (NOPE_DIM + ROPE_DIM) ** -0.5 * math.log2(math.e)
    return pl.pallas_call(
        functools.partial(_mla_q_kernel, scale=scale),
        grid=(m // tm,),
        in_specs=[pl.BlockSpec((c, tm), lambda i: (0, i)), pl.BlockSpec((n, c), lambda i: (0, 0)),
                  pl.BlockSpec((half, tm), lambda i: (0, i % nseq)),
                  pl.BlockSpec((half, tm), lambda i: (0, i % nseq))],
        out_specs=pl.BlockSpec((MLA_HEADS * QK_PAD, tm), lambda i: (0, i)),
        out_shape=jax.ShapeDtypeStruct((MLA_HEADS * QK_PAD, m), BF16),
        compiler_params=_params("parallel"),
        name="mla_q",
    )(cqt, w_uq_t, cos_t, sin_t)


def _mla_kv_kernel(ckv_ref, kpe_ref, w_ref, k_ref, vt_ref):
    res = jnp.dot(ckv_ref[...], w_ref[...], preferred_element_type=F32)
    kpe = kpe_ref[...]
    for h in range(MLA_HEADS):
        k_ref[h, :, :NOPE_DIM] = res[:, 2 * h * NOPE_DIM:(2 * h + 1) * NOPE_DIM].astype(BF16)
        k_ref[h, :, NOPE_DIM:] = kpe
        vt_ref[h * V_DIM:(h + 1) * V_DIM, :] = res[:, (2 * h + 1) * V_DIM:(2 * h + 2) * V_DIM].T.astype(BF16)


def _mla_kv(ckv, kpe, w_ukv):
    m, c = ckv.shape
    n = w_ukv.shape[1]
    tm = min(ROWS_NARROW, m)
    return pl.pallas_call(
        _mla_kv_kernel,
        grid=(m // tm,),
        in_specs=[pl.BlockSpec((tm, c), lambda i: (i, 0)), pl.BlockSpec((tm, LANES), lambda i: (i, 0)),
                  pl.BlockSpec((c, n), lambda i: (0, 0))],
        out_specs=[pl.BlockSpec((MLA_HEADS, tm, QK_PAD), lambda i: (0, i, 0)),
                   pl.BlockSpec((n // 2, tm), lambda i: (0, i))],
        out_shape=[jax.ShapeDtypeStruct((MLA_HEADS, m, QK_PAD), BF16), jax.ShapeDtypeStruct((n // 2, m), BF16)],
        compiler_params=_params("parallel"),
        name="mla_kv",
    )(ckv, kpe, w_ukv)


def _attn_kernel(qt_ref, qtn_ref, k0_ref, k_ref, vt_ref, o_ref, sa_sc, sb_sc, xa_sc, xb_sc, m_sc, acc_sc,
                 *, n_split, n_kv):
    i = pl.program_id(2)
    j = pl.program_id(3)
    width = qt_ref.shape[1] // n_split
    last = n_kv - 1
    carry = n_kv % 2 == 0
    buf_a = (sa_sc, xa_sc)
    buf_b = (sb_sc, xb_sc)

    def score(k, q_ref, dst, cols):
        s = jnp.dot(k, q_ref[:, cols], preferred_element_type=F32)
        dst[0][:, cols] = s
        dst[1][:, cols] = jnp.max(s, axis=0, keepdims=True)

    @pl.when(j == 0)
    def _():
        m_sc[...] = jnp.full(m_sc.shape, -jnp.inf, F32)
        acc_sc[...] = jnp.zeros(acc_sc.shape, F32)

    @pl.when((j == 0) & (i == 0) if carry else (j == 0))
    def _():
        k0 = k0_ref[0]
        for c in range(n_split):
            score(k0, qt_ref, buf_a, slice(c * width, (c + 1) * width))

    def step(cur, nxt, k_next_ref, q_next_ref):
        k = k_next_ref[0]
        vt = vt_ref[...]
        vt1 = jnp.concatenate([vt, jnp.ones((ONES_ROWS, vt.shape[1]), BF16)], axis=0)
        for c in range(n_split):
            cols = slice(c * width, (c + 1) * width)
            if nxt is not None:
                score(k, q_next_ref, nxt, cols)
            m_prev = m_sc[:, cols]
            m_new = jnp.maximum(m_prev, cur[1][:, cols])
            alpha = jnp.exp2(m_prev - m_new)
            pt = jnp.exp2((cur[0][:, cols] - m_new).astype(BF16))
            acc_sc[:, cols] = alpha * acc_sc[:, cols] + jnp.dot(vt1, pt, preferred_element_type=F32)
            m_sc[:, cols] = m_new

    @pl.when((j % 2 == 0) & (j < last))
    def _():
        step(buf_a, buf_b, k_ref, qt_ref)

    @pl.when((j % 2 == 1) & (j < last))
    def _():
        step(buf_b, buf_a, k_ref, qt_ref)

    @pl.when(j == last)
    def _():
        if carry:
            step(buf_b, buf_a, k0_ref, qtn_ref)
        else:
            step(buf_a, None, k_ref, qt_ref)
        o_ref[...] = (acc_sc[:V_DIM, :] / acc_sc[V_DIM:V_DIM + 1, :]).T.astype(o_ref.dtype)


def _attention(qt, k, vt, batch, seq):
    m = k.shape[1]
    tq = min(ATTN_TQ, seq)
    tk = min(ATTN_TK, seq)
    nq = seq // tq
    nk = seq // tk
    return pl.pallas_call(
        functools.partial(_attn_kernel, n_split=max(tq // ATTN_GROUP, 1), n_kv=nk),
        grid=(batch, MLA_HEADS, nq, nk),
        in_specs=[pl.BlockSpec((QK_PAD, tq), lambda b, h, i, j: (h, b * nq + i)),
                  pl.BlockSpec((QK_PAD, tq), lambda b, h, i, j: (h, b * nq + jnp.minimum(i + 1, nq - 1))),
                  pl.BlockSpec((1, tk, QK_PAD), lambda b, h, i, j: (h, b * nk, 0)),
                  pl.BlockSpec((1, tk, QK_PAD), lambda b, h, i, j: (h, b * nk + jnp.minimum(j + 1, nk - 1), 0)),
                  pl.BlockSpec((V_DIM, tk), lambda b, h, i, j: (h, b * nk + j))],
        out_specs=pl.BlockSpec((tq, V_DIM), lambda b, h, i, j: (b * nq + i, h)),
        out_shape=jax.ShapeDtypeStruct((m, MLA_HEADS * V_DIM), BF16),
        scratch_shapes=[pltpu.VMEM((tk, tq), F32), pltpu.VMEM((tk, tq), F32),
                        pltpu.VMEM((1, tq), F32), pltpu.VMEM((1, tq), F32),
                        pltpu.VMEM((1, tq), F32), pltpu.VMEM((V_DIM + ONES_ROWS, tq), F32)],
        compiler_params=_params("parallel", "parallel", "arbitrary", "arbitrary"),
        name="mla_attention",
    )(qt, qt, k, k, vt)


def _proj_res_kernel(a_ref, w_ref, x_ref, g_ref, o_ref):
    h = jnp.dot(a_ref[...], w_ref[...], preferred_element_type=F32)
    o_ref[...] = x_ref[...] + _rms(h, g_ref[...])


def _proj_res(a, w, x, g, name):
    m, kdim = a.shape
    n = w.shape[1]
    tm = min(ROWS_WIDE, m)
    return pl.pallas_call(
        _proj_res_kernel,
        grid=(m // tm,),
        in_specs=[pl.BlockSpec((tm, kdim), lambda i: (i, 0)), pl.BlockSpec((kdim, n), lambda i: (0, 0)),
                  pl.BlockSpec((tm, n), lambda i: (i, 0)), pl.BlockSpec((1, n), lambda i: (0, 0))],
        out_specs=pl.BlockSpec((tm, n), lambda i: (i, 0)),
        out_shape=jax.ShapeDtypeStruct((m, n), F32),
        compiler_params=_params("parallel"),
        name=name,
    )(a, w, x, g)


def _ffn_kernel(x_ref, g_pre_ref, wg_ref, wu_ref, wd_ref, g_post_ref, o_ref, xn_sc, acc_sc):
    f = pl.program_id(1)

    @pl.when(f == 0)
    def _():
        xn_sc[...] = _rms(x_ref[...], g_pre_ref[...]).astype(BF16)
        acc_sc[...] = jnp.zeros(acc_sc.shape, F32)

    xn = xn_sc[...]
    half = wg_ref.shape[1] // 2
    acts = []
    for c in range(2):
        cols = slice(c * half, (c + 1) * half)
        gate = jnp.dot(xn, wg_ref[:, cols], preferred_element_type=F32)
        up = jnp.dot(xn, wu_ref[:, cols], preferred_element_type=F32)
        acts.append((gate * _sigmoid(gate) * up).astype(BF16))
    acc_sc[...] += jnp.dot(jnp.concatenate(acts, axis=1), wd_ref[...], preferred_element_type=F32)

    @pl.when(f == pl.num_programs(1) - 1)
    def _():
        o_ref[...] = x_ref[...] + _rms(acc_sc[...], g_post_ref[...])


def _ffn(x, g_pre, w_gu, w_down, g_post):
    m, d = x.shape
    d_ff = w_down.shape[0]
    tm = min(ROWS_WIDE, m)
    tf = FFN_TF
    nf = d_ff // tf
    return pl.pallas_call(
        _ffn_kernel,
        grid=(m // tm, nf),
        in_specs=[pl.BlockSpec((tm, d), lambda i, f: (i, 0)), pl.BlockSpec((1, d), lambda i, f: (0, 0)),
                  pl.BlockSpec((d, tf), lambda i, f: (0, f)), pl.BlockSpec((d, tf), lambda i, f: (0, f + nf)),
                  pl.BlockSpec((tf, d), lambda i, f: (f, 0)), pl.BlockSpec((1, d), lambda i, f: (0, 0))],
        out_specs=pl.BlockSpec((tm, d), lambda i, f: (i, 0)),
        out_shape=jax.ShapeDtypeStruct((m, d), F32),
        scratch_shapes=[pltpu.VMEM((tm, d), BF16), pltpu.VMEM((tm, d), F32)],
        compiler_params=_params("parallel", "arbitrary"),
        name="ffn",
    )(x, g_pre, w_gu, w_gu, w_down, g_post)


def _rwkv_mix_kernel(x_ref, xp_ref, xn_ref, g_ref, mu_ref, wr_ref, wk_ref, wv_ref, w1_ref, a1_ref, g1_ref,
                     r_ref, k_ref, v_ref, tw_ref, ta_ref, sg_ref, *, tiles_per_seq):
    i = pl.program_id(0)
    g = g_ref[...]
    h = _rms(x_ref[...], g)
    tm = h.shape[0]
    first = (i % tiles_per_seq) == 0
    last = (i % tiles_per_seq) == tiles_per_seq - 1
    hp = jnp.where(first, 0.0, _rms(xp_ref[7:8, :], g))
    hn = jnp.where(last, 0.0, _rms(xn_ref[0:1, :], g))
    row = lax.broadcasted_iota(jnp.int32, h.shape, 0)
    h_prev = jnp.where(row == 0, hp, pltpu.roll(h, 1, 0))
    h_next = jnp.where(row == tm - 1, hn, pltpu.roll(h, tm - 1, 0))
    xx = 0.5 * (h_prev + h_next) - h

    def lerp(idx):
        return (h + xx * mu_ref[idx:idx + 1, :]).astype(BF16)

    r_ref[...] = jnp.dot(lerp(0), wr_ref[...], preferred_element_type=F32)
    k_ref[...] = jnp.dot(lerp(2), wk_ref[...], preferred_element_type=F32)
    v_ref[...] = jnp.dot(lerp(3), wv_ref[...], preferred_element_type=F32)
    tw_ref[...] = jnp.tanh(jnp.dot(lerp(1), w1_ref[...], preferred_element_type=F32)).astype(BF16)
    ta_ref[...] = jnp.dot(lerp(4), a1_ref[...], preferred_element_type=F32).astype(BF16)
    sg_ref[...] = _sigmoid(jnp.dot(lerp(5), g1_ref[...], preferred_element_type=F32)).astype(BF16)


def _rwkv_mix(x, g, mu, w_r, w_k, w_v, w1_cat, a1_cat, g1, seq):
    m, d = x.shape
    lr = w1_cat.shape[1]
    tm = min(ROWS_NARROW, seq)
    tps = seq // tm
    nb8 = m // 8
    r8 = tm // 8
    row = lambda i: (i, 0)
    fix = lambda i: (0, 0)
    weight = pl.BlockSpec((d, d), fix, pipeline_mode=pl.Buffered(1))
    weight_lr = pl.BlockSpec((d, lr), fix, pipeline_mode=pl.Buffered(1))
    return pl.pallas_call(
        functools.partial(_rwkv_mix_kernel, tiles_per_seq=tps),
        grid=(m // tm,),
        in_specs=[pl.BlockSpec((tm, d), row),
                  pl.BlockSpec((8, d), lambda i: (jnp.maximum(i * r8 - 1, 0), 0)),
                  pl.BlockSpec((8, d), lambda i: (jnp.minimum((i + 1) * r8, nb8 - 1), 0)),
                  pl.BlockSpec((1, d), fix), pl.BlockSpec((8, d), fix), weight, weight, weight,
                  weight_lr, weight_lr, weight_lr],
        out_specs=[pl.BlockSpec((tm, d), row)] * 3 + [pl.BlockSpec((tm, lr), row)] * 3,
        out_shape=[jax.ShapeDtypeStruct((m, d), F32)] * 3 + [jax.ShapeDtypeStruct((m, lr), BF16)] * 3,
        compiler_params=_params("parallel"),
        name="rwkv_mix",
    )(x, x, x, g, mu, w_r, w_k, w_v, w1_cat, a1_cat, g1)


def _wkv_pre(r_ref, k_ref, v_ref, a_ref, lw_ref, k_k, k_a, ones_bd, blk, head_mask, gl, *, reverse):
    t_len = WKV_CHUNK
    ng = gl // RWKV_HEAD
    gt = ng * t_len
    n_chunks = r_ref.shape[0] // t_len

    wrow = lax.broadcasted_iota(jnp.int32, (t_len, gt), 0)
    wcol = lax.broadcasted_iota(jnp.int32, (t_len, gt), 1) % t_len
    strict = (wcol > wrow) if reverse else (wcol < wrow)
    incl = (wcol >= wrow) if reverse else (wcol <= wrow)
    eye_w = jnp.where(wcol == wrow, 1.0, 0.0)
    trow = lax.broadcasted_iota(jnp.int32, (t_len, t_len), 0)
    tcol = lax.broadcasted_iota(jnp.int32, (t_len, t_len), 1)
    tri = jnp.where((tcol >= trow) if reverse else (tcol <= trow), 1.0, 0.0).astype(BF16)
    bd_mask = jnp.where(lax.broadcasted_iota(jnp.int32, (gt, gt), 0) // t_len
                        == lax.broadcasted_iota(jnp.int32, (gt, gt), 1) // t_len, 1.0, 0.0).astype(BF16)

    def bdiag(w):
        return jnp.concatenate([w.astype(BF16)] * ng, axis=0) * bd_mask

    def mmb(a, b_bf16):
        return jnp.dot(a.astype(BF16), b_bf16, preferred_element_type=F32)

    items = []
    for ci in range(n_chunks):
        c = (n_chunks - 1 - ci) if reverse else ci
        items.append({"rows": slice(c * t_len, (c + 1) * t_len)})

    for it in items:
        lw = lw_ref[it["rows"], :]
        lw_hi = lw.astype(BF16)
        lw_lo = (lw - lw_hi.astype(F32)).astype(BF16)
        it["lw"] = lw
        cum2 = jnp.dot(tri, jnp.concatenate([lw_hi, lw_lo], axis=1), preferred_element_type=F32)
        it["cum"] = cum2[:, :gl] + cum2[:, gl:]
    yield

    kkr_all = k_ref[...] * k_k
    nrm_all = jnp.maximum(jnp.sqrt(jnp.dot((kkr_all * kkr_all).astype(BF16), ones_bd,
                                           preferred_element_type=F32)), 1e-12)
    kk_all = kkr_all / nrm_all

    for it in items:
        rows, cum, lw = it["rows"], it["cum"], it["lw"]
        a = a_ref[rows, :]
        kk = kk_all[rows]
        kd = k_ref[rows, :] * (1.0 + (a - 1.0) * k_a)
        b = kk * a
        tot = cum[0:1] if reverse else cum[t_len - 1:t_len]
        inv_p = jnp.exp(-cum)
        rt = r_ref[rows, :] * jnp.exp(cum)
        kkt = kk * jnp.exp(cum - lw)
        to_end = jnp.exp(tot - cum)
        it["decay"] = jnp.exp(tot)
        it["b_end"] = (b * to_end).astype(BF16)
        it["k_end"] = (kd * to_end).astype(BF16)
        it["rt"] = rt.astype(BF16)
        it["kkt_blk"] = blk(kkt)
        lhs = jnp.concatenate([kkt, rt], axis=0).astype(BF16)
        rhs = jnp.concatenate([blk(b * inv_p), blk(kd * inv_p)], axis=0)
        aa = lax.dot_general(lhs, rhs, NT, preferred_element_type=F32)
        it["a_ab"] = jnp.where(strict, aa[:t_len, :gt], 0.0)
        it["a_ak"] = jnp.where(strict, aa[:t_len, gt:], 0.0).astype(BF16)
        it["a_r"] = jnp.concatenate([jnp.where(incl, aa[t_len:, :gt], 0.0),
                                     jnp.where(incl, aa[t_len:, gt:], 0.0)], axis=1).astype(BF16)
    yield

    for it in items:
        it["x"] = eye_w - it["a_ab"]
        it["p"] = mmb(it["a_ab"], bdiag(it["a_ab"]))
    yield
    rounds = int(math.log2(t_len)) - 1
    for rnd in range(rounds):
        for it in items:
            p_bd = bdiag(it["p"])
            if rnd < rounds - 1:
                xp = mmb(jnp.concatenate([it["x"], it["p"]], axis=0), p_bd)
                it["x"] = it["x"] + xp[:t_len]
                it["p"] = xp[t_len:]
            else:
                it["x"] = it["x"] + mmb(it["x"], p_bd)
        yield

    for it in items:
        v = v_ref[it["rows"], :]
        it["v"] = v
        it["v_blk"] = blk(v)
        it["akv"] = jnp.dot(it["a_ak"], it["v_blk"], preferred_element_type=F32)
        it["x_b"] = it["x"].astype(BF16)
        it["w_k"] = jnp.dot(it["x_b"], it["kkt_blk"], preferred_element_type=F32)
    yield
    for it in items:
        it["u"] = jnp.dot(it["x_b"], blk(it["akv"]), preferred_element_type=F32)
    yield
    for it in items:
        m_full = lax.dot_general(it["w_k"].astype(BF16), it["b_end"], TN, preferred_element_type=F32)
        it["m"] = jnp.where(head_mask, m_full, 0.0).astype(BF16)
    yield
    for it in items:
        c_full = lax.dot_general(jnp.concatenate([-it["u"], it["v"]], axis=0).astype(BF16),
                                 jnp.concatenate([it["b_end"], it["k_end"]], axis=0),
                                 TN, preferred_element_type=F32)
        it["c"] = jnp.where(head_mask, c_full, 0.0)
        it["wr"] = jnp.concatenate([it["w_k"].astype(BF16), it["rt"]], axis=0)
    return items


def _staggered(first, second):
    results = [None, None]
    live = [first, second]
    next(first)
    while any(g is not None for g in live):
        for idx in (1, 0):
            if live[idx] is not None:
                try:
                    next(live[idx])
                except StopIteration as stop:
                    results[idx] = stop.value
                    live[idx] = None
    return results


def _wkv2_kernel(rf_ref, kf_ref, vf_ref, twf_ref, taf_ref, rb_ref, kb_ref, vb_ref, twb_ref, tab_ref,
                 w2_ref, a2_ref, par_ref, yf_ref, yb_ref, bonus_ref, stf_ref, stb_ref):
    @pl.when(pl.program_id(2) == 0)
    def _():
        stf_ref[...] = jnp.zeros(stf_ref.shape, F32)
        stb_ref[...] = jnp.zeros(stb_ref.shape, F32)

    gl = stf_ref.shape[0]
    gt = (gl // RWKV_HEAD) * WKV_CHUNK
    lr = w2_ref.shape[0] // 2
    ones_bd = _head_ones(gl)
    k_k, k_a, r_k = par_ref[0:1, :], par_ref[1:2, :], par_ref[2:3, :]

    def second_stage(t_ref, w_ref, di, bias_row):
        lo = di * lr
        return (jnp.dot(t_ref[:, lo:lo + lr], w_ref[lo:lo + lr, :], preferred_element_type=F32)
                + par_ref[bias_row + di:bias_row + di + 1, :])

    lw_f = -math.exp(-0.5) * _sigmoid(second_stage(twf_ref, w2_ref, 0, 3))
    lw_b = -math.exp(-0.5) * _sigmoid(second_stage(twb_ref, w2_ref, 1, 3))
    a0_f = _sigmoid(second_stage(taf_ref, a2_ref, 0, 5))
    a1_f = _sigmoid(second_stage(taf_ref, a2_ref, 1, 5))
    a1_b = _sigmoid(second_stage(tab_ref, a2_ref, 1, 5))

    kd_sum = kf_ref[...] * (2.0 + (a0_f + a1_f - 2.0) * k_a)
    bonus_ref[...] = jnp.dot((rf_ref[...] * kd_sum * r_k).astype(BF16), ones_bd,
                             preferred_element_type=F32) * vf_ref[...]

    blk_mask = jnp.where(lax.broadcasted_iota(jnp.int32, (gt, gl), 0) // WKV_CHUNK
                         == lax.broadcasted_iota(jnp.int32, (gt, gl), 1) // RWKV_HEAD, 1.0, 0.0).astype(BF16)
    head_mask = (lax.broadcasted_iota(jnp.int32, (gl, gl), 0) // RWKV_HEAD
                 == lax.broadcasted_iota(jnp.int32, (gl, gl), 1) // RWKV_HEAD)

    def blk(x):
        return jnp.concatenate([x.astype(BF16)] * (gl // RWKV_HEAD), axis=0) * blk_mask

    t_len = WKV_CHUNK
    items_f, items_b = _staggered(
        _wkv_pre(rf_ref, kf_ref, vf_ref, a0_f, lw_f, k_k, k_a, ones_bd, blk, head_mask, gl, reverse=False),
        _wkv_pre(rb_ref, kb_ref, vb_ref, a1_b, lw_b, k_k, k_a, ones_bd, blk, head_mask, gl, reverse=True))
    chains = [(items_f, yf_ref, stf_ref), (items_b, yb_ref, stb_ref)]
    states = [st_ref[...] for _, _, st_ref in chains]

    def emit_y(y_ref, it, sa, rs):
        y_ref[it["rows"], :] = rs + jnp.dot(it["a_r"], jnp.concatenate([blk(sa), it["v_blk"]], axis=0),
                                            preferred_element_type=F32)

    pending = []
    for ci in range(len(chains[0][0])):
        st_bf = [st.astype(BF16) for st in states]
        its = [items[ci] for items, _, _ in chains]
        st_m = [jnp.dot(sb, it["m"], preferred_element_type=F32) for sb, it in zip(st_bf, its)]
        ws = [lax.dot_general(it["wr"], sb, NT, preferred_element_type=F32) for sb, it in zip(st_bf, its)]
        for args in pending:
            emit_y(*args)
        pending = [(y_ref, it, -(w[:t_len] + it["u"]), w[t_len:])
                   for (_, y_ref, _), it, w in zip(chains, its, ws)]
        states = [st * it["decay"] - sm + it["c"] for st, it, sm in zip(states, its, st_m)]
    for args in pending:
        emit_y(*args)
    for (_, _, st_ref), st in zip(chains, states):
        st_ref[...] = st


def _wkv2(r, k, v, tw, ta, w2_cat, a2_cat, par, batch, seq):
    m, d = r.shape
    gl = WKV_GROUP * RWKV_HEAD
    lr2 = tw.shape[1]
    tb = min(WKV_BLOCK, seq)
    nb = seq // tb
    fwd = pl.BlockSpec((tb, gl), lambda bi, g, j: (bi * nb + j, g))
    bwd = pl.BlockSpec((tb, gl), lambda bi, g, j: (bi * nb + nb - 1 - j, g))
    fwd_lr = pl.BlockSpec((tb, lr2), lambda bi, g, j: (bi * nb + j, 0))
    bwd_lr = pl.BlockSpec((tb, lr2), lambda bi, g, j: (bi * nb + nb - 1 - j, 0))
    col = lambda rows: pl.BlockSpec((rows, gl), lambda bi, g, j: (0, g))
    out = jax.ShapeDtypeStruct((m, d), F32)
    return pl.pallas_call(
        _wkv2_kernel,
        grid=(batch, d // gl, nb),
        in_specs=[fwd] * 3 + [fwd_lr] * 2 + [bwd] * 3 + [bwd_lr] * 2 + [col(lr2), col(lr2), col(par.shape[0])],
        out_specs=[fwd, bwd, fwd],
        out_shape=[out, out, out],
        scratch_shapes=[pltpu.VMEM((gl, gl), F32), pltpu.VMEM((gl, gl), F32)],
        compiler_params=_params("parallel", "parallel", "arbitrary"),
        name="wkv",
    )(r, k, v, tw, ta, r, k, v, tw, ta, w2_cat, a2_cat, par)


def _rwkv_out_kernel(y0_ref, y1_ref, bonus_ref, sg_ref, g2_ref, lg_ref, lb_ref, wo_ref, x_ref, g_ref, o_ref):
    width = 2 * LANES
    ones_bd = _head_ones(width)
    d = y0_ref.shape[1]
    inv_n = 1.0 / RWKV_HEAD

    def head_sum(z):
        return jnp.dot(z.astype(BF16), ones_bd, preferred_element_type=F32)

    gate = jnp.dot(sg_ref[...], g2_ref[...], preferred_element_type=F32)
    gated = []
    for c in range(d // width):
        sl = slice(c * width, (c + 1) * width)
        y = y0_ref[:, sl] + y1_ref[:, sl]
        mean = head_sum(y) * inv_n
        yc = y - mean
        var = head_sum(yc * yc) * inv_n
        yn = yc * lax.rsqrt(var + LNX_EPS) * lg_ref[:, sl] + lb_ref[:, sl]
        gated.append(((yn + bonus_ref[:, sl]) * gate[:, sl]).astype(BF16))
    h = jnp.dot(jnp.concatenate(gated, axis=1), wo_ref[...], preferred_element_type=F32)
    o_ref[...] = x_ref[...] + _rms(h, g_ref[...])


def _rwkv_out(y0, y1, bonus, sg, g2, lnx_g, lnx_b, w_o, x, g_post):
    m, d = y0.shape
    lr = sg.shape[1]
    tm = min(ROWS_NARROW, m)
    row = pl.BlockSpec((tm, d), lambda i: (i, 0))
    par = pl.BlockSpec((1, d), lambda i: (0, 0))
    fix = lambda i: (0, 0)
    return pl.pallas_call(
        _rwkv_out_kernel,
        grid=(m // tm,),
        in_specs=[row] * 3 + [pl.BlockSpec((tm, lr), lambda i: (i, 0)),
                              pl.BlockSpec((lr, d), fix, pipeline_mode=pl.Buffered(1)), par, par,
                              pl.BlockSpec((d, d), fix, pipeline_mode=pl.Buffered(1)), row, par],
        out_specs=row,
        out_shape=jax.ShapeDtypeStruct((m, d), F32),
        compiler_params=_params("parallel"),
        name="rwkv_out",
    )(y0, y1, bonus, sg, g2, lnx_g, lnx_b, w_o, x, g_post)


def _rope_tables(seq):
    half = ROPE_DIM // 2
    inv = 1.0 / (ROPE_THETA ** (jnp.arange(half, dtype=F32) * (2.0 / ROPE_DIM)))
    ang = jnp.arange(seq, dtype=F32)[:, None] * inv[None, :]
    cos, sin = jnp.cos(ang), jnp.sin(ang)
    zero = jnp.zeros((seq, LANES - ROPE_DIM), F32)
    return (jnp.concatenate([cos, cos, zero], axis=1), jnp.concatenate([-sin, sin, zero], axis=1), cos.T, sin.T)


def _row(v):
    return v.reshape(1, -1).astype(F32)


def _pad_cols(w, n):
    return jnp.pad(w, ((0, 0), (0, n - w.shape[1])))


def _pad_rows(w, n):
    return jnp.pad(w, ((0, n - w.shape[0]), (0, 0)))


def _mla_layer(x, batch, seq, g_pre, g_post, w_in, g_q, g_kv, w_uq, w_ukv, w_o):
    cosw, sinw, cos_t, sin_t = _rope_tables(seq)
    w_in_pad = _pad_cols(w_in, Q_LORA + KV_LORA + LANES).astype(BF16)
    cqt, ckv, kpe = _mla_in(x, _row(g_pre), w_in_pad, _row(g_q), _row(g_kv), cosw, sinw, seq)
    qt = _mla_q(cqt, w_uq.T.astype(BF16), cos_t, sin_t, seq)
    k, vt = _mla_kv(ckv, kpe, w_ukv.astype(BF16))
    o = _attention(qt, k, vt, batch, seq)
    return _proj_res(o, w_o.astype(BF16), x, _row(g_post), "mla_out")


def _rwkv_layer(x, batch, seq, g_pre, g_post, mu, w_r, w_k, w_v, w_o, w0, w1, w2, a0, a1, a2, g1, g2,
                k_k, k_a, r_k, lnx_g, lnx_b):
    mu8 = jnp.pad(mu, ((0, 8 - mu.shape[0]), (0, 0)))
    w1_cat = jnp.concatenate([_pad_cols(w1[di], LANES) for di in range(2)], axis=1).astype(BF16)
    a1_cat = jnp.concatenate([_pad_cols(a1[di], LANES) for di in range(2)], axis=1).astype(BF16)
    w2_cat = jnp.concatenate([_pad_rows(w2[di], LANES) for di in range(2)], axis=0).astype(BF16)
    a2_cat = jnp.concatenate([_pad_rows(a2[di], LANES) for di in range(2)], axis=0).astype(BF16)
    par = jnp.stack([k_k, k_a, r_k, w0[0], w0[1], a0[0], a0[1], jnp.zeros_like(k_k)]).astype(F32)
    r, k, v, tw, ta, sg = _rwkv_mix(x, _row(g_pre), mu8, w_r.astype(BF16), w_k.astype(BF16), w_v.astype(BF16),
                                    w1_cat, a1_cat, g1.astype(BF16), seq)
    y0, y1, bonus = _wkv2(r, k, v, tw, ta, w2_cat, a2_cat, par, batch, seq)
    return _rwkv_out(y0, y1, bonus, sg, g2.astype(BF16), _row(lnx_g), _row(lnx_b), w_o.astype(BF16), x,
                     _row(g_post))


def _trunk(x3, norm_g, mla_w_in, mla_g_q, mla_g_kv, mla_w_uq, mla_w_ukv, mla_w_o,
           rwkv_mu, rwkv_w_r, rwkv_w_k, rwkv_w_v, rwkv_w_o, rwkv_w0, rwkv_w1, rwkv_w2,
           rwkv_a0, rwkv_a1, rwkv_a2, rwkv_g1, rwkv_g2, rwkv_k_k, rwkv_k_a, rwkv_r_k,
           rwkv_lnx_g, rwkv_lnx_b, ffn_w_gu, ffn_w_down):
    batch, seq, d = x3.shape
    for tile in (ROWS_WIDE, ATTN_TQ, ATTN_TK, WKV_BLOCK):
        assert seq % min(tile, seq) == 0, (seq, tile)
    assert seq % WKV_CHUNK == 0 and d % (WKV_GROUP * RWKV_HEAD) == 0, (seq, d)
    x = x3.reshape(batch * seq, d)
    depth = norm_g.shape[0]
    for i in range(depth):
        j = i // 2
        if i % 2 == 0:
            x = _mla_layer(x, batch, seq, norm_g[i, 0], norm_g[i, 1], mla_w_in[j], mla_g_q[j], mla_g_kv[j],
                           mla_w_uq[j], mla_w_ukv[j], mla_w_o[j])
        else:
            x = _rwkv_layer(x, batch, seq, norm_g[i, 0], norm_g[i, 1], rwkv_mu[j], rwkv_w_r[j], rwkv_w_k[j],
                            rwkv_w_v[j], rwkv_w_o[j], rwkv_w0[j], rwkv_w1[j], rwkv_w2[j], rwkv_a0[j],
                            rwkv_a1[j], rwkv_a2[j], rwkv_g1[j], rwkv_g2[j], rwkv_k_k[j], rwkv_k_a[j],
                            rwkv_r_k[j].reshape(-1), rwkv_lnx_g[j], rwkv_lnx_b[j])
        x = _ffn(x, _row(norm_g[i, 2]), ffn_w_gu[i].astype(BF16), ffn_w_down[i].astype(BF16), _row(norm_g[i, 3]))
    return x.reshape(batch, seq, d)


def kernel(x_prompt, x_sample, norm_g, mla_w_in, mla_g_q, mla_g_kv, mla_w_uq, mla_w_ukv, mla_w_o, rwkv_mu, rwkv_w_r, rwkv_w_k, rwkv_w_v, rwkv_w_o, rwkv_w0, rwkv_w1, rwkv_w2, rwkv_a0, rwkv_a1, rwkv_a2, rwkv_g1, rwkv_g2, rwkv_k_k, rwkv_k_a, rwkv_r_k, rwkv_lnx_g, rwkv_lnx_b, ffn_w_gu, ffn_w_down):
    params = (norm_g, mla_w_in, mla_g_q, mla_g_kv, mla_w_uq, mla_w_ukv, mla_w_o,
              rwkv_mu, rwkv_w_r, rwkv_w_k, rwkv_w_v, rwkv_w_o, rwkv_w0, rwkv_w1, rwkv_w2,
              rwkv_a0, rwkv_a1, rwkv_a2, rwkv_g1, rwkv_g2, rwkv_k_k, rwkv_k_a, rwkv_r_k,
              rwkv_lnx_g, rwkv_lnx_b, ffn_w_gu, ffn_w_down)
    return (_trunk(x_prompt, *params), _trunk(x_sample, *params))
```

```python
import functools
import math

import jax
import jax.numpy as jnp
from jax import lax
from jax.experimental import pallas as pl
from jax.experimental.pallas import tpu as pltpu

F32 = jnp.float32
BF16 = jnp.bfloat16

NORM_EPS = 1e-6
LNX_EPS = 64e-5
ROPE_THETA = 10000.0

MLA_HEADS = 16
Q_LORA = 512
KV_LORA = 512
NOPE_DIM = 128
ROPE_DIM = 64
V_DIM = 128
QK_PAD = 256
ONES_ROWS = 16
RWKV_HEAD = 64

LANES = 128
VMEM_LIMIT = 56 * 1024 * 1024

ROWS_WIDE = 512
ROWS_NARROW = 256
ATTN_TQ = 2048
ATTN_TK = 2048
ATTN_GROUP = 256
FFN_TF = 512

WKV_CHUNK = 64
WKV_GROUP = 4
WKV_BLOCK = 1024

NT = (((1,), (1,)), ((), ()))
TN = (((0,), (0,)), ((), ()))


def _params(*sem):
    return pltpu.CompilerParams(dimension_semantics=sem, vmem_limit_bytes=VMEM_LIMIT)


def _rms(x, g):
    return x * lax.rsqrt(jnp.mean(x * x, axis=-1, keepdims=True) + NORM_EPS) * g


def _sigmoid(z):
    return 1.0 / (1.0 + jnp.exp(-z))


def _rope_upper(up, cosw, sinw):
    lane = lax.broadcasted_iota(jnp.int32, up.shape, 1)
    swapped = jnp.where(lane < ROPE_DIM // 2, pltpu.roll(up, LANES - ROPE_DIM // 2, 1),
                        pltpu.roll(up, ROPE_DIM // 2, 1))
    return up * cosw + swapped * sinw


def _head_ones(n):
    r = lax.broadcasted_iota(jnp.int32, (n, n), 0) // RWKV_HEAD
    c = lax.broadcasted_iota(jnp.int32, (n, n), 1) // RWKV_HEAD
    return jnp.where(r == c, 1.0, 0.0).astype(BF16)


def _mla_in_kernel(x_ref, g_ref, w_ref, gq_ref, gkv_ref, cos_ref, sin_ref, cq_ref, ckv_ref, kpe_ref):
    xn = _rms(x_ref[...], g_ref[...]).astype(BF16)
    h = jnp.dot(xn, w_ref[...], preferred_element_type=F32)
    cq_ref[...] = _rms(h[:, :Q_LORA], gq_ref[...]).T.astype(BF16)
    ckv_ref[...] = _rms(h[:, Q_LORA:Q_LORA + KV_LORA], gkv_ref[...]).astype(BF16)
    kpe_ref[...] = _rope_upper(h[:, Q_LORA + KV_LORA:], cos_ref[...], sin_ref[...]).astype(BF16)


def _mla_in(x, g, w_in_pad, g_q, g_kv, cosw, sinw, seq):
    m, d = x.shape
    tm = min(ROWS_WIDE, seq)
    nseq = seq // tm
    n = w_in_pad.shape[1]
    row = lambda i: (i, 0)
    fix = lambda i: (0, 0)
    tab = lambda i: (i % nseq, 0)
    return pl.pallas_call(
        _mla_in_kernel,
        grid=(m // tm,),
        in_specs=[pl.BlockSpec((tm, d), row), pl.BlockSpec((1, d), fix), pl.BlockSpec((d, n), fix),
                  pl.BlockSpec((1, Q_LORA), fix), pl.BlockSpec((1, KV_LORA), fix),
                  pl.BlockSpec((tm, LANES), tab), pl.BlockSpec((tm, LANES), tab)],
        out_specs=[pl.BlockSpec((Q_LORA, tm), lambda i: (0, i)), pl.BlockSpec((tm, KV_LORA), row),
                   pl.BlockSpec((tm, LANES), row)],
        out_shape=[jax.ShapeDtypeStruct((Q_LORA, m), BF16), jax.ShapeDtypeStruct((m, KV_LORA), BF16),
                   jax.ShapeDtypeStruct((m, LANES), BF16)],
        compiler_params=_params("parallel"),
        name="mla_in",
    )(x, g, w_in_pad, g_q, g_kv, cosw, sinw)


def _mla_q_kernel(cqt_ref, wt_ref, cos_ref, sin_ref, q_ref, *, scale):
    res = jnp.dot(wt_ref[...], cqt_ref[...], preferred_element_type=F32)
    cos = cos_ref[...]
    sin = sin_ref[...]
    half = ROPE_DIM // 2
    head = NOPE_DIM + ROPE_DIM
    for h in range(MLA_HEADS):
        src = h * head
        dst = h * QK_PAD
        x1 = res[src + NOPE_DIM:src + NOPE_DIM + half]
        x2 = res[src + NOPE_DIM + half:src + head]
        q_ref[dst:dst + NOPE_DIM, :] = (res[src:src + NOPE_DIM] * scale).astype(BF16)
        q_ref[dst + NOPE_DIM:dst + NOPE_DIM + half, :] = ((x1 * cos - x2 * sin) * scale).astype(BF16)
        q_ref[dst + NOPE_DIM + half:dst + head, :] = ((x1 * sin + x2 * cos) * scale).astype(BF16)
        q_ref[dst + head:dst + QK_PAD, :] = jnp.zeros((QK_PAD - head, res.shape[1]), BF16)


def _mla_q(cqt, w_uq_t, cos_t, sin_t, seq):
    c, m = cqt.shape
    n = w_uq_t.shape[0]
    tm = min(ROWS_NARROW, seq)
    nseq = seq // tm
    half = ROPE_DIM // 2
    scale = (NOPE_DIM + ROPE_DIM) ** -0.5 * math.log2(math.e)
    return pl.pallas_call(
        functools.partial(_mla_q_kernel, scale=scale),
        grid=(m // tm,),
        in_specs=[pl.BlockSpec((c, tm), lambda i: (0, i)), pl.BlockSpec((n, c), lambda i: (0, 0)),
                  pl.BlockSpec((half, tm), lambda i: (0, i % nseq)),
                  pl.BlockSpec((half, tm), lambda i: (0, i % nseq))],
        out_specs=pl.BlockSpec((MLA_HEADS * QK_PAD, tm), lambda i: (0, i)),
        out_shape=jax.ShapeDtypeStruct((MLA_HEADS * QK_PAD, m), BF16),
        compiler_params=_params("parallel"),
        name="mla_q",
    )(cqt, w_uq_t, cos_t, sin_t)


def _mla_kv_kernel(ckv_ref, kpe_ref, w_ref, k_ref, vt_ref):
    res = jnp.dot(ckv_ref[...], w_ref[...], preferred_element_type=F32)
    kpe = kpe_ref[...]
    for h in range(MLA_HEADS):
        k_ref[h, :, :NOPE_DIM] = res[:, 2 * h * NOPE_DIM:(2 * h + 1) * NOPE_DIM].astype(BF16)
        k_ref[h, :, NOPE_DIM:] = kpe
        vt_ref[h * V_DIM:(h + 1) * V_DIM, :] = res[:, (2 * h + 1) * V_DIM:(2 * h + 2) * V_DIM].T.astype(BF16)


def _mla_kv(ckv, kpe, w_ukv):
    m, c = ckv.shape
    n = w_ukv.shape[1]
    tm = min(ROWS_NARROW, m)
    return pl.pallas_call(
        _mla_kv_kernel,
        grid=(m // tm,),
        in_specs=[pl.BlockSpec((tm, c), lambda i: (i, 0)), pl.BlockSpec((tm, LANES), lambda i: (i, 0)),
                  pl.BlockSpec((c, n), lambda i: (0, 0))],
        out_specs=[pl.BlockSpec((MLA_HEADS, tm, QK_PAD), lambda i: (0, i, 0)),
                   pl.BlockSpec((n // 2, tm), lambda i: (0, i))],
        out_shape=[jax.ShapeDtypeStruct((MLA_HEADS, m, QK_PAD), BF16), jax.ShapeDtypeStruct((n // 2, m), BF16)],
        compiler_params=_params("parallel"),
        name="mla_kv",
    )(ckv, kpe, w_ukv)


def _attn_kernel(qt_ref, qtn_ref, k0_ref, k_ref, vt_ref, o_ref, sa_sc, sb_sc, xa_sc, xb_sc, m_sc, acc_sc,
                 *, n_split, n_kv):
    i = pl.program_id(2)
    j = pl.program_id(3)
    width = qt_ref.shape[1] // n_split
    last = n_kv - 1
    carry = n_kv % 2 == 0
    buf_a = (sa_sc, xa_sc)
    buf_b = (sb_sc, xb_sc)

    def score(k, q_ref, dst, cols):
        s = jnp.dot(k, q_ref[:, cols], preferred_element_type=F32)
        dst[0][:, cols] = s
        dst[1][:, cols] = jnp.max(s, axis=0, keepdims=True)

    @pl.when(j == 0)
    def _():
        m_sc[...] = jnp.full(m_sc.shape, -jnp.inf, F32)
        acc_sc[...] = jnp.zeros(acc_sc.shape, F32)

    @pl.when((j == 0) & (i == 0) if carry else (j == 0))
    def _():
        k0 = k0_ref[0]
        for c in range(n_split):
            score(k0, qt_ref, buf_a, slice(c * width, (c + 1) * width))

    def step(cur, nxt, k_next_ref, q_next_ref):
        k = k_next_ref[0]
        vt = vt_ref[...]
        vt1 = jnp.concatenate([vt, jnp.ones((ONES_ROWS, vt.shape[1]), BF16)], axis=0)
        for c in range(n_split):
            cols = slice(c * width, (c + 1) * width)
            if nxt is not None:
                score(k, q_next_ref, nxt, cols)
            m_prev = m_sc[:, cols]
            m_new = jnp.maximum(m_prev, cur[1][:, cols])
            alpha = jnp.exp2(m_prev - m_new)
            pt = jnp.exp2((cur[0][:, cols] - m_new).astype(BF16))
            acc_sc[:, cols] = alpha * acc_sc[:, cols] + jnp.dot(vt1, pt, preferred_element_type=F32)
            m_sc[:, cols] = m_new

    @pl.when((j % 2 == 0) & (j < last))
    def _():
        step(buf_a, buf_b, k_ref, qt_ref)

    @pl.when((j % 2 == 1) & (j < last))
    def _():
        step(buf_b, buf_a, k_ref, qt_ref)

    @pl.when(j == last)
    def _():
        if carry:
            step(buf_b, buf_a, k0_ref, qtn_ref)
        else:
            step(buf_a, None, k_ref, qt_ref)
        o_ref[...] = (acc_sc[:V_DIM, :] / acc_sc[V_DIM:V_DIM + 1, :]).T.astype(o_ref.dtype)


def _attention(qt, k, vt, batch, seq):
    m = k.shape[1]
    tq = min(ATTN_TQ, seq)
    tk = min(ATTN_TK, seq)
    nq = seq // tq
    nk = seq // tk
    return pl.pallas_call(
        functools.partial(_attn_kernel, n_split=max(tq // ATTN_GROUP, 1), n_kv=nk),
        grid=(batch, MLA_HEADS, nq, nk),
        in_specs=[pl.BlockSpec((QK_PAD, tq), lambda b, h, i, j: (h, b * nq + i)),
                  pl.BlockSpec((QK_PAD, tq), lambda b, h, i, j: (h, b * nq + jnp.minimum(i + 1, nq - 1))),
                  pl.BlockSpec((1, tk, QK_PAD), lambda b, h, i, j: (h, b * nk, 0)),
                  pl.BlockSpec((1, tk, QK_PAD), lambda b, h, i, j: (h, b * nk + jnp.minimum(j + 1, nk - 1), 0)),
                  pl.BlockSpec((V_DIM, tk), lambda b, h, i, j: (h, b * nk + j))],
        out_specs=pl.BlockSpec((tq, V_DIM), lambda b, h, i, j: (b * nq + i, h)),
        out_shape=jax.ShapeDtypeStruct((m, MLA_HEADS * V_DIM), BF16),
        scratch_shapes=[pltpu.VMEM((tk, tq), F32), pltpu.VMEM((tk, tq), F32),
                        pltpu.VMEM((1, tq), F32), pltpu.VMEM((1, tq), F32),
                        pltpu.VMEM((1, tq), F32), pltpu.VMEM((V_DIM + ONES_ROWS, tq), F32)],
        compiler_params=_params("parallel", "parallel", "arbitrary", "arbitrary"),
        name="mla_attention",
    )(qt, qt, k, k, vt)


def _proj_res_kernel(a_ref, w_ref, x_ref, g_ref, g_next_ref, o_ref, xn_ref):
    h = jnp.dot(a_ref[...], w_ref[...], preferred_element_type=F32)
    y = x_ref[...] + _rms(h, g_ref[...])
    o_ref[...] = y
    xn_ref[...] = _rms(y, g_next_ref[...]).astype(BF16)


def _proj_res(a, w, x, g, g_next, name):
    m, kdim = a.shape
    n = w.shape[1]
    tm = min(ROWS_WIDE, m)
    row = pl.BlockSpec((tm, n), lambda i: (i, 0))
    par = pl.BlockSpec((1, n), lambda i: (0, 0))
    return pl.pallas_call(
        _proj_res_kernel,
        grid=(m // tm,),
        in_specs=[pl.BlockSpec((tm, kdim), lambda i: (i, 0)), pl.BlockSpec((kdim, n), lambda i: (0, 0)),
                  row, par, par],
        out_specs=[row, row],
        out_shape=[jax.ShapeDtypeStruct((m, n), F32), jax.ShapeDtypeStruct((m, n), BF16)],
        compiler_params=_params("parallel"),
        name=name,
    )(a, w, x, g, g_next)


def _ffn_kernel(x_ref, xn_ref, wg_ref, wu_ref, wd_ref, g_post_ref, o_ref, acc_sc):
    f = pl.program_id(1)

    @pl.when(f == 0)
    def _():
        acc_sc[...] = jnp.zeros(acc_sc.shape, F32)

    xn = xn_ref[...]
    half = wg_ref.shape[1] // 2
    acts = []
    for c in range(2):
        cols = slice(c * half, (c + 1) * half)
        gate = jnp.dot(xn, wg_ref[:, cols], preferred_element_type=F32)
        up = jnp.dot(xn, wu_ref[:, cols], preferred_element_type=F32)
        acts.append((gate * _sigmoid(gate) * up).astype(BF16))
    acc_sc[...] += jnp.dot(jnp.concatenate(acts, axis=1), wd_ref[...], preferred_element_type=F32)

    @pl.when(f == pl.num_programs(1) - 1)
    def _():
        o_ref[...] = x_ref[...] + _rms(acc_sc[...], g_post_ref[...])


def _ffn(x, xn, w_gu, w_down, g_post):
    m, d = x.shape
    d_ff = w_down.shape[0]
    tm = min(ROWS_WIDE, m)
    tf = FFN_TF
    nf = d_ff // tf
    return pl.pallas_call(
        _ffn_kernel,
        grid=(m // tm, nf),
        in_specs=[pl.BlockSpec((tm, d), lambda i, f: (i, 0)), pl.BlockSpec((tm, d), lambda i, f: (i, 0)),
                  pl.BlockSpec((d, tf), lambda i, f: (0, f)), pl.BlockSpec((d, tf), lambda i, f: (0, f + nf)),
                  pl.BlockSpec((tf, d), lambda i, f: (f, 0)), pl.BlockSpec((1, d), lambda i, f: (0, 0))],
        out_specs=pl.BlockSpec((tm, d), lambda i, f: (i, 0)),
        out_shape=jax.ShapeDtypeStruct((m, d), F32),
        scratch_shapes=[pltpu.VMEM((tm, d), F32)],
        compiler_params=_params("parallel", "arbitrary"),
        name="ffn",
    )(x, xn, w_gu, w_gu, w_down, g_post)


def _rwkv_mix_kernel(x_ref, xp_ref, xn_ref, g_ref, mu_ref, wr_ref, wk_ref, wv_ref, w1_ref, a1_ref, g1_ref,
                     r_ref, k_ref, v_ref, tw_ref, ta_ref, sg_ref, *, tiles_per_seq):
    i = pl.program_id(0)
    g = g_ref[...]
    h = _rms(x_ref[...], g)
    tm = h.shape[0]
    first = (i % tiles_per_seq) == 0
    last = (i % tiles_per_seq) == tiles_per_seq - 1
    hp = jnp.where(first, 0.0, _rms(xp_ref[7:8, :], g))
    hn = jnp.where(last, 0.0, _rms(xn_ref[0:1, :], g))
    row = lax.broadcasted_iota(jnp.int32, h.shape, 0)
    h_prev = jnp.where(row == 0, hp, pltpu.roll(h, 1, 0))
    h_next = jnp.where(row == tm - 1, hn, pltpu.roll(h, tm - 1, 0))
    xx = 0.5 * (h_prev + h_next) - h

    def lerp(idx):
        return (h + xx * mu_ref[idx:idx + 1, :]).astype(BF16)

    r_ref[...] = jnp.dot(lerp(0), wr_ref[...], preferred_element_type=F32)
    k_ref[...] = jnp.dot(lerp(2), wk_ref[...], preferred_element_type=F32)
    v_ref[...] = jnp.dot(lerp(3), wv_ref[...], preferred_element_type=F32)
    tw_ref[...] = jnp.tanh(jnp.dot(lerp(1), w1_ref[...], preferred_element_type=F32)).astype(BF16)
    ta_ref[...] = jnp.dot(lerp(4), a1_ref[...], preferred_element_type=F32).astype(BF16)
    sg_ref[...] = _sigmoid(jnp.dot(lerp(5), g1_ref[...], preferred_element_type=F32)).astype(BF16)


def _rwkv_mix(x, g, mu, w_r, w_k, w_v, w1_cat, a1_cat, g1, seq):
    m, d = x.shape
    lr = w1_cat.shape[1]
    tm = min(ROWS_NARROW, seq)
    tps = seq // tm
    nb8 = m // 8
    r8 = tm // 8
    row = lambda i: (i, 0)
    fix = lambda i: (0, 0)
    weight = pl.BlockSpec((d, d), fix, pipeline_mode=pl.Buffered(1))
    weight_lr = pl.BlockSpec((d, lr), fix, pipeline_mode=pl.Buffered(1))
    return pl.pallas_call(
        functools.partial(_rwkv_mix_kernel, tiles_per_seq=tps),
        grid=(m // tm,),
        in_specs=[pl.BlockSpec((tm, d), row),
                  pl.BlockSpec((8, d), lambda i: (jnp.maximum(i * r8 - 1, 0), 0)),
                  pl.BlockSpec((8, d), lambda i: (jnp.minimum((i + 1) * r8, nb8 - 1), 0)),
                  pl.BlockSpec((1, d), fix), pl.BlockSpec((8, d), fix), weight, weight, weight,
                  weight_lr, weight_lr, weight_lr],
        out_specs=[pl.BlockSpec((tm, d), row)] * 3 + [pl.BlockSpec((tm, lr), row)] * 3,
        out_shape=[jax.ShapeDtypeStruct((m, d), F32)] * 3 + [jax.ShapeDtypeStruct((m, lr), BF16)] * 3,
        compiler_params=_params("parallel"),
        name="rwkv_mix",
    )(x, x, x, g, mu, w_r, w_k, w_v, w1_cat, a1_cat, g1)


def _wkv_pre(r_ref, k_ref, v_ref, a_ref, lw_ref, k_k, k_a, ones_bd, blk, head_mask, gl, *, reverse):
    t_len = WKV_CHUNK
    ng = gl // RWKV_HEAD
    gt = ng * t_len
    n_chunks = r_ref.shape[0] // t_len

    wrow = lax.broadcasted_iota(jnp.int32, (t_len, gt), 0)
    wcol = lax.broadcasted_iota(jnp.int32, (t_len, gt), 1) % t_len
    strict = (wcol > wrow) if reverse else (wcol < wrow)
    incl = (wcol >= wrow) if reverse else (wcol <= wrow)
    eye_w = jnp.where(wcol == wrow, 1.0, 0.0)
    trow = lax.broadcasted_iota(jnp.int32, (t_len, t_len), 0)
    tcol = lax.broadcasted_iota(jnp.int32, (t_len, t_len), 1)
    tri = jnp.where((tcol >= trow) if reverse else (tcol <= trow), 1.0, 0.0).astype(BF16)
    bd_mask = jnp.where(lax.broadcasted_iota(jnp.int32, (gt, gt), 0) // t_len
                        == lax.broadcasted_iota(jnp.int32, (gt, gt), 1) // t_len, 1.0, 0.0).astype(BF16)

    def bdiag(w):
        return jnp.concatenate([w.astype(BF16)] * ng, axis=0) * bd_mask

    def mmb(a, b_bf16):
        return jnp.dot(a.astype(BF16), b_bf16, preferred_element_type=F32)

    items = []
    for ci in range(n_chunks):
        c = (n_chunks - 1 - ci) if reverse else ci
        items.append({"rows": slice(c * t_len, (c + 1) * t_len)})

    for it in items:
        lw = lw_ref[it["rows"], :]
        lw_hi = lw.astype(BF16)
        lw_lo = (lw - lw_hi.astype(F32)).astype(BF16)
        it["lw"] = lw
        cum2 = jnp.dot(tri, jnp.concatenate([lw_hi, lw_lo], axis=1), preferred_element_type=F32)
        it["cum"] = cum2[:, :gl] + cum2[:, gl:]
    yield

    kkr_all = k_ref[...] * k_k
    nrm_all = jnp.maximum(jnp.sqrt(jnp.dot((kkr_all * kkr_all).astype(BF16), ones_bd,
                                           preferred_element_type=F32)), 1e-12)
    kk_all = kkr_all / nrm_all

    for it in items:
        rows, cum, lw = it["rows"], it["cum"], it["lw"]
        a = a_ref[rows, :]
        kk = kk_all[rows]
        kd = k_ref[rows, :] * (1.0 + (a - 1.0) * k_a)
        b = kk * a
        tot = cum[0:1] if reverse else cum[t_len - 1:t_len]
        inv_p = jnp.exp(-cum)
        rt = r_ref[rows, :] * jnp.exp(cum)
        kkt = kk * jnp.exp(cum - lw)
        to_end = jnp.exp(tot - cum)
        it["decay"] = jnp.exp(tot)
        it["b_end"] = (b * to_end).astype(BF16)
        it["k_end"] = (kd * to_end).astype(BF16)
        it["rt"] = rt.astype(BF16)
        it["kkt_blk"] = blk(kkt)
        lhs = jnp.concatenate([kkt, rt], axis=0).astype(BF16)
        rhs = jnp.concatenate([blk(b * inv_p), blk(kd * inv_p)], axis=0)
        aa = lax.dot_general(lhs, rhs, NT, preferred_element_type=F32)
        it["a_ab"] = jnp.where(strict, aa[:t_len, :gt], 0.0)
        it["a_ak"] = jnp.where(strict, aa[:t_len, gt:], 0.0).astype(BF16)
        it["a_r"] = jnp.concatenate([jnp.where(incl, aa[t_len:, :gt], 0.0),
                                     jnp.where(incl, aa[t_len:, gt:], 0.0)], axis=1).astype(BF16)
    yield

    for it in items:
        it["x"] = eye_w - it["a_ab"]
        it["p"] = mmb(it["a_ab"], bdiag(it["a_ab"]))
    yield
    rounds = int(math.log2(t_len)) - 1
    for rnd in range(rounds):
        for it in items:
            p_bd = bdiag(it["p"])
            if rnd < rounds - 1:
                xp = mmb(jnp.concatenate([it["x"], it["p"]], axis=0), p_bd)
                it["x"] = it["x"] + xp[:t_len]
                it["p"] = xp[t_len:]
            else:
                it["x"] = it["x"] + mmb(it["x"], p_bd)
        yield

    for it in items:
        v = v_ref[it["rows"], :]
        it["v"] = v
        it["v_blk"] = blk(v)
        it["akv"] = jnp.dot(it["a_ak"], it["v_blk"], preferred_element_type=F32)
        it["x_b"] = it["x"].astype(BF16)
        it["w_k"] = jnp.dot(it["x_b"], it["kkt_blk"], preferred_element_type=F32)
    yield
    for it in items:
        it["u"] = jnp.dot(it["x_b"], blk(it["akv"]), preferred_element_type=F32)
    yield
    for it in items:
        m_full = lax.dot_general(it["w_k"].astype(BF16), it["b_end"], TN, preferred_element_type=F32)
        it["m"] = jnp.where(head_mask, m_full, 0.0).astype(BF16)
    yield
    for it in items:
        c_full = lax.dot_general(jnp.concatenate([-it["u"], it["v"]], axis=0).astype(BF16),
                                 jnp.concatenate([it["b_end"], it["k_end"]], axis=0),
                                 TN, preferred_element_type=F32)
        it["c"] = jnp.where(head_mask, c_full, 0.0)
        it["wr"] = jnp.concatenate([it["w_k"].astype(BF16), it["rt"]], axis=0)
    return items


def _staggered(first, second):
    results = [None, None]
    live = [first, second]
    next(first)
    while any(g is not None for g in live):
        for idx in (1, 0):
            if live[idx] is not None:
                try:
                    next(live[idx])
                except StopIteration as stop:
                    results[idx] = stop.value
                    live[idx] = None
    return results


def _wkv2_kernel(rf_ref, kf_ref, vf_ref, twf_ref, taf_ref, rb_ref, kb_ref, vb_ref, twb_ref, tab_ref,
                 w2_ref, a2_ref, par_ref, yf_ref, yb_ref, bonus_ref, stf_ref, stb_ref):
    @pl.when(pl.program_id(2) == 0)
    def _():
        stf_ref[...] = jnp.zeros(stf_ref.shape, F32)
        stb_ref[...] = jnp.zeros(stb_ref.shape, F32)

    gl = stf_ref.shape[0]
    gt = (gl // RWKV_HEAD) * WKV_CHUNK
    lr = w2_ref.shape[0] // 2
    ones_bd = _head_ones(gl)
    k_k, k_a, r_k = par_ref[0:1, :], par_ref[1:2, :], par_ref[2:3, :]

    def second_stage(t_ref, w_ref, di, bias_row):
        lo = di * lr
        return (jnp.dot(t_ref[:, lo:lo + lr], w_ref[lo:lo + lr, :], preferred_element_type=F32)
                + par_ref[bias_row + di:bias_row + di + 1, :])

    lw_f = -math.exp(-0.5) * _sigmoid(second_stage(twf_ref, w2_ref, 0, 3))
    lw_b = -math.exp(-0.5) * _sigmoid(second_stage(twb_ref, w2_ref, 1, 3))
    a0_f = _sigmoid(second_stage(taf_ref, a2_ref, 0, 5))
    a1_f = _sigmoid(second_stage(taf_ref, a2_ref, 1, 5))
    a1_b = _sigmoid(second_stage(tab_ref, a2_ref, 1, 5))

    kd_sum = kf_ref[...] * (2.0 + (a0_f + a1_f - 2.0) * k_a)
    bonus_ref[...] = jnp.dot((rf_ref[...] * kd_sum * r_k).astype(BF16), ones_bd,
                             preferred_element_type=F32) * vf_ref[...]

    blk_mask = jnp.where(lax.broadcasted_iota(jnp.int32, (gt, gl), 0) // WKV_CHUNK
                         == lax.broadcasted_iota(jnp.int32, (gt, gl), 1) // RWKV_HEAD, 1.0, 0.0).astype(BF16)
    head_mask = (lax.broadcasted_iota(jnp.int32, (gl, gl), 0) // RWKV_HEAD
                 == lax.broadcasted_iota(jnp.int32, (gl, gl), 1) // RWKV_HEAD)

    def blk(x):
        return jnp.concatenate([x.astype(BF16)] * (gl // RWKV_HEAD), axis=0) * blk_mask

    t_len = WKV_CHUNK
    items_f, items_b = _staggered(
        _wkv_pre(rf_ref, kf_ref, vf_ref, a0_f, lw_f, k_k, k_a, ones_bd, blk, head_mask, gl, reverse=False),
        _wkv_pre(rb_ref, kb_ref, vb_ref, a1_b, lw_b, k_k, k_a, ones_bd, blk, head_mask, gl, reverse=True))
    chains = [(items_f, yf_ref, stf_ref), (items_b, yb_ref, stb_ref)]
    states = [st_ref[...] for _, _, st_ref in chains]

    def emit_y(y_ref, it, sa, rs):
        y_ref[it["rows"], :] = rs + jnp.dot(it["a_r"], jnp.concatenate([blk(sa), it["v_blk"]], axis=0),
                                            preferred_element_type=F32)

    pending = []
    for ci in range(len(chains[0][0])):
        st_bf = [st.astype(BF16) for st in states]
        its = [items[ci] for items, _, _ in chains]
        st_m = [jnp.dot(sb, it["m"], preferred_element_type=F32) for sb, it in zip(st_bf, its)]
        ws = [lax.dot_general(it["wr"], sb, NT, preferred_element_type=F32) for sb, it in zip(st_bf, its)]
        for args in pending:
            emit_y(*args)
        pending = [(y_ref, it, -(w[:t_len] + it["u"]), w[t_len:])
                   for (_, y_ref, _), it, w in zip(chains, its, ws)]
        states = [st * it["decay"] - sm + it["c"] for st, it, sm in zip(states, its, st_m)]
    for args in pending:
        emit_y(*args)
    for (_, _, st_ref), st in zip(chains, states):
        st_ref[...] = st


def _wkv2(r, k, v, tw, ta, w2_cat, a2_cat, par, batch, seq):
    m, d = r.shape
    gl = WKV_GROUP * RWKV_HEAD
    lr2 = tw.shape[1]
    tb = min(WKV_BLOCK, seq)
    nb = seq // tb
    fwd = pl.BlockSpec((tb, gl), lambda bi, g, j: (bi * nb + j, g))
    bwd = pl.BlockSpec((tb, gl), lambda bi, g, j: (bi * nb + nb - 1 - j, g))
    fwd_lr = pl.BlockSpec((tb, lr2), lambda bi, g, j: (bi * nb + j, 0))
    bwd_lr = pl.BlockSpec((tb, lr2), lambda bi, g, j: (bi * nb + nb - 1 - j, 0))
    col = lambda rows: pl.BlockSpec((rows, gl), lambda bi, g, j: (0, g))
    out = jax.ShapeDtypeStruct((m, d), F32)
    return pl.pallas_call(
        _wkv2_kernel,
        grid=(batch, d // gl, nb),
        in_specs=[fwd] * 3 + [fwd_lr] * 2 + [bwd] * 3 + [bwd_lr] * 2 + [col(lr2), col(lr2), col(par.shape[0])],
        out_specs=[fwd, bwd, fwd],
        out_shape=[out, out, out],
        scratch_shapes=[pltpu.VMEM((gl, gl), F32), pltpu.VMEM((gl, gl), F32)],
        compiler_params=_params("parallel", "parallel", "arbitrary"),
        name="wkv",
    )(r, k, v, tw, ta, r, k, v, tw, ta, w2_cat, a2_cat, par)


def _rwkv_out_kernel(y0_ref, y1_ref, bonus_ref, sg_ref, g2_ref, lg_ref, lb_ref, wo_ref, x_ref, g_ref, g_next_ref,
                     o_ref, xn_ref):
    width = 2 * LANES
    ones_bd = _head_ones(width)
    d = y0_ref.shape[1]
    inv_n = 1.0 / RWKV_HEAD

    def head_sum(z):
        return jnp.dot(z.astype(BF16), ones_bd, preferred_element_type=F32)

    gate = jnp.dot(sg_ref[...], g2_ref[...], preferred_element_type=F32)
    gated = []
    for c in range(d // width):
        sl = slice(c * width, (c + 1) * width)
        y = y0_ref[:, sl] + y1_ref[:, sl]
        mean = head_sum(y) * inv_n
        yc = y - mean
        var = head_sum(yc * yc) * inv_n
        yn = yc * lax.rsqrt(var + LNX_EPS) * lg_ref[:, sl] + lb_ref[:, sl]
        gated.append(((yn + bonus_ref[:, sl]) * gate[:, sl]).astype(BF16))
    h = jnp.dot(jnp.concatenate(gated, axis=1), wo_ref[...], preferred_element_type=F32)
    y_out = x_ref[...] + _rms(h, g_ref[...])
    o_ref[...] = y_out
    xn_ref[...] = _rms(y_out, g_next_ref[...]).astype(BF16)


def _rwkv_out(y0, y1, bonus, sg, g2, lnx_g, lnx_b, w_o, x, g_post, g_next):
    m, d = y0.shape
    lr = sg.shape[1]
    tm = min(ROWS_NARROW, m)
    row = pl.BlockSpec((tm, d), lambda i: (i, 0))
    par = pl.BlockSpec((1, d), lambda i: (0, 0))
    fix = lambda i: (0, 0)
    return pl.pallas_call(
        _rwkv_out_kernel,
        grid=(m // tm,),
        in_specs=[row] * 3 + [pl.BlockSpec((tm, lr), lambda i: (i, 0)),
                              pl.BlockSpec((lr, d), fix, pipeline_mode=pl.Buffered(1)), par, par,
                              pl.BlockSpec((d, d), fix, pipeline_mode=pl.Buffered(1)), row, par, par],
        out_specs=[row, row],
        out_shape=[jax.ShapeDtypeStruct((m, d), F32), jax.ShapeDtypeStruct((m, d), BF16)],
        compiler_params=_params("parallel"),
        name="rwkv_out",
    )(y0, y1, bonus, sg, g2, lnx_g, lnx_b, w_o, x, g_post, g_next)


def _rope_tables(seq):
    half = ROPE_DIM // 2
    inv = 1.0 / (ROPE_THETA ** (jnp.arange(half, dtype=F32) * (2.0 / ROPE_DIM)))
    ang = jnp.arange(seq, dtype=F32)[:, None] * inv[None, :]
    cos, sin = jnp.cos(ang), jnp.sin(ang)
    zero = jnp.zeros((seq, LANES - ROPE_DIM), F32)
    return (jnp.concatenate([cos, cos, zero], axis=1), jnp.concatenate([-sin, sin, zero], axis=1), cos.T, sin.T)


def _row(v):
    return v.reshape(1, -1).astype(F32)


def _pad_cols(w, n):
    return jnp.pad(w, ((0, 0), (0, n - w.shape[1])))


def _pad_rows(w, n):
    return jnp.pad(w, ((0, n - w.shape[0]), (0, 0)))


def _mla_layer(x, batch, seq, g_pre, g_post, g_next, w_in, g_q, g_kv, w_uq, w_ukv, w_o):
    cosw, sinw, cos_t, sin_t = _rope_tables(seq)
    w_in_pad = _pad_cols(w_in, Q_LORA + KV_LORA + LANES).astype(BF16)
    cqt, ckv, kpe = _mla_in(x, _row(g_pre), w_in_pad, _row(g_q), _row(g_kv), cosw, sinw, seq)
    qt = _mla_q(cqt, w_uq.T.astype(BF16), cos_t, sin_t, seq)
    k, vt = _mla_kv(ckv, kpe, w_ukv.astype(BF16))
    o = _attention(qt, k, vt, batch, seq)
    return _proj_res(o, w_o.astype(BF16), x, _row(g_post), _row(g_next), "mla_out")


def _rwkv_layer(x, batch, seq, g_pre, g_post, g_next, mu, w_r, w_k, w_v, w_o, w0, w1, w2, a0, a1, a2, g1, g2,
                k_k, k_a, r_k, lnx_g, lnx_b):
    mu8 = jnp.pad(mu, ((0, 8 - mu.shape[0]), (0, 0)))
    w1_cat = jnp.concatenate([_pad_cols(w1[di], LANES) for di in range(2)], axis=1).astype(BF16)
    a1_cat = jnp.concatenate([_pad_cols(a1[di], LANES) for di in range(2)], axis=1).astype(BF16)
    w2_cat = jnp.concatenate([_pad_rows(w2[di], LANES) for di in range(2)], axis=0).astype(BF16)
    a2_cat = jnp.concatenate([_pad_rows(a2[di], LANES) for di in range(2)], axis=0).astype(BF16)
    par = jnp.stack([k_k, k_a, r_k, w0[0], w0[1], a0[0], a0[1], jnp.zeros_like(k_k)]).astype(F32)
    r, k, v, tw, ta, sg = _rwkv_mix(x, _row(g_pre), mu8, w_r.astype(BF16), w_k.astype(BF16), w_v.astype(BF16),
                                    w1_cat, a1_cat, g1.astype(BF16), seq)
    y0, y1, bonus = _wkv2(r, k, v, tw, ta, w2_cat, a2_cat, par, batch, seq)
    return _rwkv_out(y0, y1, bonus, sg, g2.astype(BF16), _row(lnx_g), _row(lnx_b), w_o.astype(BF16), x,
                     _row(g_post), _row(g_next))


def _trunk(x3, norm_g, mla_w_in, mla_g_q, mla_g_kv, mla_w_uq, mla_w_ukv, mla_w_o,
           rwkv_mu, rwkv_w_r, rwkv_w_k, rwkv_w_v, rwkv_w_o, rwkv_w0, rwkv_w1, rwkv_w2,
           rwkv_a0, rwkv_a1, rwkv_a2, rwkv_g1, rwkv_g2, rwkv_k_k, rwkv_k_a, rwkv_r_k,
           rwkv_lnx_g, rwkv_lnx_b, ffn_w_gu, ffn_w_down):
    batch, seq, d = x3.shape
    for tile in (ROWS_WIDE, ATTN_TQ, ATTN_TK, WKV_BLOCK):
        assert seq % min(tile, seq) == 0, (seq, tile)
    assert seq % WKV_CHUNK == 0 and d % (WKV_GROUP * RWKV_HEAD) == 0, (seq, d)
    x = x3.reshape(batch * seq, d)
    depth = norm_g.shape[0]
    for i in range(depth):
        j = i // 2
        if i % 2 == 0:
            x, xn = _mla_layer(x, batch, seq, norm_g[i, 0], norm_g[i, 1], norm_g[i, 2], mla_w_in[j], mla_g_q[j],
                               mla_g_kv[j], mla_w_uq[j], mla_w_ukv[j], mla_w_o[j])
        else:
            x, xn = _rwkv_layer(x, batch, seq, norm_g[i, 0], norm_g[i, 1], norm_g[i, 2], rwkv_mu[j], rwkv_w_r[j],
                                rwkv_w_k[j], rwkv_w_v[j], rwkv_w_o[j], rwkv_w0[j], rwkv_w1[j], rwkv_w2[j],
                                rwkv_a0[j], rwkv_a1[j], rwkv_a2[j], rwkv_g1[j], rwkv_g2[j], rwkv_k_k[j],
                                rwkv_k_a[j], rwkv_r_k[j].reshape(-1), rwkv_lnx_g[j], rwkv_lnx_b[j])
        x = _ffn(x, xn, ffn_w_gu[i].astype(BF16), ffn_w_down[i].astype(BF16), _row(norm_g[i, 3]))
    return x.reshape(batch, seq, d)


def kernel(x_prompt, x_sample, norm_g, mla_w_in, mla_g_q, mla_g_kv, mla_w_uq, mla_w_ukv, mla_w_o, rwkv_mu, rwkv_w_r, rwkv_w_k, rwkv_w_v, rwkv_w_o, rwkv_w0, rwkv_w1, rwkv_w2, rwkv_a0, rwkv_a1, rwkv_a2, rwkv_g1, rwkv_g2, rwkv_k_k, rwkv_k_a, rwkv_r_k, rwkv_lnx_g, rwkv_lnx_b, ffn_w_gu, ffn_w_down):
    params = (norm_g, mla_w_in, mla_g_q, mla_g_kv, mla_w_uq, mla_w_ukv, mla_w_o,
              rwkv_mu, rwkv_w_r, rwkv_w_k, rwkv_w_v, rwkv_w_o, rwkv_w0, rwkv_w1, rwkv_w2,
              rwkv_a0, rwkv_a1, rwkv_a2, rwkv_g1, rwkv_g2, rwkv_k_k, rwkv_k_a, rwkv_r_k,
              rwkv_lnx_g, rwkv_lnx_b, ffn_w_gu, ffn_w_down)
    return (_trunk(x_prompt, *params), _trunk(x_sample, *params))
```

```python
import functools
import math

import jax
import jax.numpy as jnp
from jax import lax
from jax.experimental import pallas as pl
from jax.experimental.pallas import tpu as pltpu

F32 = jnp.float32
BF16 = jnp.bfloat16

NORM_EPS = 1e-6
LNX_EPS = 64e-5
ROPE_THETA = 10000.0

MLA_HEADS = 16
Q_LORA = 512
KV_LORA = 512
NOPE_DIM = 128
ROPE_DIM = 64
V_DIM = 128
QK_PAD = 256
ONES_ROWS = 16
RWKV_HEAD = 64

LANES = 128
VMEM_LIMIT = 56 * 1024 * 1024

ROWS_WIDE = 512
ROWS_NARROW = 256
ATTN_TQ = 2048
ATTN_TK = 2048
ATTN_GROUP = 256
FFN_TF = 512

WKV_CHUNK = 64
WKV_GROUP = 4
WKV_BLOCK = 1024

NT = (((1,), (1,)), ((), ()))
TN = (((0,), (0,)), ((), ()))


def _params(*sem):
    return pltpu.CompilerParams(dimension_semantics=sem, vmem_limit_bytes=VMEM_LIMIT)


def _rms(x, g):
    return x * lax.rsqrt(jnp.mean(x * x, axis=-1, keepdims=True) + NORM_EPS) * g


def _sigmoid(z):
    return 1.0 / (1.0 + jnp.exp(-z))


def _rope_upper(up, cosw, sinw):
    lane = lax.broadcasted_iota(jnp.int32, up.shape, 1)
    swapped = jnp.where(lane < ROPE_DIM // 2, pltpu.roll(up, LANES - ROPE_DIM // 2, 1),
                        pltpu.roll(up, ROPE_DIM // 2, 1))
    return up * cosw + swapped * sinw


def _head_ones(n):
    r = lax.broadcasted_iota(jnp.int32, (n, n), 0) // RWKV_HEAD
    c = lax.broadcasted_iota(jnp.int32, (n, n), 1) // RWKV_HEAD
    return jnp.where(r == c, 1.0, 0.0).astype(BF16)


def _mla_in_kernel(x_ref, g_ref, w_ref, gq_ref, gkv_ref, cos_ref, sin_ref, cq_ref, ckv_ref, kpe_ref):
    xn = _rms(x_ref[...], g_ref[...]).astype(BF16)
    h = jnp.dot(xn, w_ref[...], preferred_element_type=F32)
    cq_ref[...] = _rms(h[:, :Q_LORA], gq_ref[...]).T.astype(BF16)
    ckv_ref[...] = _rms(h[:, Q_LORA:Q_LORA + KV_LORA], gkv_ref[...]).astype(BF16)
    kpe_ref[...] = _rope_upper(h[:, Q_LORA + KV_LORA:], cos_ref[...], sin_ref[...]).astype(BF16)


def _mla_in(x, g, w_in_pad, g_q, g_kv, cosw, sinw, seq):
    m, d = x.shape
    tm = min(ROWS_WIDE, seq)
    nseq = seq // tm
    n = w_in_pad.shape[1]
    row = lambda i: (i, 0)
    fix = lambda i: (0, 0)
    tab = lambda i: (i % nseq, 0)
    return pl.pallas_call(
        _mla_in_kernel,
        grid=(m // tm,),
        in_specs=[pl.BlockSpec((tm, d), row), pl.BlockSpec((1, d), fix), pl.BlockSpec((d, n), fix),
                  pl.BlockSpec((1, Q_LORA), fix), pl.BlockSpec((1, KV_LORA), fix),
                  pl.BlockSpec((tm, LANES), tab), pl.BlockSpec((tm, LANES), tab)],
        out_specs=[pl.BlockSpec((Q_LORA, tm), lambda i: (0, i)), pl.BlockSpec((tm, KV_LORA), row),
                   pl.BlockSpec((tm, LANES), row)],
        out_shape=[jax.ShapeDtypeStruct((Q_LORA, m), BF16), jax.ShapeDtypeStruct((m, KV_LORA), BF16),
                   jax.ShapeDtypeStruct((m, LANES), BF16)],
        compiler_params=_params("parallel"),
        name="mla_in",
    )(x, g, w_in_pad, g_q, g_kv, cosw, sinw)


def _mla_q_kernel(cqt_ref, wt_ref, cos_ref, sin_ref, q_ref, *, scale):
    res = jnp.dot(wt_ref[...], cqt_ref[...], preferred_element_type=F32)
    cos = cos_ref[...]
    sin = sin_ref[...]
    half = ROPE_DIM // 2
    head = NOPE_DIM + ROPE_DIM
    for h in range(MLA_HEADS):
        src = h * head
        dst = h * QK_PAD
        x1 = res[src + NOPE_DIM:src + NOPE_DIM + half]
        x2 = res[src + NOPE_DIM + half:src + head]
        q_ref[dst:dst + NOPE_DIM, :] = (res[src:src + NOPE_DIM] * scale).astype(BF16)
        q_ref[dst + NOPE_DIM:dst + NOPE_DIM + half, :] = ((x1 * cos - x2 * sin) * scale).astype(BF16)
        q_ref[dst + NOPE_DIM + half:dst + head, :] = ((x1 * sin + x2 * cos) * scale).astype(BF16)
        q_ref[dst + head:dst + QK_PAD, :] = jnp.zeros((QK_PAD - head, res.shape[1]), BF16)


def _mla_q(cqt, w_uq_t, cos_t, sin_t, seq):
    c, m = cqt.shape
    n = w_uq_t.shape[0]
    tm = min(ROWS_NARROW, seq)
    nseq = seq // tm
    half = ROPE_DIM // 2
    scale = (NOPE_DIM + ROPE_DIM) ** -0.5 * math.log2(math.e)
    return pl.pallas_call(
        functools.partial(_mla_q_kernel, scale=scale),
        grid=(m // tm,),
        in_specs=[pl.BlockSpec((c, tm), lambda i: (0, i)), pl.BlockSpec((n, c), lambda i: (0, 0)),
                  pl.BlockSpec((half, tm), lambda i: (0, i % nseq)),
                  pl.BlockSpec((half, tm), lambda i: (0, i % nseq))],
        out_specs=pl.BlockSpec((MLA_HEADS * QK_PAD, tm), lambda i: (0, i)),
        out_shape=jax.ShapeDtypeStruct((MLA_HEADS * QK_PAD, m), BF16),
        compiler_params=_params("parallel"),
        name="mla_q",
    )(cqt, w_uq_t, cos_t, sin_t)


def _mla_kv_kernel(ckv_ref, kpe_ref, w_ref, k_ref, vt_ref):
    res = jnp.dot(ckv_ref[...], w_ref[...], preferred_element_type=F32)
    kpe = kpe_ref[...]
    for h in range(MLA_HEADS):
        k_ref[h, :, :NOPE_DIM] = res[:, 2 * h * NOPE_DIM:(2 * h + 1) * NOPE_DIM].astype(BF16)
        k_ref[h, :, NOPE_DIM:] = kpe
        vt_ref[h * V_DIM:(h + 1) * V_DIM, :] = res[:, (2 * h + 1) * V_DIM:(2 * h + 2) * V_DIM].T.astype(BF16)


def _mla_kv(ckv, kpe, w_ukv):
    m, c = ckv.shape
    n = w_ukv.shape[1]
    tm = min(ROWS_NARROW, m)
    return pl.pallas_call(
        _mla_kv_kernel,
        grid=(m // tm,),
        in_specs=[pl.BlockSpec((tm, c), lambda i: (i, 0)), pl.BlockSpec((tm, LANES), lambda i: (i, 0)),
                  pl.BlockSpec((c, n), lambda i: (0, 0))],
        out_specs=[pl.BlockSpec((MLA_HEADS, tm, QK_PAD), lambda i: (0, i, 0)),
                   pl.BlockSpec((n // 2, tm), lambda i: (0, i))],
        out_shape=[jax.ShapeDtypeStruct((MLA_HEADS, m, QK_PAD), BF16), jax.ShapeDtypeStruct((n // 2, m), BF16)],
        compiler_params=_params("parallel"),
        name="mla_kv",
    )(ckv, kpe, w_ukv)


def _attn_kernel(qt_ref, qtn_ref, k0_ref, k_ref, vt_ref, o_ref, sa_sc, sb_sc, xa_sc, xb_sc, m_sc, acc_sc,
                 *, n_split, n_kv):
    i = pl.program_id(2)
    j = pl.program_id(3)
    width = qt_ref.shape[1] // n_split
    last = n_kv - 1
    carry = n_kv % 2 == 0
    buf_a = (sa_sc, xa_sc)
    buf_b = (sb_sc, xb_sc)

    def score(k, q_ref, dst, cols):
        s = jnp.dot(k, q_ref[:, cols], preferred_element_type=F32)
        dst[0][:, cols] = s
        dst[1][:, cols] = jnp.max(s, axis=0, keepdims=True)

    @pl.when(j == 0)
    def _():
        m_sc[...] = jnp.full(m_sc.shape, -jnp.inf, F32)
        acc_sc[...] = jnp.zeros(acc_sc.shape, F32)

    @pl.when((j == 0) & (i == 0) if carry else (j == 0))
    def _():
        k0 = k0_ref[0]
        for c in range(n_split):
            score(k0, qt_ref, buf_a, slice(c * width, (c + 1) * width))

    def step(cur, nxt, k_next_ref, q_next_ref):
        k = k_next_ref[0]
        vt = vt_ref[...]
        vt1 = jnp.concatenate([vt, jnp.ones((ONES_ROWS, vt.shape[1]), BF16)], axis=0)
        for c in range(n_split):
            cols = slice(c * width, (c + 1) * width)
            if nxt is not None:
                score(k, q_next_ref, nxt, cols)
            m_prev = m_sc[:, cols]
            m_new = jnp.maximum(m_prev, cur[1][:, cols])
            alpha = jnp.exp2(m_prev - m_new)
            pt = jnp.exp2((cur[0][:, cols] - m_new).astype(BF16))
            acc_sc[:, cols] = alpha * acc_sc[:, cols] + jnp.dot(vt1, pt, preferred_element_type=F32)
            m_sc[:, cols] = m_new

    @pl.when((j % 2 == 0) & (j < last))
    def _():
        step(buf_a, buf_b, k_ref, qt_ref)

    @pl.when((j % 2 == 1) & (j < last))
    def _():
        step(buf_b, buf_a, k_ref, qt_ref)

    @pl.when(j == last)
    def _():
        if carry:
            step(buf_b, buf_a, k0_ref, qtn_ref)
        else:
            step(buf_a, None, k_ref, qt_ref)
        o_ref[...] = (acc_sc[:V_DIM, :] / acc_sc[V_DIM:V_DIM + 1, :]).T.astype(o_ref.dtype)


def _attention(qt, k, vt, batch, seq):
    m = k.shape[1]
    tq = min(ATTN_TQ, seq)
    tk = min(ATTN_TK, seq)
    nq = seq // tq
    nk = seq // tk
    return pl.pallas_call(
        functools.partial(_attn_kernel, n_split=max(tq // ATTN_GROUP, 1), n_kv=nk),
        grid=(batch, MLA_HEADS, nq, nk),
        in_specs=[pl.BlockSpec((QK_PAD, tq), lambda b, h, i, j: (h, b * nq + i)),
                  pl.BlockSpec((QK_PAD, tq), lambda b, h, i, j: (h, b * nq + jnp.minimum(i + 1, nq - 1))),
                  pl.BlockSpec((1, tk, QK_PAD), lambda b, h, i, j: (h, b * nk, 0)),
                  pl.BlockSpec((1, tk, QK_PAD), lambda b, h, i, j: (h, b * nk + jnp.minimum(j + 1, nk - 1), 0)),
                  pl.BlockSpec((V_DIM, tk), lambda b, h, i, j: (h, b * nk + j))],
        out_specs=pl.BlockSpec((tq, V_DIM), lambda b, h, i, j: (b * nq + i, h)),
        out_shape=jax.ShapeDtypeStruct((m, MLA_HEADS * V_DIM), BF16),
        scratch_shapes=[pltpu.VMEM((tk, tq), F32), pltpu.VMEM((tk, tq), F32),
                        pltpu.VMEM((1, tq), F32), pltpu.VMEM((1, tq), F32),
                        pltpu.VMEM((1, tq), F32), pltpu.VMEM((V_DIM + ONES_ROWS, tq), F32)],
        compiler_params=_params("parallel", "parallel", "arbitrary", "arbitrary"),
        name="mla_attention",
    )(qt, qt, k, k, vt)


def _proj_res_kernel(a_ref, w_ref, x_ref, g_ref, o_ref):
    h = jnp.dot(a_ref[...], w_ref[...], preferred_element_type=F32)
    o_ref[...] = x_ref[...] + _rms(h, g_ref[...])


def _proj_res(a, w, x, g, name):
    m, kdim = a.shape
    n = w.shape[1]
    tm = min(ROWS_WIDE, m)
    return pl.pallas_call(
        _proj_res_kernel,
        grid=(m // tm,),
        in_specs=[pl.BlockSpec((tm, kdim), lambda i: (i, 0)), pl.BlockSpec((kdim, n), lambda i: (0, 0)),
                  pl.BlockSpec((tm, n), lambda i: (i, 0)), pl.BlockSpec((1, n), lambda i: (0, 0))],
        out_specs=pl.BlockSpec((tm, n), lambda i: (i, 0)),
        out_shape=jax.ShapeDtypeStruct((m, n), F32),
        compiler_params=_params("parallel"),
        name=name,
    )(a, w, x, g)


def _ffn_kernel(x_ref, g_pre_ref, wg_ref, wu_ref, wd_ref, g_post_ref, o_ref, xn_sc, acc_sc):
    f = pl.program_id(1)

    @pl.when(f == 0)
    def _():
        xn_sc[...] = _rms(x_ref[...], g_pre_ref[...]).astype(BF16)
        acc_sc[...] = jnp.zeros(acc_sc.shape, F32)

    xn = xn_sc[...]
    half = wg_ref.shape[2] // 2
    acts = []
    for c in range(2):
        cols = slice(c * half, (c + 1) * half)
        gate = jnp.dot(xn, wg_ref[0, :, cols], preferred_element_type=F32)
        up = jnp.dot(xn, wu_ref[0, :, cols], preferred_element_type=F32)
        acts.append((gate * _sigmoid(gate) * up).astype(BF16))
    acc_sc[...] += jnp.dot(jnp.concatenate(acts, axis=1), wd_ref[...], preferred_element_type=F32)

    @pl.when(f == pl.num_programs(1) - 1)
    def _():
        o_ref[...] = x_ref[...] + _rms(acc_sc[...], g_post_ref[...])


def _ffn(x, g_pre, w_gu_blocks, w_down, g_post):
    m, d = x.shape
    d_ff = w_down.shape[0]
    tm = min(ROWS_WIDE, m)
    tf = w_gu_blocks.shape[2]
    nf = d_ff // tf
    return pl.pallas_call(
        _ffn_kernel,
        grid=(m // tm, nf),
        in_specs=[pl.BlockSpec((tm, d), lambda i, f: (i, 0)), pl.BlockSpec((1, d), lambda i, f: (0, 0)),
                  pl.BlockSpec((1, d, tf), lambda i, f: (f, 0, 0)),
                  pl.BlockSpec((1, d, tf), lambda i, f: (f + nf, 0, 0)),
                  pl.BlockSpec((tf, d), lambda i, f: (f, 0)), pl.BlockSpec((1, d), lambda i, f: (0, 0))],
        out_specs=pl.BlockSpec((tm, d), lambda i, f: (i, 0)),
        out_shape=jax.ShapeDtypeStruct((m, d), F32),
        scratch_shapes=[pltpu.VMEM((tm, d), BF16), pltpu.VMEM((tm, d), F32)],
        compiler_params=_params("parallel", "arbitrary"),
        name="ffn",
    )(x, g_pre, w_gu_blocks, w_gu_blocks, w_down, g_post)


def _rwkv_mix_kernel(x_ref, xp_ref, xn_ref, g_ref, mu_ref, wr_ref, wk_ref, wv_ref, w1_ref, a1_ref, g1_ref,
                     r_ref, k_ref, v_ref, tw_ref, ta_ref, sg_ref, *, tiles_per_seq):
    i = pl.program_id(0)
    g = g_ref[...]
    h = _rms(x_ref[...], g)
    tm = h.shape[0]
    first = (i % tiles_per_seq) == 0
    last = (i % tiles_per_seq) == tiles_per_seq - 1
    hp = jnp.where(first, 0.0, _rms(xp_ref[7:8, :], g))
    hn = jnp.where(last, 0.0, _rms(xn_ref[0:1, :], g))
    row = lax.broadcasted_iota(jnp.int32, h.shape, 0)
    h_prev = jnp.where(row == 0, hp, pltpu.roll(h, 1, 0))
    h_next = jnp.where(row == tm - 1, hn, pltpu.roll(h, tm - 1, 0))
    xx = 0.5 * (h_prev + h_next) - h

    def lerp(idx):
        return (h + xx * mu_ref[idx:idx + 1, :]).astype(BF16)

    r_ref[...] = jnp.dot(lerp(0), wr_ref[...], preferred_element_type=F32)
    k_ref[...] = jnp.dot(lerp(2), wk_ref[...], preferred_element_type=F32)
    v_ref[...] = jnp.dot(lerp(3), wv_ref[...], preferred_element_type=F32)
    tw_ref[...] = jnp.tanh(jnp.dot(lerp(1), w1_ref[...], preferred_element_type=F32)).astype(BF16)
    ta_ref[...] = jnp.dot(lerp(4), a1_ref[...], preferred_element_type=F32).astype(BF16)
    sg_ref[...] = _sigmoid(jnp.dot(lerp(5), g1_ref[...], preferred_element_type=F32)).astype(BF16)


def _rwkv_mix(x, g, mu, w_r, w_k, w_v, w1_cat, a1_cat, g1, seq):
    m, d = x.shape
    lr = w1_cat.shape[1]
    tm = min(ROWS_NARROW, seq)
    tps = seq // tm
    nb8 = m // 8
    r8 = tm // 8
    row = lambda i: (i, 0)
    fix = lambda i: (0, 0)
    weight = pl.BlockSpec((d, d), fix, pipeline_mode=pl.Buffered(1))
    weight_lr = pl.BlockSpec((d, lr), fix, pipeline_mode=pl.Buffered(1))
    return pl.pallas_call(
        functools.partial(_rwkv_mix_kernel, tiles_per_seq=tps),
        grid=(m // tm,),
        in_specs=[pl.BlockSpec((tm, d), row),
                  pl.BlockSpec((8, d), lambda i: (jnp.maximum(i * r8 - 1, 0), 0)),
                  pl.BlockSpec((8, d), lambda i: (jnp.minimum((i + 1) * r8, nb8 - 1), 0)),
                  pl.BlockSpec((1, d), fix), pl.BlockSpec((8, d), fix), weight, weight, weight,
                  weight_lr, weight_lr, weight_lr],
        out_specs=[pl.BlockSpec((tm, d), row)] * 3 + [pl.BlockSpec((tm, lr), row)] * 3,
        out_shape=[jax.ShapeDtypeStruct((m, d), F32)] * 3 + [jax.ShapeDtypeStruct((m, lr), BF16)] * 3,
        compiler_params=_params("parallel"),
        name="rwkv_mix",
    )(x, x, x, g, mu, w_r, w_k, w_v, w1_cat, a1_cat, g1)


def _wkv_pre(r_ref, k_ref, v_ref, a_ref, lw_ref, k_k, k_a, ones_bd, blk, head_mask, gl, *, reverse):
    t_len = WKV_CHUNK
    ng = gl // RWKV_HEAD
    gt = ng * t_len
    n_chunks = r_ref.shape[0] // t_len

    wrow = lax.broadcasted_iota(jnp.int32, (t_len, gt), 0)
    wcol = lax.broadcasted_iota(jnp.int32, (t_len, gt), 1) % t_len
    strict = (wcol > wrow) if reverse else (wcol < wrow)
    incl = (wcol >= wrow) if reverse else (wcol <= wrow)
    eye_w = jnp.where(wcol == wrow, 1.0, 0.0)
    trow = lax.broadcasted_iota(jnp.int32, (t_len, t_len), 0)
    tcol = lax.broadcasted_iota(jnp.int32, (t_len, t_len), 1)
    tri = jnp.where((tcol >= trow) if reverse else (tcol <= trow), 1.0, 0.0).astype(BF16)
    bd_mask = jnp.where(lax.broadcasted_iota(jnp.int32, (gt, gt), 0) // t_len
                        == lax.broadcasted_iota(jnp.int32, (gt, gt), 1) // t_len, 1.0, 0.0).astype(BF16)

    def bdiag(w):
        return jnp.concatenate([w.astype(BF16)] * ng, axis=0) * bd_mask

    def mmb(a, b_bf16):
        return jnp.dot(a.astype(BF16), b_bf16, preferred_element_type=F32)

    items = []
    for ci in range(n_chunks):
        c = (n_chunks - 1 - ci) if reverse else ci
        items.append({"rows": slice(c * t_len, (c + 1) * t_len)})

    for it in items:
        lw = lw_ref[it["rows"], :]
        lw_hi = lw.astype(BF16)
        lw_lo = (lw - lw_hi.astype(F32)).astype(BF16)
        it["lw"] = lw
        cum2 = jnp.dot(tri, jnp.concatenate([lw_hi, lw_lo], axis=1), preferred_element_type=F32)
        it["cum"] = cum2[:, :gl] + cum2[:, gl:]
    yield

    kkr_all = k_ref[...] * k_k
    nrm_all = jnp.maximum(jnp.sqrt(jnp.dot((kkr_all * kkr_all).astype(BF16), ones_bd,
                                           preferred_element_type=F32)), 1e-12)
    kk_all = kkr_all / nrm_all

    for it in items:
        rows, cum, lw = it["rows"], it["cum"], it["lw"]
        a = a_ref[rows, :]
        kk = kk_all[rows]
        kd = k_ref[rows, :] * (1.0 + (a - 1.0) * k_a)
        b = kk * a
        tot = cum[0:1] if reverse else cum[t_len - 1:t_len]
        inv_p = jnp.exp(-cum)
        rt = r_ref[rows, :] * jnp.exp(cum)
        kkt = kk * jnp.exp(cum - lw)
        to_end = jnp.exp(tot - cum)
        it["decay"] = jnp.exp(tot)
        it["b_end"] = (b * to_end).astype(BF16)
        it["k_end"] = (kd * to_end).astype(BF16)
        it["rt"] = rt.astype(BF16)
        it["kkt_blk"] = blk(kkt)
        lhs = jnp.concatenate([kkt, rt], axis=0).astype(BF16)
        rhs = jnp.concatenate([blk(b * inv_p), blk(kd * inv_p)], axis=0)
        aa = lax.dot_general(lhs, rhs, NT, preferred_element_type=F32)
        it["a_ab"] = jnp.where(strict, aa[:t_len, :gt], 0.0)
        it["a_ak"] = jnp.where(strict, aa[:t_len, gt:], 0.0).astype(BF16)
        it["a_r"] = jnp.concatenate([jnp.where(incl, aa[t_len:, :gt], 0.0),
                                     jnp.where(incl, aa[t_len:, gt:], 0.0)], axis=1).astype(BF16)
    yield

    for it in items:
        it["x"] = eye_w - it["a_ab"]
        it["p"] = mmb(it["a_ab"], bdiag(it["a_ab"]))
    yield
    rounds = int(math.log2(t_len)) - 1
    for rnd in range(rounds):
        for it in items:
            p_bd = bdiag(it["p"])
            if rnd < rounds - 1:
                xp = mmb(jnp.concatenate([it["x"], it["p"]], axis=0), p_bd)
                it["x"] = it["x"] + xp[:t_len]
                it["p"] = xp[t_len:]
            else:
                it["x"] = it["x"] + mmb(it["x"], p_bd)
        yield

    for it in items:
        v = v_ref[it["rows"], :]
        it["v"] = v
        it["v_blk"] = blk(v)
        it["akv"] = jnp.dot(it["a_ak"], it["v_blk"], preferred_element_type=F32)
        it["x_b"] = it["x"].astype(BF16)
        it["w_k"] = jnp.dot(it["x_b"], it["kkt_blk"], preferred_element_type=F32)
    yield
    for it in items:
        it["u"] = jnp.dot(it["x_b"], blk(it["akv"]), preferred_element_type=F32)
    yield
    for it in items:
        m_full = lax.dot_general(it["w_k"].astype(BF16), it["b_end"], TN, preferred_element_type=F32)
        it["m"] = jnp.where(head_mask, m_full, 0.0).astype(BF16)
    yield
    for it in items:
        c_full = lax.dot_general(jnp.concatenate([-it["u"], it["v"]], axis=0).astype(BF16),
                                 jnp.concatenate([it["b_end"], it["k_end"]], axis=0),
                                 TN, preferred_element_type=F32)
        it["c"] = jnp.where(head_mask, c_full, 0.0)
        it["wr"] = jnp.concatenate([it["w_k"].astype(BF16), it["rt"]], axis=0)
    return items


def _staggered(first, second):
    results = [None, None]
    live = [first, second]
    next(first)
    while any(g is not None for g in live):
        for idx in (1, 0):
            if live[idx] is not None:
                try:
                    next(live[idx])
                except StopIteration as stop:
                    results[idx] = stop.value
                    live[idx] = None
    return results


def _wkv2_kernel(rf_ref, kf_ref, vf_ref, twf_ref, taf_ref, rb_ref, kb_ref, vb_ref, twb_ref, tab_ref,
                 w2_ref, a2_ref, par_ref, yf_ref, yb_ref, bonus_ref, stf_ref, stb_ref):
    @pl.when(pl.program_id(2) == 0)
    def _():
        stf_ref[...] = jnp.zeros(stf_ref.shape, F32)
        stb_ref[...] = jnp.zeros(stb_ref.shape, F32)

    gl = stf_ref.shape[0]
    gt = (gl // RWKV_HEAD) * WKV_CHUNK
    lr = w2_ref.shape[0] // 2
    ones_bd = _head_ones(gl)
    k_k, k_a, r_k = par_ref[0:1, :], par_ref[1:2, :], par_ref[2:3, :]

    def second_stage(t_ref, w_ref, di, bias_row):
        lo = di * lr
        return (jnp.dot(t_ref[:, lo:lo + lr], w_ref[lo:lo + lr, :], preferred_element_type=F32)
                + par_ref[bias_row + di:bias_row + di + 1, :])

    lw_f = -math.exp(-0.5) * _sigmoid(second_stage(twf_ref, w2_ref, 0, 3))
    lw_b = -math.exp(-0.5) * _sigmoid(second_stage(twb_ref, w2_ref, 1, 3))
    a0_f = _sigmoid(second_stage(taf_ref, a2_ref, 0, 5))
    a1_f = _sigmoid(second_stage(taf_ref, a2_ref, 1, 5))
    a1_b = _sigmoid(second_stage(tab_ref, a2_ref, 1, 5))

    kd_sum = kf_ref[...] * (2.0 + (a0_f + a1_f - 2.0) * k_a)
    bonus_ref[...] = jnp.dot((rf_ref[...] * kd_sum * r_k).astype(BF16), ones_bd,
                             preferred_element_type=F32) * vf_ref[...]

    blk_mask = jnp.where(lax.broadcasted_iota(jnp.int32, (gt, gl), 0) // WKV_CHUNK
                         == lax.broadcasted_iota(jnp.int32, (gt, gl), 1) // RWKV_HEAD, 1.0, 0.0).astype(BF16)
    head_mask = (lax.broadcasted_iota(jnp.int32, (gl, gl), 0) // RWKV_HEAD
                 == lax.broadcasted_iota(jnp.int32, (gl, gl), 1) // RWKV_HEAD)

    def blk(x):
        return jnp.concatenate([x.astype(BF16)] * (gl // RWKV_HEAD), axis=0) * blk_mask

    t_len = WKV_CHUNK
    items_f, items_b = _staggered(
        _wkv_pre(rf_ref, kf_ref, vf_ref, a0_f, lw_f, k_k, k_a, ones_bd, blk, head_mask, gl, reverse=False),
        _wkv_pre(rb_ref, kb_ref, vb_ref, a1_b, lw_b, k_k, k_a, ones_bd, blk, head_mask, gl, reverse=True))
    chains = [(items_f, yf_ref, stf_ref), (items_b, yb_ref, stb_ref)]
    states = [st_ref[...] for _, _, st_ref in chains]

    def emit_y(y_ref, it, sa, rs):
        y_ref[it["rows"], :] = rs + jnp.dot(it["a_r"], jnp.concatenate([blk(sa), it["v_blk"]], axis=0),
                                            preferred_element_type=F32)

    pending = []
    for ci in range(len(chains[0][0])):
        st_bf = [st.astype(BF16) for st in states]
        its = [items[ci] for items, _, _ in chains]
        st_m = [jnp.dot(sb, it["m"], preferred_element_type=F32) for sb, it in zip(st_bf, its)]
        ws = [lax.dot_general(it["wr"], sb, NT, preferred_element_type=F32) for sb, it in zip(st_bf, its)]
        for args in pending:
            emit_y(*args)
        pending = [(y_ref, it, -(w[:t_len] + it["u"]), w[t_len:])
                   for (_, y_ref, _), it, w in zip(chains, its, ws)]
        states = [st * it["decay"] - sm + it["c"] for st, it, sm in zip(states, its, st_m)]
    for args in pending:
        emit_y(*args)
    for (_, _, st_ref), st in zip(chains, states):
        st_ref[...] = st


def _wkv2(r, k, v, tw, ta, w2_cat, a2_cat, par, batch, seq):
    m, d = r.shape
    gl = WKV_GROUP * RWKV_HEAD
    lr2 = tw.shape[1]
    tb = min(WKV_BLOCK, seq)
    nb = seq // tb
    fwd = pl.BlockSpec((tb, gl), lambda bi, g, j: (bi * nb + j, g))
    bwd = pl.BlockSpec((tb, gl), lambda bi, g, j: (bi * nb + nb - 1 - j, g))
    fwd_lr = pl.BlockSpec((tb, lr2), lambda bi, g, j: (bi * nb + j, 0))
    bwd_lr = pl.BlockSpec((tb, lr2), lambda bi, g, j: (bi * nb + nb - 1 - j, 0))
    col = lambda rows: pl.BlockSpec((rows, gl), lambda bi, g, j: (0, g))
    out = jax.ShapeDtypeStruct((m, d), F32)
    return pl.pallas_call(
        _wkv2_kernel,
        grid=(batch, d // gl, nb),
        in_specs=[fwd] * 3 + [fwd_lr] * 2 + [bwd] * 3 + [bwd_lr] * 2 + [col(lr2), col(lr2), col(par.shape[0])],
        out_specs=[fwd, bwd, fwd],
        out_shape=[out, out, out],
        scratch_shapes=[pltpu.VMEM((gl, gl), F32), pltpu.VMEM((gl, gl), F32)],
        compiler_params=_params("parallel", "parallel", "arbitrary"),
        name="wkv",
    )(r, k, v, tw, ta, r, k, v, tw, ta, w2_cat, a2_cat, par)


def _rwkv_out_kernel(y0_ref, y1_ref, bonus_ref, sg_ref, g2_ref, lg_ref, lb_ref, wo_ref, x_ref, g_ref, o_ref):
    width = 2 * LANES
    ones_bd = _head_ones(width)
    d = y0_ref.shape[1]
    inv_n = 1.0 / RWKV_HEAD

    def head_sum(z):
        return jnp.dot(z.astype(BF16), ones_bd, preferred_element_type=F32)

    gate = jnp.dot(sg_ref[...], g2_ref[...], preferred_element_type=F32)
    gated = []
    for c in range(d // width):
        sl = slice(c * width, (c + 1) * width)
        y = y0_ref[:, sl] + y1_ref[:, sl]
        mean = head_sum(y) * inv_n
        yc = y - mean
        var = head_sum(yc * yc) * inv_n
        yn = yc * lax.rsqrt(var + LNX_EPS) * lg_ref[:, sl] + lb_ref[:, sl]
        gated.append(((yn + bonus_ref[:, sl]) * gate[:, sl]).astype(BF16))
    h = jnp.dot(jnp.concatenate(gated, axis=1), wo_ref[...], preferred_element_type=F32)
    o_ref[...] = x_ref[...] + _rms(h, g_ref[...])


def _rwkv_out(y0, y1, bonus, sg, g2, lnx_g, lnx_b, w_o, x, g_post):
    m, d = y0.shape
    lr = sg.shape[1]
    tm = min(ROWS_NARROW, m)
    row = pl.BlockSpec((tm, d), lambda i: (i, 0))
    par = pl.BlockSpec((1, d), lambda i: (0, 0))
    fix = lambda i: (0, 0)
    return pl.pallas_call(
        _rwkv_out_kernel,
        grid=(m // tm,),
        in_specs=[row] * 3 + [pl.BlockSpec((tm, lr), lambda i: (i, 0)),
                              pl.BlockSpec((lr, d), fix, pipeline_mode=pl.Buffered(1)), par, par,
                              pl.BlockSpec((d, d), fix, pipeline_mode=pl.Buffered(1)), row, par],
        out_specs=row,
        out_shape=jax.ShapeDtypeStruct((m, d), F32),
        compiler_params=_params("parallel"),
        name="rwkv_out",
    )(y0, y1, bonus, sg, g2, lnx_g, lnx_b, w_o, x, g_post)


def _rope_tables(seq):
    half = ROPE_DIM // 2
    inv = 1.0 / (ROPE_THETA ** (jnp.arange(half, dtype=F32) * (2.0 / ROPE_DIM)))
    ang = jnp.arange(seq, dtype=F32)[:, None] * inv[None, :]
    cos, sin = jnp.cos(ang), jnp.sin(ang)
    zero = jnp.zeros((seq, LANES - ROPE_DIM), F32)
    return (jnp.concatenate([cos, cos, zero], axis=1), jnp.concatenate([-sin, sin, zero], axis=1), cos.T, sin.T)


def _row(v):
    return v.reshape(1, -1).astype(F32)


def _pad_cols(w, n):
    return jnp.pad(w, ((0, 0), (0, n - w.shape[1])))


def _pad_rows(w, n):
    return jnp.pad(w, ((0, n - w.shape[0]), (0, 0)))


def _mla_layer(x, batch, seq, g_pre, g_post, w_in, g_q, g_kv, w_uq, w_ukv, w_o):
    cosw, sinw, cos_t, sin_t = _rope_tables(seq)
    w_in_pad = _pad_cols(w_in, Q_LORA + KV_LORA + LANES).astype(BF16)
    cqt, ckv, kpe = _mla_in(x, _row(g_pre), w_in_pad, _row(g_q), _row(g_kv), cosw, sinw, seq)
    qt = _mla_q(cqt, w_uq.T.astype(BF16), cos_t, sin_t, seq)
    k, vt = _mla_kv(ckv, kpe, w_ukv.astype(BF16))
    o = _attention(qt, k, vt, batch, seq)
    return _proj_res(o, w_o.astype(BF16), x, _row(g_post), "mla_out")


def _rwkv_layer(x, batch, seq, g_pre, g_post, mu, w_r, w_k, w_v, w_o, w0, w1, w2, a0, a1, a2, g1, g2,
                k_k, k_a, r_k, lnx_g, lnx_b):
    mu8 = jnp.pad(mu, ((0, 8 - mu.shape[0]), (0, 0)))
    w1_cat = jnp.concatenate([_pad_cols(w1[di], LANES) for di in range(2)], axis=1).astype(BF16)
    a1_cat = jnp.concatenate([_pad_cols(a1[di], LANES) for di in range(2)], axis=1).astype(BF16)
    w2_cat = jnp.concatenate([_pad_rows(w2[di], LANES) for di in range(2)], axis=0).astype(BF16)
    a2_cat = jnp.concatenate([_pad_rows(a2[di], LANES) for di in range(2)], axis=0).astype(BF16)
    par = jnp.stack([k_k, k_a, r_k, w0[0], w0[1], a0[0], a0[1], jnp.zeros_like(k_k)]).astype(F32)
    r, k, v, tw, ta, sg = _rwkv_mix(x, _row(g_pre), mu8, w_r.astype(BF16), w_k.astype(BF16), w_v.astype(BF16),
                                    w1_cat, a1_cat, g1.astype(BF16), seq)
    y0, y1, bonus = _wkv2(r, k, v, tw, ta, w2_cat, a2_cat, par, batch, seq)
    return _rwkv_out(y0, y1, bonus, sg, g2.astype(BF16), _row(lnx_g), _row(lnx_b), w_o.astype(BF16), x,
                     _row(g_post))


def _trunk(x3, norm_g, mla_w_in, mla_g_q, mla_g_kv, mla_w_uq, mla_w_ukv, mla_w_o,
           rwkv_mu, rwkv_w_r, rwkv_w_k, rwkv_w_v, rwkv_w_o, rwkv_w0, rwkv_w1, rwkv_w2,
           rwkv_a0, rwkv_a1, rwkv_a2, rwkv_g1, rwkv_g2, rwkv_k_k, rwkv_k_a, rwkv_r_k,
           rwkv_lnx_g, rwkv_lnx_b, ffn_w_gu, ffn_w_down):
    batch, seq, d = x3.shape
    for tile in (ROWS_WIDE, ATTN_TQ, ATTN_TK, WKV_BLOCK):
        assert seq % min(tile, seq) == 0, (seq, tile)
    assert seq % WKV_CHUNK == 0 and d % (WKV_GROUP * RWKV_HEAD) == 0, (seq, d)
    x = x3.reshape(batch * seq, d)
    depth = norm_g.shape[0]
    for i in range(depth):
        j = i // 2
        if i % 2 == 0:
            x = _mla_layer(x, batch, seq, norm_g[i, 0], norm_g[i, 1], mla_w_in[j], mla_g_q[j], mla_g_kv[j],
                           mla_w_uq[j], mla_w_ukv[j], mla_w_o[j])
        else:
            x = _rwkv_layer(x, batch, seq, norm_g[i, 0], norm_g[i, 1], rwkv_mu[j], rwkv_w_r[j], rwkv_w_k[j],
                            rwkv_w_v[j], rwkv_w_o[j], rwkv_w0[j], rwkv_w1[j], rwkv_w2[j], rwkv_a0[j],
                            rwkv_a1[j], rwkv_a2[j], rwkv_g1[j], rwkv_g2[j], rwkv_k_k[j], rwkv_k_a[j],
                            rwkv_r_k[j].reshape(-1), rwkv_lnx_g[j], rwkv_lnx_b[j])
        w_gu_blocks = ffn_w_gu[i].astype(BF16).reshape(d, -1, FFN_TF).transpose(1, 0, 2)
        x = _ffn(x, _row(norm_g[i, 2]), w_gu_blocks, ffn_w_down[i].astype(BF16), _row(norm_g[i, 3]))
    return x.reshape(batch, seq, d)


def kernel(x_prompt, x_sample, norm_g, mla_w_in, mla_g_q, mla_g_kv, mla_w_uq, mla_w_ukv, mla_w_o, rwkv_mu, rwkv_w_r, rwkv_w_k, rwkv_w_v, rwkv_w_o, rwkv_w0, rwkv_w1, rwkv_w2, rwkv_a0, rwkv_a1, rwkv_a2, rwkv_g1, rwkv_g2, rwkv_k_k, rwkv_k_a, rwkv_r_k, rwkv_lnx_g, rwkv_lnx_b, ffn_w_gu, ffn_w_down):
    params = (norm_g, mla_w_in, mla_g_q, mla_g_kv, mla_w_uq, mla_w_ukv, mla_w_o,
              rwkv_mu, rwkv_w_r, rwkv_w_k, rwkv_w_v, rwkv_w_o, rwkv_w0, rwkv_w1, rwkv_w2,
              rwkv_a0, rwkv_a1, rwkv_a2, rwkv_g1, rwkv_g2, rwkv_k_k, rwkv_k_a, rwkv_r_k,
              rwkv_lnx_g, rwkv_lnx_b, ffn_w_gu, ffn_w_down)
    return (_trunk(x_prompt, *params), _trunk(x_sample, *params))
```

```python
import functools
import math

import jax
import jax.numpy as jnp
from jax import lax
from jax.experimental import pallas as pl
from jax.experimental.pallas import tpu as pltpu

F32 = jnp.float32
BF16 = jnp.bfloat16

NORM_EPS = 1e-6
LNX_EPS = 64e-5
ROPE_THETA = 10000.0

MLA_HEADS = 16
Q_LORA = 512
KV_LORA = 512
NOPE_DIM = 128
ROPE_DIM = 64
V_DIM = 128
QK_PAD = 256
ONES_ROWS = 16
RWKV_HEAD = 64

LANES = 128
VMEM_LIMIT = 56 * 1024 * 1024

ROWS_WIDE = 512
ROWS_NARROW = 256
ATTN_TQ = 2048
ATTN_TK = 2048
ATTN_GROUP = 256
FFN_TF = 512

WKV_CHUNK = 64
WKV_GROUP = 4
WKV_BLOCK = 1024

NT = (((1,), (1,)), ((), ()))
TN = (((0,), (0,)), ((), ()))


def _params(*sem):
    return pltpu.CompilerParams(dimension_semantics=sem, vmem_limit_bytes=VMEM_LIMIT)


def _rms(x, g):
    return x * lax.rsqrt(jnp.mean(x * x, axis=-1, keepdims=True) + NORM_EPS) * g


def _sigmoid(z):
    return 1.0 / (1.0 + jnp.exp(-z))


def _rope_upper(up, cosw, sinw):
    lane = lax.broadcasted_iota(jnp.int32, up.shape, 1)
    swapped = jnp.where(lane < ROPE_DIM // 2, pltpu.roll(up, LANES - ROPE_DIM // 2, 1),
                        pltpu.roll(up, ROPE_DIM // 2, 1))
    return up * cosw + swapped * sinw


def _head_ones(n):
    r = lax.broadcasted_iota(jnp.int32, (n, n), 0) // RWKV_HEAD
    c = lax.broadcasted_iota(jnp.int32, (n, n), 1) // RWKV_HEAD
    return jnp.where(r == c, 1.0, 0.0).astype(BF16)


def _mla_in_kernel(x_ref, g_ref, w_ref, gq_ref, gkv_ref, cos_ref, sin_ref, cq_ref, ckv_ref, kpe_ref):
    xn = _rms(x_ref[...], g_ref[...]).astype(BF16)
    h = jnp.dot(xn, w_ref[...], preferred_element_type=F32)
    cq_ref[...] = _rms(h[:, :Q_LORA], gq_ref[...]).T.astype(BF16)
    ckv_ref[...] = _rms(h[:, Q_LORA:Q_LORA + KV_LORA], gkv_ref[...]).astype(BF16)
    kpe_ref[...] = _rope_upper(h[:, Q_LORA + KV_LORA:], cos_ref[...], sin_ref[...]).astype(BF16)


def _mla_in(x, g, w_in_pad, g_q, g_kv, cosw, sinw, seq):
    m, d = x.shape
    tm = min(ROWS_WIDE, seq)
    nseq = seq // tm
    n = w_in_pad.shape[1]
    row = lambda i: (i, 0)
    fix = lambda i: (0, 0)
    tab = lambda i: (i % nseq, 0)
    return pl.pallas_call(
        _mla_in_kernel,
        grid=(m // tm,),
        in_specs=[pl.BlockSpec((tm, d), row), pl.BlockSpec((1, d), fix), pl.BlockSpec((d, n), fix),
                  pl.BlockSpec((1, Q_LORA), fix), pl.BlockSpec((1, KV_LORA), fix),
                  pl.BlockSpec((tm, LANES), tab), pl.BlockSpec((tm, LANES), tab)],
        out_specs=[pl.BlockSpec((Q_LORA, tm), lambda i: (0, i)), pl.BlockSpec((tm, KV_LORA), row),
                   pl.BlockSpec((tm, LANES), row)],
        out_shape=[jax.ShapeDtypeStruct((Q_LORA, m), BF16), jax.ShapeDtypeStruct((m, KV_LORA), BF16),
                   jax.ShapeDtypeStruct((m, LANES), BF16)],
        compiler_params=_params("parallel"),
        name="mla_in",
    )(x, g, w_in_pad, g_q, g_kv, cosw, sinw)


def _mla_q_kernel(cqt_ref, wt_ref, cos_ref, sin_ref, q_ref, *, scale):
    res = jnp.dot(wt_ref[...], cqt_ref[...], preferred_element_type=F32)
    cos = cos_ref[...]
    sin = sin_ref[...]
    half = ROPE_DIM // 2
    head = NOPE_DIM + ROPE_DIM
    for h in range(MLA_HEADS):
        src = h * head
        dst = h * QK_PAD
        x1 = res[src + NOPE_DIM:src + NOPE_DIM + half]
        x2 = res[src + NOPE_DIM + half:src + head]
        q_ref[dst:dst + NOPE_DIM, :] = (res[src:src + NOPE_DIM] * scale).astype(BF16)
        q_ref[dst + NOPE_DIM:dst + NOPE_DIM + half, :] = ((x1 * cos - x2 * sin) * scale).astype(BF16)
        q_ref[dst + NOPE_DIM + half:dst + head, :] = ((x1 * sin + x2 * cos) * scale).astype(BF16)
        q_ref[dst + head:dst + QK_PAD, :] = jnp.zeros((QK_PAD - head, res.shape[1]), BF16)


def _mla_q(cqt, w_uq_t, cos_t, sin_t, seq):
    c, m = cqt.shape
    n = w_uq_t.shape[0]
    tm = min(ROWS_WIDE, seq)
    nseq = seq // tm
    half = ROPE_DIM // 2
    scale = (NOPE_DIM + ROPE_DIM) ** -0.5 * math.log2(math.e)
    return pl.pallas_call(
        functools.partial(_mla_q_kernel, scale=scale),
        grid=(m // tm,),
        in_specs=[pl.BlockSpec((c, tm), lambda i: (0, i)), pl.BlockSpec((n, c), lambda i: (0, 0)),
                  pl.BlockSpec((half, tm), lambda i: (0, i % nseq)),
                  pl.BlockSpec((half, tm), lambda i: (0, i % nseq))],
        out_specs=pl.BlockSpec((MLA_HEADS * QK_PAD, tm), lambda i: (0, i)),
        out_shape=jax.ShapeDtypeStruct((MLA_HEADS * QK_PAD, m), BF16),
        compiler_params=_params("parallel"),
        name="mla_q",
    )(cqt, w_uq_t, cos_t, sin_t)


def _mla_kv_kernel(ckv_ref, kpe_ref, w_ref, k_ref, vt_ref):
    res = jnp.dot(ckv_ref[...], w_ref[...], preferred_element_type=F32)
    kpe = kpe_ref[...]
    for h in range(MLA_HEADS):
        k_ref[h, :, :NOPE_DIM] = res[:, 2 * h * NOPE_DIM:(2 * h + 1) * NOPE_DIM].astype(BF16)
        k_ref[h, :, NOPE_DIM:] = kpe
        vt_ref[h * V_DIM:(h + 1) * V_DIM, :] = res[:, (2 * h + 1) * V_DIM:(2 * h + 2) * V_DIM].T.astype(BF16)


def _mla_kv(ckv, kpe, w_ukv):
    m, c = ckv.shape
    n = w_ukv.shape[1]
    tm = min(ROWS_WIDE, m)
    return pl.pallas_call(
        _mla_kv_kernel,
        grid=(m // tm,),
        in_specs=[pl.BlockSpec((tm, c), lambda i: (i, 0)), pl.BlockSpec((tm, LANES), lambda i: (i, 0)),
                  pl.BlockSpec((c, n), lambda i: (0, 0))],
        out_specs=[pl.BlockSpec((MLA_HEADS, tm, QK_PAD), lambda i: (0, i, 0)),
                   pl.BlockSpec((n // 2, tm), lambda i: (0, i))],
        out_shape=[jax.ShapeDtypeStruct((MLA_HEADS, m, QK_PAD), BF16), jax.ShapeDtypeStruct((n // 2, m), BF16)],
        compiler_params=_params("parallel"),
        name="mla_kv",
    )(ckv, kpe, w_ukv)


def _attn_kernel(qt_ref, qtn_ref, k0_ref, k_ref, vt_ref, o_ref, sa_sc, sb_sc, xa_sc, xb_sc, m_sc, acc_sc,
                 *, n_split, n_kv):
    i = pl.program_id(2)
    j = pl.program_id(3)
    width = qt_ref.shape[1] // n_split
    last = n_kv - 1
    carry = n_kv % 2 == 0
    buf_a = (sa_sc, xa_sc)
    buf_b = (sb_sc, xb_sc)

    def score(k, q_ref, dst, cols):
        s = jnp.dot(k, q_ref[:, cols], preferred_element_type=F32)
        dst[0][:, cols] = s
        dst[1][:, cols] = jnp.max(s, axis=0, keepdims=True)

    @pl.when(j == 0)
    def _():
        m_sc[...] = jnp.full(m_sc.shape, -jnp.inf, F32)
        acc_sc[...] = jnp.zeros(acc_sc.shape, F32)

    @pl.when((j == 0) & (i == 0) if carry else (j == 0))
    def _():
        k0 = k0_ref[0]
        for c in range(n_split):
            score(k0, qt_ref, buf_a, slice(c * width, (c + 1) * width))

    def step(cur, nxt, k_next_ref, q_next_ref):
        k = k_next_ref[0]
        vt = vt_ref[...]
        vt1 = jnp.concatenate([vt, jnp.ones((ONES_ROWS, vt.shape[1]), BF16)], axis=0)
        for c in range(n_split):
            cols = slice(c * width, (c + 1) * width)
            if nxt is not None:
                score(k, q_next_ref, nxt, cols)
            m_prev = m_sc[:, cols]
            m_new = jnp.maximum(m_prev, cur[1][:, cols])
            alpha = jnp.exp2(m_prev - m_new)
            pt = jnp.exp2((cur[0][:, cols] - m_new).astype(BF16))
            acc_sc[:, cols] = alpha * acc_sc[:, cols] + jnp.dot(vt1, pt, preferred_element_type=F32)
            m_sc[:, cols] = m_new

    @pl.when((j % 2 == 0) & (j < last))
    def _():
        step(buf_a, buf_b, k_ref, qt_ref)

    @pl.when((j % 2 == 1) & (j < last))
    def _():
        step(buf_b, buf_a, k_ref, qt_ref)

    @pl.when(j == last)
    def _():
        if carry:
            step(buf_b, buf_a, k0_ref, qtn_ref)
        else:
            step(buf_a, None, k_ref, qt_ref)
        o_ref[...] = (acc_sc[:V_DIM, :] / acc_sc[V_DIM:V_DIM + 1, :]).T.astype(o_ref.dtype)


def _attention(qt, k, vt, batch, seq):
    m = k.shape[1]
    tq = min(ATTN_TQ, seq)
    tk = min(ATTN_TK, seq)
    nq = seq // tq
    nk = seq // tk
    return pl.pallas_call(
        functools.partial(_attn_kernel, n_split=max(tq // ATTN_GROUP, 1), n_kv=nk),
        grid=(batch, MLA_HEADS, nq, nk),
        in_specs=[pl.BlockSpec((QK_PAD, tq), lambda b, h, i, j: (h, b * nq + i)),
                  pl.BlockSpec((QK_PAD, tq), lambda b, h, i, j: (h, b * nq + jnp.minimum(i + 1, nq - 1))),
                  pl.BlockSpec((1, tk, QK_PAD), lambda b, h, i, j: (h, b * nk, 0)),
                  pl.BlockSpec((1, tk, QK_PAD), lambda b, h, i, j: (h, b * nk + jnp.minimum(j + 1, nk - 1), 0)),
                  pl.BlockSpec((V_DIM, tk), lambda b, h, i, j: (h, b * nk + j))],
        out_specs=pl.BlockSpec((tq, V_DIM), lambda b, h, i, j: (b * nq + i, h)),
        out_shape=jax.ShapeDtypeStruct((m, MLA_HEADS * V_DIM), BF16),
        scratch_shapes=[pltpu.VMEM((tk, tq), F32), pltpu.VMEM((tk, tq), F32),
                        pltpu.VMEM((1, tq), F32), pltpu.VMEM((1, tq), F32),
                        pltpu.VMEM((1, tq), F32), pltpu.VMEM((V_DIM + ONES_ROWS, tq), F32)],
        compiler_params=_params("parallel", "parallel", "arbitrary", "arbitrary"),
        name="mla_attention",
    )(qt, qt, k, k, vt)


def _proj_res_kernel(a_ref, w_ref, x_ref, g_ref, o_ref):
    h = jnp.dot(a_ref[...], w_ref[...], preferred_element_type=F32)
    o_ref[...] = x_ref[...] + _rms(h, g_ref[...])


def _proj_res(a, w, x, g, name):
    m, kdim = a.shape
    n = w.shape[1]
    tm = min(ROWS_WIDE, m)
    return pl.pallas_call(
        _proj_res_kernel,
        grid=(m // tm,),
        in_specs=[pl.BlockSpec((tm, kdim), lambda i: (i, 0)), pl.BlockSpec((kdim, n), lambda i: (0, 0)),
                  pl.BlockSpec((tm, n), lambda i: (i, 0)), pl.BlockSpec((1, n), lambda i: (0, 0))],
        out_specs=pl.BlockSpec((tm, n), lambda i: (i, 0)),
        out_shape=jax.ShapeDtypeStruct((m, n), F32),
        compiler_params=_params("parallel"),
        name=name,
    )(a, w, x, g)


def _ffn_kernel(x_ref, g_pre_ref, wg_ref, wu_ref, wd_ref, g_post_ref, o_ref, xn_sc, acc_sc):
    f = pl.program_id(1)

    @pl.when(f == 0)
    def _():
        xn_sc[...] = _rms(x_ref[...], g_pre_ref[...]).astype(BF16)
        acc_sc[...] = jnp.zeros(acc_sc.shape, F32)

    xn = xn_sc[...]
    half = wg_ref.shape[1] // 2
    acts = []
    for c in range(2):
        cols = slice(c * half, (c + 1) * half)
        gate = jnp.dot(xn, wg_ref[:, cols], preferred_element_type=F32)
        up = jnp.dot(xn, wu_ref[:, cols], preferred_element_type=F32)
        acts.append((gate * _sigmoid(gate) * up).astype(BF16))
    acc_sc[...] += jnp.dot(jnp.concatenate(acts, axis=1), wd_ref[...], preferred_element_type=F32)

    @pl.when(f == pl.num_programs(1) - 1)
    def _():
        o_ref[...] = x_ref[...] + _rms(acc_sc[...], g_post_ref[...])


def _ffn(x, g_pre, w_gu, w_down, g_post):
    m, d = x.shape
    d_ff = w_down.shape[0]
    tm = min(ROWS_WIDE, m)
    tf = FFN_TF
    nf = d_ff // tf
    return pl.pallas_call(
        _ffn_kernel,
        grid=(m // tm, nf),
        in_specs=[pl.BlockSpec((tm, d), lambda i, f: (i, 0)), pl.BlockSpec((1, d), lambda i, f: (0, 0)),
                  pl.BlockSpec((d, tf), lambda i, f: (0, f)), pl.BlockSpec((d, tf), lambda i, f: (0, f + nf)),
                  pl.BlockSpec((tf, d), lambda i, f: (f, 0)), pl.BlockSpec((1, d), lambda i, f: (0, 0))],
        out_specs=pl.BlockSpec((tm, d), lambda i, f: (i, 0)),
        out_shape=jax.ShapeDtypeStruct((m, d), F32),
        scratch_shapes=[pltpu.VMEM((tm, d), BF16), pltpu.VMEM((tm, d), F32)],
        compiler_params=_params("parallel", "arbitrary"),
        name="ffn",
    )(x, g_pre, w_gu, w_gu, w_down, g_post)


def _rwkv_mix_kernel(x_ref, xp_ref, xn_ref, g_ref, mu_ref, wr_ref, wk_ref, wv_ref, w1_ref, a1_ref, g1_ref,
                     r_ref, k_ref, v_ref, tw_ref, ta_ref, sg_ref, *, tiles_per_seq):
    i = pl.program_id(0)
    g = g_ref[...]
    h = _rms(x_ref[...], g)
    tm = h.shape[0]
    first = (i % tiles_per_seq) == 0
    last = (i % tiles_per_seq) == tiles_per_seq - 1
    hp = jnp.where(first, 0.0, _rms(xp_ref[7:8, :], g))
    hn = jnp.where(last, 0.0, _rms(xn_ref[0:1, :], g))
    row = lax.broadcasted_iota(jnp.int32, h.shape, 0)
    h_prev = jnp.where(row == 0, hp, pltpu.roll(h, 1, 0))
    h_next = jnp.where(row == tm - 1, hn, pltpu.roll(h, tm - 1, 0))
    xx = 0.5 * (h_prev + h_next) - h

    def lerp(idx):
        return (h + xx * mu_ref[idx:idx + 1, :]).astype(BF16)

    r_ref[...] = jnp.dot(lerp(0), wr_ref[...], preferred_element_type=F32)
    k_ref[...] = jnp.dot(lerp(2), wk_ref[...], preferred_element_type=F32)
    v_ref[...] = jnp.dot(lerp(3), wv_ref[...], preferred_element_type=F32)
    tw_ref[...] = jnp.tanh(jnp.dot(lerp(1), w1_ref[...], preferred_element_type=F32)).astype(BF16)
    ta_ref[...] = jnp.dot(lerp(4), a1_ref[...], preferred_element_type=F32).astype(BF16)
    sg_ref[...] = _sigmoid(jnp.dot(lerp(5), g1_ref[...], preferred_element_type=F32)).astype(BF16)


def _rwkv_mix(x, g, mu, w_r, w_k, w_v, w1_cat, a1_cat, g1, seq):
    m, d = x.shape
    lr = w1_cat.shape[1]
    tm = min(ROWS_NARROW, seq)
    tps = seq // tm
    nb8 = m // 8
    r8 = tm // 8
    row = lambda i: (i, 0)
    fix = lambda i: (0, 0)
    weight = pl.BlockSpec((d, d), fix, pipeline_mode=pl.Buffered(1))
    weight_lr = pl.BlockSpec((d, lr), fix, pipeline_mode=pl.Buffered(1))
    return pl.pallas_call(
        functools.partial(_rwkv_mix_kernel, tiles_per_seq=tps),
        grid=(m // tm,),
        in_specs=[pl.BlockSpec((tm, d), row),
                  pl.BlockSpec((8, d), lambda i: (jnp.maximum(i * r8 - 1, 0), 0)),
                  pl.BlockSpec((8, d), lambda i: (jnp.minimum((i + 1) * r8, nb8 - 1), 0)),
                  pl.BlockSpec((1, d), fix), pl.BlockSpec((8, d), fix), weight, weight, weight,
                  weight_lr, weight_lr, weight_lr],
        out_specs=[pl.BlockSpec((tm, d), row)] * 3 + [pl.BlockSpec((tm, lr), row)] * 3,
        out_shape=[jax.ShapeDtypeStruct((m, d), F32)] * 3 + [jax.ShapeDtypeStruct((m, lr), BF16)] * 3,
        compiler_params=_params("parallel"),
        name="rwkv_mix",
    )(x, x, x, g, mu, w_r, w_k, w_v, w1_cat, a1_cat, g1)


def _wkv_pre(r_ref, k_ref, v_ref, a_ref, lw_ref, k_k, k_a, ones_bd, blk, head_mask, gl, *, reverse):
    t_len = WKV_CHUNK
    ng = gl // RWKV_HEAD
    gt = ng * t_len
    n_chunks = r_ref.shape[0] // t_len

    wrow = lax.broadcasted_iota(jnp.int32, (t_len, gt), 0)
    wcol = lax.broadcasted_iota(jnp.int32, (t_len, gt), 1) % t_len
    strict = (wcol > wrow) if reverse else (wcol < wrow)
    incl = (wcol >= wrow) if reverse else (wcol <= wrow)
    eye_w = jnp.where(wcol == wrow, 1.0, 0.0)
    trow = lax.broadcasted_iota(jnp.int32, (t_len, t_len), 0)
    tcol = lax.broadcasted_iota(jnp.int32, (t_len, t_len), 1)
    tri = jnp.where((tcol >= trow) if reverse else (tcol <= trow), 1.0, 0.0).astype(BF16)
    bd_mask = jnp.where(lax.broadcasted_iota(jnp.int32, (gt, gt), 0) // t_len
                        == lax.broadcasted_iota(jnp.int32, (gt, gt), 1) // t_len, 1.0, 0.0).astype(BF16)

    def bdiag(w):
        return jnp.concatenate([w.astype(BF16)] * ng, axis=0) * bd_mask

    def mmb(a, b_bf16):
        return jnp.dot(a.astype(BF16), b_bf16, preferred_element_type=F32)

    items = []
    for ci in range(n_chunks):
        c = (n_chunks - 1 - ci) if reverse else ci
        items.append({"rows": slice(c * t_len, (c + 1) * t_len)})

    for it in items:
        lw = lw_ref[it["rows"], :]
        lw_hi = lw.astype(BF16)
        lw_lo = (lw - lw_hi.astype(F32)).astype(BF16)
        it["lw"] = lw
        cum2 = jnp.dot(tri, jnp.concatenate([lw_hi, lw_lo], axis=1), preferred_element_type=F32)
        it["cum"] = cum2[:, :gl] + cum2[:, gl:]
    yield

    kkr_all = k_ref[...] * k_k
    nrm_all = jnp.maximum(jnp.sqrt(jnp.dot((kkr_all * kkr_all).astype(BF16), ones_bd,
                                           preferred_element_type=F32)), 1e-12)
    kk_all = kkr_all / nrm_all

    for it in items:
        rows, cum, lw = it["rows"], it["cum"], it["lw"]
        a = a_ref[rows, :]
        kk = kk_all[rows]
        kd = k_ref[rows, :] * (1.0 + (a - 1.0) * k_a)
        b = kk * a
        tot = cum[0:1] if reverse else cum[t_len - 1:t_len]
        inv_p = jnp.exp(-cum)
        rt = r_ref[rows, :] * jnp.exp(cum)
        kkt = kk * jnp.exp(cum - lw)
        to_end = jnp.exp(tot - cum)
        it["decay"] = jnp.exp(tot)
        it["b_end"] = (b * to_end).astype(BF16)
        it["k_end"] = (kd * to_end).astype(BF16)
        it["rt"] = rt.astype(BF16)
        it["kkt_blk"] = blk(kkt)
        lhs = jnp.concatenate([kkt, rt], axis=0).astype(BF16)
        rhs = jnp.concatenate([blk(b * inv_p), blk(kd * inv_p)], axis=0)
        aa = lax.dot_general(lhs, rhs, NT, preferred_element_type=F32)
        it["a_ab"] = jnp.where(strict, aa[:t_len, :gt], 0.0)
        it["a_ak"] = jnp.where(strict, aa[:t_len, gt:], 0.0).astype(BF16)
        it["a_r"] = jnp.concatenate([jnp.where(incl, aa[t_len:, :gt], 0.0),
                                     jnp.where(incl, aa[t_len:, gt:], 0.0)], axis=1).astype(BF16)
    yield

    for it in items:
        it["x"] = eye_w - it["a_ab"]
        it["p"] = mmb(it["a_ab"], bdiag(it["a_ab"]))
    yield
    rounds = int(math.log2(t_len)) - 1
    for rnd in range(rounds):
        for it in items:
            p_bd = bdiag(it["p"])
            if rnd < rounds - 1:
                xp = mmb(jnp.concatenate([it["x"], it["p"]], axis=0), p_bd)
                it["x"] = it["x"] + xp[:t_len]
                it["p"] = xp[t_len:]
            else:
                it["x"] = it["x"] + mmb(it["x"], p_bd)
        yield

    for it in items:
        v = v_ref[it["rows"], :]
        it["v"] = v
        it["v_blk"] = blk(v)
        it["akv"] = jnp.dot(it["a_ak"], it["v_blk"], preferred_element_type=F32)
        it["x_b"] = it["x"].astype(BF16)
        it["w_k"] = jnp.dot(it["x_b"], it["kkt_blk"], preferred_element_type=F32)
    yield
    for it in items:
        it["u"] = jnp.dot(it["x_b"], blk(it["akv"]), preferred_element_type=F32)
    yield
    for it in items:
        m_full = lax.dot_general(it["w_k"].astype(BF16), it["b_end"], TN, preferred_element_type=F32)
        it["m"] = jnp.where(head_mask, m_full, 0.0).astype(BF16)
    yield
    for it in items:
        c_full = lax.dot_general(jnp.concatenate([-it["u"], it["v"]], axis=0).astype(BF16),
                                 jnp.concatenate([it["b_end"], it["k_end"]], axis=0),
                                 TN, preferred_element_type=F32)
        it["c"] = jnp.where(head_mask, c_full, 0.0)
        it["wr"] = jnp.concatenate([it["w_k"].astype(BF16), it["rt"]], axis=0)
    return items


def _staggered(first, second):
    results = [None, None]
    live = [first, second]
    next(first)
    while any(g is not None for g in live):
        for idx in (1, 0):
            if live[idx] is not None:
                try:
                    next(live[idx])
                except StopIteration as stop:
                    results[idx] = stop.value
                    live[idx] = None
    return results


def _wkv2_kernel(rf_ref, kf_ref, vf_ref, twf_ref, taf_ref, rb_ref, kb_ref, vb_ref, twb_ref, tab_ref,
                 w2_ref, a2_ref, par_ref, yf_ref, yb_ref, bonus_ref, stf_ref, stb_ref):
    @pl.when(pl.program_id(2) == 0)
    def _():
        stf_ref[...] = jnp.zeros(stf_ref.shape, F32)
        stb_ref[...] = jnp.zeros(stb_ref.shape, F32)

    gl = stf_ref.shape[0]
    gt = (gl // RWKV_HEAD) * WKV_CHUNK
    lr = w2_ref.shape[0] // 2
    ones_bd = _head_ones(gl)
    k_k, k_a, r_k = par_ref[0:1, :], par_ref[1:2, :], par_ref[2:3, :]

    def second_stage(t_ref, w_ref, di, bias_row):
        lo = di * lr
        return (jnp.dot(t_ref[:, lo:lo + lr], w_ref[lo:lo + lr, :], preferred_element_type=F32)
                + par_ref[bias_row + di:bias_row + di + 1, :])

    lw_f = -math.exp(-0.5) * _sigmoid(second_stage(twf_ref, w2_ref, 0, 3))
    lw_b = -math.exp(-0.5) * _sigmoid(second_stage(twb_ref, w2_ref, 1, 3))
    a0_f = _sigmoid(second_stage(taf_ref, a2_ref, 0, 5))
    a1_f = _sigmoid(second_stage(taf_ref, a2_ref, 1, 5))
    a1_b = _sigmoid(second_stage(tab_ref, a2_ref, 1, 5))

    kd_sum = kf_ref[...] * (2.0 + (a0_f + a1_f - 2.0) * k_a)
    bonus_ref[...] = jnp.dot((rf_ref[...] * kd_sum * r_k).astype(BF16), ones_bd,
                             preferred_element_type=F32) * vf_ref[...]

    blk_mask = jnp.where(lax.broadcasted_iota(jnp.int32, (gt, gl), 0) // WKV_CHUNK
                         == lax.broadcasted_iota(jnp.int32, (gt, gl), 1) // RWKV_HEAD, 1.0, 0.0).astype(BF16)
    head_mask = (lax.broadcasted_iota(jnp.int32, (gl, gl), 0) // RWKV_HEAD
                 == lax.broadcasted_iota(jnp.int32, (gl, gl), 1) // RWKV_HEAD)

    def blk(x):
        return jnp.concatenate([x.astype(BF16)] * (gl // RWKV_HEAD), axis=0) * blk_mask

    t_len = WKV_CHUNK
    items_f, items_b = _staggered(
        _wkv_pre(rf_ref, kf_ref, vf_ref, a0_f, lw_f, k_k, k_a, ones_bd, blk, head_mask, gl, reverse=False),
        _wkv_pre(rb_ref, kb_ref, vb_ref, a1_b, lw_b, k_k, k_a, ones_bd, blk, head_mask, gl, reverse=True))
    chains = [(items_f, yf_ref, stf_ref), (items_b, yb_ref, stb_ref)]
    states = [st_ref[...] for _, _, st_ref in chains]

    def emit_y(y_ref, it, sa, rs):
        y_ref[it["rows"], :] = rs + jnp.dot(it["a_r"], jnp.concatenate([blk(sa), it["v_blk"]], axis=0),
                                            preferred_element_type=F32)

    pending = []
    for ci in range(len(chains[0][0])):
        st_bf = [st.astype(BF16) for st in states]
        its = [items[ci] for items, _, _ in chains]
        st_m = [jnp.dot(sb, it["m"], preferred_element_type=F32) for sb, it in zip(st_bf, its)]
        ws = [lax.dot_general(it["wr"], sb, NT, preferred_element_type=F32) for sb, it in zip(st_bf, its)]
        for args in pending:
            emit_y(*args)
        pending = [(y_ref, it, -(w[:t_len] + it["u"]), w[t_len:])
                   for (_, y_ref, _), it, w in zip(chains, its, ws)]
        states = [st * it["decay"] - sm + it["c"] for st, it, sm in zip(states, its, st_m)]
    for args in pending:
        emit_y(*args)
    for (_, _, st_ref), st in zip(chains, states):
        st_ref[...] = st


def _wkv2(r, k, v, tw, ta, w2_cat, a2_cat, par, batch, seq):
    m, d = r.shape
    gl = WKV_GROUP * RWKV_HEAD
    lr2 = tw.shape[1]
    tb = min(WKV_BLOCK, seq)
    nb = seq // tb
    fwd = pl.BlockSpec((tb, gl), lambda bi, g, j: (bi * nb + j, g))
    bwd = pl.BlockSpec((tb, gl), lambda bi, g, j: (bi * nb + nb - 1 - j, g))
    fwd_lr = pl.BlockSpec((tb, lr2), lambda bi, g, j: (bi * nb + j, 0))
    bwd_lr = pl.BlockSpec((tb, lr2), lambda bi, g, j: (bi * nb + nb - 1 - j, 0))
    col = lambda rows: pl.BlockSpec((rows, gl), lambda bi, g, j: (0, g))
    out = jax.ShapeDtypeStruct((m, d), F32)
    return pl.pallas_call(
        _wkv2_kernel,
        grid=(batch, d // gl, nb),
        in_specs=[fwd] * 3 + [fwd_lr] * 2 + [bwd] * 3 + [bwd_lr] * 2 + [col(lr2), col(lr2), col(par.shape[0])],
        out_specs=[fwd, bwd, fwd],
        out_shape=[out, out, out],
        scratch_shapes=[pltpu.VMEM((gl, gl), F32), pltpu.VMEM((gl, gl), F32)],
        compiler_params=_params("parallel", "parallel", "arbitrary"),
        name="wkv",
    )(r, k, v, tw, ta, r, k, v, tw, ta, w2_cat, a2_cat, par)


def _rwkv_out_kernel(y0_ref, y1_ref, bonus_ref, sg_ref, g2_ref, lg_ref, lb_ref, wo_ref, x_ref, g_ref, o_ref):
    width = 2 * LANES
    ones_bd = _head_ones(width)
    d = y0_ref.shape[1]
    inv_n = 1.0 / RWKV_HEAD

    def head_sum(z):
        return jnp.dot(z.astype(BF16), ones_bd, preferred_element_type=F32)

    gate = jnp.dot(sg_ref[...], g2_ref[...], preferred_element_type=F32)
    gated = []
    for c in range(d // width):
        sl = slice(c * width, (c + 1) * width)
        y = y0_ref[:, sl] + y1_ref[:, sl]
        mean = head_sum(y) * inv_n
        yc = y - mean
        var = head_sum(yc * yc) * inv_n
        yn = yc * lax.rsqrt(var + LNX_EPS) * lg_ref[:, sl] + lb_ref[:, sl]
        gated.append(((yn + bonus_ref[:, sl]) * gate[:, sl]).astype(BF16))
    h = jnp.dot(jnp.concatenate(gated, axis=1), wo_ref[...], preferred_element_type=F32)
    o_ref[...] = x_ref[...] + _rms(h, g_ref[...])


def _rwkv_out(y0, y1, bonus, sg, g2, lnx_g, lnx_b, w_o, x, g_post):
    m, d = y0.shape
    lr = sg.shape[1]
    tm = min(ROWS_NARROW, m)
    row = pl.BlockSpec((tm, d), lambda i: (i, 0))
    par = pl.BlockSpec((1, d), lambda i: (0, 0))
    fix = lambda i: (0, 0)
    return pl.pallas_call(
        _rwkv_out_kernel,
        grid=(m // tm,),
        in_specs=[row] * 3 + [pl.BlockSpec((tm, lr), lambda i: (i, 0)),
                              pl.BlockSpec((lr, d), fix, pipeline_mode=pl.Buffered(1)), par, par,
                              pl.BlockSpec((d, d), fix, pipeline_mode=pl.Buffered(1)), row, par],
        out_specs=row,
        out_shape=jax.ShapeDtypeStruct((m, d), F32),
        compiler_params=_params("parallel"),
        name="rwkv_out",
    )(y0, y1, bonus, sg, g2, lnx_g, lnx_b, w_o, x, g_post)


def _rope_tables(seq):
    half = ROPE_DIM // 2
    inv = 1.0 / (ROPE_THETA ** (jnp.arange(half, dtype=F32) * (2.0 / ROPE_DIM)))
    ang = jnp.arange(seq, dtype=F32)[:, None] * inv[None, :]
    cos, sin = jnp.cos(ang), jnp.sin(ang)
    zero = jnp.zeros((seq, LANES - ROPE_DIM), F32)
    return (jnp.concatenate([cos, cos, zero], axis=1), jnp.concatenate([-sin, sin, zero], axis=1), cos.T, sin.T)


def _row(v):
    return v.reshape(1, -1).astype(F32)


def _pad_cols(w, n):
    return jnp.pad(w, ((0, 0), (0, n - w.shape[1])))


def _pad_rows(w, n):
    return jnp.pad(w, ((0, n - w.shape[0]), (0, 0)))


def _mla_layer(x, batch, seq, g_pre, g_post, w_in, g_q, g_kv, w_uq, w_ukv, w_o):
    cosw, sinw, cos_t, sin_t = _rope_tables(seq)
    w_in_pad = _pad_cols(w_in, Q_LORA + KV_LORA + LANES).astype(BF16)
    cqt, ckv, kpe = _mla_in(x, _row(g_pre), w_in_pad, _row(g_q), _row(g_kv), cosw, sinw, seq)
    qt = _mla_q(cqt, w_uq.T.astype(BF16), cos_t, sin_t, seq)
    k, vt = _mla_kv(ckv, kpe, w_ukv.astype(BF16))
    o = _attention(qt, k, vt, batch, seq)
    return _proj_res(o, w_o.astype(BF16), x, _row(g_post), "mla_out")


def _rwkv_layer(x, batch, seq, g_pre, g_post, mu, w_r, w_k, w_v, w_o, w0, w1, w2, a0, a1, a2, g1, g2,
                k_k, k_a, r_k, lnx_g, lnx_b):
    mu8 = jnp.pad(mu, ((0, 8 - mu.shape[0]), (0, 0)))
    w1_cat = jnp.concatenate([_pad_cols(w1[di], LANES) for di in range(2)], axis=1).astype(BF16)
    a1_cat = jnp.concatenate([_pad_cols(a1[di], LANES) for di in range(2)], axis=1).astype(BF16)
    w2_cat = jnp.concatenate([_pad_rows(w2[di], LANES) for di in range(2)], axis=0).astype(BF16)
    a2_cat = jnp.concatenate([_pad_rows(a2[di], LANES) for di in range(2)], axis=0).astype(BF16)
    par = jnp.stack([k_k, k_a, r_k, w0[0], w0[1], a0[0], a0[1], jnp.zeros_like(k_k)]).astype(F32)
    r, k, v, tw, ta, sg = _rwkv_mix(x, _row(g_pre), mu8, w_r.astype(BF16), w_k.astype(BF16), w_v.astype(BF16),
                                    w1_cat, a1_cat, g1.astype(BF16), seq)
    y0, y1, bonus = _wkv2(r, k, v, tw, ta, w2_cat, a2_cat, par, batch, seq)
    return _rwkv_out(y0, y1, bonus, sg, g2.astype(BF16), _row(lnx_g), _row(lnx_b), w_o.astype(BF16), x,
                     _row(g_post))


def _trunk(x3, norm_g, mla_w_in, mla_g_q, mla_g_kv, mla_w_uq, mla_w_ukv, mla_w_o,
           rwkv_mu, rwkv_w_r, rwkv_w_k, rwkv_w_v, rwkv_w_o, rwkv_w0, rwkv_w1, rwkv_w2,
           rwkv_a0, rwkv_a1, rwkv_a2, rwkv_g1, rwkv_g2, rwkv_k_k, rwkv_k_a, rwkv_r_k,
           rwkv_lnx_g, rwkv_lnx_b, ffn_w_gu, ffn_w_down):
    batch, seq, d = x3.shape
    for tile in (ROWS_WIDE, ATTN_TQ, ATTN_TK, WKV_BLOCK):
        assert seq % min(tile, seq) == 0, (seq, tile)
    assert seq % WKV_CHUNK == 0 and d % (WKV_GROUP * RWKV_HEAD) == 0, (seq, d)
    x = x3.reshape(batch * seq, d)
    depth = norm_g.shape[0]
    for i in range(depth):
        j = i // 2
        if i % 2 == 0:
            x = _mla_layer(x, batch, seq, norm_g[i, 0], norm_g[i, 1], mla_w_in[j], mla_g_q[j], mla_g_kv[j],
                           mla_w_uq[j], mla_w_ukv[j], mla_w_o[j])
        else:
            x = _rwkv_layer(x, batch, seq, norm_g[i, 0], norm_g[i, 1], rwkv_mu[j], rwkv_w_r[j], rwkv_w_k[j],
                            rwkv_w_v[j], rwkv_w_o[j], rwkv_w0[j], rwkv_w1[j], rwkv_w2[j], rwkv_a0[j],
                            rwkv_a1[j], rwkv_a2[j], rwkv_g1[j], rwkv_g2[j], rwkv_k_k[j], rwkv_k_a[j],
                            rwkv_r_k[j].reshape(-1), rwkv_lnx_g[j], rwkv_lnx_b[j])
        x = _ffn(x, _row(norm_g[i, 2]), ffn_w_gu[i].astype(BF16), ffn_w_down[i].astype(BF16), _row(norm_g[i, 3]))
    return x.reshape(batch, seq, d)


def kernel(x_prompt, x_sample, norm_g, mla_w_in, mla_g_q, mla_g_kv, mla_w_uq, mla_w_ukv, mla_w_o, rwkv_mu, rwkv_w_r, rwkv_w_k, rwkv_w_v, rwkv_w_o, rwkv_w0, rwkv_w1, rwkv_w2, rwkv_a0, rwkv_a1, rwkv_a2, rwkv_g1, rwkv_g2, rwkv_k_k, rwkv_k_a, rwkv_r_k, rwkv_lnx_g, rwkv_lnx_b, ffn_w_gu, ffn_w_down):
    params = (norm_g, mla_w_in, mla_g_q, mla_g_kv, mla_w_uq, mla_w_ukv, mla_w_o,
              rwkv_mu, rwkv_w_r, rwkv_w_k, rwkv_w_v, rwkv_w_o, rwkv_w0, rwkv_w1, rwkv_w2,
              rwkv_a0, rwkv_a1, rwkv_a2, rwkv_g1, rwkv_g2, rwkv_k_k, rwkv_k_a, rwkv_r_k,
              rwkv_lnx_g, rwkv_lnx_b, ffn_w_gu, ffn_w_down)
    return (_trunk(x_prompt, *params), _trunk(x_sample, *params))
```
